```python
import jax
import jax.numpy as jnp
from jax import lax
import numpy as np

D_MODEL = 1024
BATCH = 8
SEQ = 2048
DEPTH = 1
DEC_BATCH = 32
DEC_SEQ = 8
PAST_LEN = 16384
PAGE_SIZE = 128

H_M = 4
DH_M = D_MODEL // (2 * H_M)
W_M = H_M * DH_M
H_F = 8
DH_F = D_MODEL // (2 * H_F)
W_F = H_F * DH_F
D_MIX = W_M + W_F
N_IN = 4 * W_M + 2 * H_M + 3 * W_F + H_F
D_FF = -(-8 * D_MODEL // (3 * 256)) * 256
CHUNK = 128
Q_BLOCK = 128
FOX_F_BIAS = 4.0
EPS = 1e-6

kernel_name = 'hymba_mlstm_fox_decode_step'


def _rmsnorm(x, g):
    x32 = x.astype(jnp.float32)
    y = x32 * lax.rsqrt(jnp.mean(x32 * x32, axis=-1, keepdims=True) + EPS)
    return (y * g.astype(jnp.float32)).astype(x.dtype)


def _project(x, g, w_in, b_mi, b_mf, b_ff):
    B, L = x.shape[0], x.shape[1]
    h = _rmsnorm(x, g)
    proj = jnp.matmul(h, w_in).astype(jnp.float32)
    sizes = (W_M, W_M, W_M, W_M, H_M, H_M, W_F, W_F, W_F, H_F)
    cuts = [int(c) for c in np.cumsum(sizes)[:-1]]
    q_m, k_m, v_m, o_m, i_m, f_m, q_f, k_f, v_f, f_f = jnp.split(proj, cuts, axis=-1)
    mh = lambda a: a.reshape(B, L, H_M, DH_M)
    fh = lambda a: a.reshape(B, L, H_F, DH_F)
    mlstm = (mh(q_m), mh(k_m) * (DH_M ** -0.5), mh(v_m),
             i_m + b_mi.astype(jnp.float32),
             jax.nn.log_sigmoid(f_m + b_mf.astype(jnp.float32)))
    fox = (fh(q_f) * (DH_F ** -0.5), fh(k_f), fh(v_f),
           jax.nn.log_sigmoid(f_f + b_ff.astype(jnp.float32)))
    return mlstm, o_m, fox


def _mlstm_chunk(state, inp):
    C, n, m = state
    q, k, v, logi, logf = inp
    L = q.shape[1]
    b = jnp.swapaxes(jnp.cumsum(logf, axis=1), 1, 2)
    li = jnp.swapaxes(logi, 1, 2)
    causal = jnp.tril(jnp.ones((L, L), dtype=bool))
    D = jnp.where(causal, b[..., :, None] - b[..., None, :] + li[..., None, :], -jnp.inf)
    inter = m[..., None] + b
    m_t = jnp.maximum(inter, jnp.max(D, axis=-1))
    S = jnp.einsum('bthd,bshd->bhts', q, k) * jnp.exp(D - m_t[..., None])
    w_inter = jnp.exp(inter - m_t)
    num = w_inter[..., None] * jnp.einsum('bhvk,bthk->bhtv', C, q) + jnp.einsum('bhts,bshv->bhtv', S, v)
    den = w_inter * jnp.einsum('bhk,bthk->bht', n, q) + jnp.sum(S, axis=-1)
    h = num / jnp.maximum(jnp.abs(den), jnp.exp(-m_t))[..., None]
    b_last = b[..., -1]
    m_new = m_t[..., -1]
    decay = jnp.exp(m + b_last - m_new)
    w_s = jnp.swapaxes(jnp.exp(li + b_last[..., None] - b - m_new[..., None]), 1, 2)
    C_new = decay[..., None, None] * C + jnp.einsum('bshv,bshk->bhvk', v * w_s[..., None], k)
    n_new = decay[..., None] * n + jnp.einsum('bsh,bshk->bhk', w_s, k)
    return (C_new, n_new, m_new), jnp.transpose(h, (0, 2, 1, 3))


def _mlstm_prompt(q, k, v, logi, logf):
    B, S = q.shape[0], q.shape[1]
    n_chunks = S // CHUNK
    to_chunks = lambda a: jnp.swapaxes(a.reshape((B, n_chunks, CHUNK) + a.shape[2:]), 0, 1)
    state0 = (jnp.zeros((B, H_M, DH_M, DH_M), jnp.float32),
              jnp.zeros((B, H_M, DH_M), jnp.float32),
              jnp.zeros((B, H_M), jnp.float32))
    state, h = lax.scan(_mlstm_chunk, state0,
                        (to_chunks(q), to_chunks(k), to_chunks(v), to_chunks(logi), to_chunks(logf)))
    return jnp.swapaxes(h, 0, 1).reshape(B, S, H_M, DH_M), state


def _fox_logits(q, cq, qpos, k, ck, kpos):
    s = jnp.einsum('bqhd,bkhd->bhqk', q, k)
    bias = jnp.swapaxes(cq, 1, 2)[..., :, None] - jnp.swapaxes(ck, 1, 2)[..., None, :]
    mask = kpos[None, :] <= qpos[:, None]
    return jnp.where(mask, s + bias, -jnp.inf)


def _fox_prompt(q, k, v, logf):
    B, S = q.shape[0], q.shape[1]
    nb = S // Q_BLOCK
    c = jnp.cumsum(logf, axis=1)
    pos = jnp.arange(S)
    qb = jnp.swapaxes(q.reshape(B, nb, Q_BLOCK, H_F, DH_F), 0, 1)
    cb = jnp.swapaxes(c.reshape(B, nb, Q_BLOCK, H_F), 0, 1)
    pb = pos.reshape(nb, Q_BLOCK)

    def block(args):
        q_i, c_i, p_i = args
        p = jax.nn.softmax(_fox_logits(q_i, c_i, p_i, k, c, pos), axis=-1)
        return jnp.einsum('bhqk,bkhd->bqhd', p, v)

    o = lax.map(block, (qb, cb, pb))
    return jnp.swapaxes(o, 0, 1).reshape(B, S, H_F, DH_F)


def _fox_sample(q, k, v, logf, k_pool, v_pool, logf_pool, page_table):
    B, T = q.shape[0], q.shape[1]
    P = page_table.shape[1] * PAGE_SIZE
    k_past = k_pool[page_table].reshape(B, P, H_F, DH_F).astype(jnp.float32)
    v_past = v_pool[page_table].reshape(B, P, H_F, DH_F).astype(jnp.float32)
    lf_past = logf_pool[page_table].reshape(B, P, H_F).astype(jnp.float32)
    c_past = lf_past - lax.cumsum(lf_past, axis=1, reverse=True)
    c_new = jnp.cumsum(logf, axis=1)
    pos_past = jnp.arange(P)
    pos_new = P + jnp.arange(T)
    s = jnp.concatenate([_fox_logits(q, c_new, pos_new, k_past, c_past, pos_past),
                         _fox_logits(q, c_new, pos_new, k, c_new, pos_new)], axis=-1)
    p = jax.nn.softmax(s, axis=-1)
    return (jnp.einsum('bhqk,bkhd->bqhd', p[..., :P], v_past)
            + jnp.einsum('bhqk,bkhd->bqhd', p[..., P:], v))


def _merge_ffn(x, h_m, o_m, h_f, head_g, w_out, g_ffn, w_gate, w_up, w_down):
    B, L = x.shape[0], x.shape[1]
    hm = h_m * lax.rsqrt(jnp.mean(h_m * h_m, axis=-1, keepdims=True) + EPS)
    hm = hm * head_g.astype(jnp.float32).reshape(H_M, DH_M)
    hm = hm.reshape(B, L, W_M) * jax.nn.sigmoid(o_m)
    mix = jnp.concatenate([hm, h_f.reshape(B, L, W_F)], axis=-1).astype(x.dtype)
    x = x + jnp.matmul(mix, w_out)
    h = _rmsnorm(x, g_ffn)
    return x + jnp.matmul(jax.nn.silu(jnp.matmul(h, w_gate)) * jnp.matmul(h, w_up), w_down)


def setup_inputs(seed: int = 0) -> dict:
    key = jax.random.key(seed)
    ks = jax.random.split(key, 24)
    n_pages = PAST_LEN // PAGE_SIZE
    n_pool = (DEC_BATCH * n_pages * 5) // 4
    nrm = lambda kk, shape, s=1.0: s * jax.random.normal(kk, shape, jnp.float32)
    return {
        'x_prompt': nrm(ks[0], (BATCH, SEQ, D_MODEL)),
        'x_sample': nrm(ks[1], (DEC_BATCH, DEC_SEQ, D_MODEL)),
        'cache_fox_k': nrm(ks[2], (DEPTH, n_pool, PAGE_SIZE, H_F, DH_F)),
        'cache_fox_v': nrm(ks[3], (DEPTH, n_pool, PAGE_SIZE, H_F, DH_F)),
        'cache_fox_logf': jax.nn.log_sigmoid(nrm(ks[4], (DEPTH, n_pool, PAGE_SIZE, H_F)) + FOX_F_BIAS),
        'page_table': jax.random.permutation(ks[5], n_pool)[: DEC_BATCH * n_pages].reshape(DEC_BATCH, n_pages).astype(jnp.int32),
        'state_mlstm_C': nrm(ks[6], (DEPTH, DEC_BATCH, H_M, DH_M, DH_M), 0.5),
        'state_mlstm_n': nrm(ks[7], (DEPTH, DEC_BATCH, H_M, DH_M), 0.5),
        'state_mlstm_m': nrm(ks[8], (DEPTH, DEC_BATCH, H_M)),
        'norm_mix_g': 1.0 + nrm(ks[9], (DEPTH, D_MODEL), 0.01),
        'w_in': nrm(ks[10], (DEPTH, D_MODEL, N_IN), D_MODEL ** -0.5),
        'b_m_igate': nrm(ks[11], (DEPTH, H_M), 0.1),
        'b_m_fgate': jnp.linspace(3.0, 6.0, H_M)[None, :] + nrm(ks[12], (DEPTH, H_M), 0.1),
        'b_f_fgate': FOX_F_BIAS + nrm(ks[13], (DEPTH, H_F), 0.1),
        'mlstm_head_g': 1.0 + nrm(ks[14], (DEPTH, W_M), 0.01),
        'w_out': nrm(ks[15], (DEPTH, D_MIX, D_MODEL), D_MIX ** -0.5),
        'norm_ffn_g': 1.0 + nrm(ks[16], (DEPTH, D_MODEL), 0.01),
        'w_gate': nrm(ks[17], (DEPTH, D_MODEL, D_FF), D_MODEL ** -0.5),
        'w_up': nrm(ks[18], (DEPTH, D_MODEL, D_FF), D_MODEL ** -0.5),
        'w_down': nrm(ks[19], (DEPTH, D_FF, D_MODEL), D_FF ** -0.5),
        'norm_final_g': 1.0 + nrm(ks[20], (D_MODEL,), 0.01),
    }


def reference(x_prompt, x_sample, cache_fox_k, cache_fox_v, cache_fox_logf, page_table,
              state_mlstm_C, state_mlstm_n, state_mlstm_m, norm_mix_g, w_in, b_m_igate, b_m_fgate,
              b_f_fgate, mlstm_head_g, w_out, norm_ffn_g, w_gate, w_up, w_down, norm_final_g):
    xp, xs = x_prompt, x_sample
    pk, pv, plf, pC, pn, pm = [], [], [], [], [], []
    sk, sv, slf, sC, sn, sm = [], [], [], [], [], []
    for l in range(DEPTH):
        (q, k, v, li, lf), o_m, (qf, kf, vf, lff) = _project(
            xp, norm_mix_g[l], w_in[l], b_m_igate[l], b_m_fgate[l], b_f_fgate[l])
        h_m, (C, n, m) = _mlstm_prompt(q, k, v, li, lf)
        h_f = _fox_prompt(qf, kf, vf, lff)
        xp = _merge_ffn(xp, h_m, o_m, h_f, mlstm_head_g[l], w_out[l], norm_ffn_g[l],
                        w_gate[l], w_up[l], w_down[l])
        pk.append(kf); pv.append(vf); plf.append(lff); pC.append(C); pn.append(n); pm.append(m)
        (q, k, v, li, lf), o_m, (qf, kf, vf, lff) = _project(
            xs, norm_mix_g[l], w_in[l], b_m_igate[l], b_m_fgate[l], b_f_fgate[l])
        state = (state_mlstm_C[l].astype(jnp.float32), state_mlstm_n[l].astype(jnp.float32),
                 state_mlstm_m[l].astype(jnp.float32))
        (C, n, m), h_m = _mlstm_chunk(state, (q, k, v, li, lf))
        h_f = _fox_sample(qf, kf, vf, lff, cache_fox_k[l], cache_fox_v[l], cache_fox_logf[l], page_table)
        xs = _merge_ffn(xs, h_m, o_m, h_f, mlstm_head_g[l], w_out[l], norm_ffn_g[l],
                        w_gate[l], w_up[l], w_down[l])
        sk.append(kf); sv.append(vf); slf.append(lff); sC.append(C); sn.append(n); sm.append(m)
    y_prompt = _rmsnorm(xp, norm_final_g)
    y_sample = _rmsnorm(xs, norm_final_g)
    st = lambda a, ref: jnp.stack(a, axis=0).astype(ref.dtype)
    return (y_prompt, y_sample,
            st(pk, cache_fox_k), st(pv, cache_fox_v), st(plf, cache_fox_logf),
            st(pC, state_mlstm_C), st(pn, state_mlstm_n), st(pm, state_mlstm_m),
            st(sk, cache_fox_k), st(sv, cache_fox_v), st(slf, cache_fox_logf),
            st(sC, state_mlstm_C), st(sn, state_mlstm_n), st(sm, state_mlstm_m))
```

```python
import functools

import jax
import jax.numpy as jnp
from jax import lax
from jax.experimental import pallas as pl
from jax.experimental.pallas import tpu as pltpu

F32 = jnp.float32
BF16 = jnp.bfloat16
HI = lax.Precision.HIGHEST
NT = (((1,), (1,)), ((), ()))
TN = (((0,), (0,)), ((), ()))

EPS = 1e-6
H_M = 4
DH_M = 128
H_F = 8
DH_F = 64
W_HEADS = 512
N_GATE_ROWS = 16
LANES = 128
MIB = 1024 * 1024

PROJ_ROWS = 512
FFN_ROWS = 512
MLSTM_BATCH = 8
FOX_BLOCK = 256
PAGES_PER_GROUP = 8


def _rms(x, g):
    return x * lax.rsqrt(jnp.mean(x * x, axis=-1, keepdims=True) + EPS) * g


def _log_sigmoid(x):
    return jnp.minimum(x, 0.0) - jnp.log1p(jnp.exp(-jnp.abs(x)))


def _sigmoid(x):
    return 1.0 / (1.0 + jnp.exp(-x))


def _const_spec(shape):
    return pl.BlockSpec(shape, lambda *_: (0,) * len(shape), pipeline_mode=pl.Buffered(1))


def _proj_kernel(x_ref, g_ref, wt_ref, wkv_ref, wg_ref, bcol_ref, brow_ref, *outs, kv_transposed):
    if kv_transposed:
        qm_ref, km_ref, vm_ref, om_ref, qf_ref, kf_ref, vf_ref, kfb_ref, vfb_ref, gcol_ref, grow_ref = outs
    else:
        qm_ref, km_ref, vm_ref, om_ref, qf_ref, kf_ref, vf_ref, gcol_ref, grow_ref = outs
    h = _rms(x_ref[...], g_ref[...]).astype(BF16)
    rows = h.shape[0]

    def mm(i):
        w = wt_ref[i * W_HEADS:(i + 1) * W_HEADS, :]
        return lax.dot_general(h, w, NT, preferred_element_type=F32)

    qm_ref[...] = mm(0).astype(qm_ref.dtype)
    km_ref[...] = (mm(1) * (DH_M ** -0.5)).astype(km_ref.dtype)
    vm_ref[...] = mm(2).astype(vm_ref.dtype)
    om_ref[...] = mm(3).astype(om_ref.dtype)
    qf_ref[...] = (mm(4) * (DH_F ** -0.5)).astype(qf_ref.dtype)
    if kv_transposed:
        kt = lax.dot_general(wkv_ref[0:W_HEADS, :], h, NT, preferred_element_type=F32)
        kf_ref[0] = kt
        kfb_ref[0] = kt.astype(BF16)
        vt = lax.dot_general(wkv_ref[W_HEADS:2 * W_HEADS, :], h, NT, preferred_element_type=F32)
        vf_ref[0] = vt
        vfb_ref[0] = vt.astype(BF16)
    else:
        kf_ref[...] = lax.dot_general(h, wkv_ref[0:W_HEADS, :], NT, preferred_element_type=F32)
        vf_ref[...] = lax.dot_general(h, wkv_ref[W_HEADS:2 * W_HEADS, :], NT, preferred_element_type=F32)
    pre_c = lax.dot_general(h, wg_ref[...], NT, preferred_element_type=F32) + bcol_ref[...]
    lane = lax.broadcasted_iota(jnp.int32, (rows, LANES), 1)
    gcol_ref[...] = jnp.where(lane < H_M, pre_c, _log_sigmoid(pre_c))
    pre_r = lax.dot_general(wg_ref[0:N_GATE_ROWS, :], h, NT, preferred_element_type=F32) + brow_ref[...]
    row = lax.broadcasted_iota(jnp.int32, (N_GATE_ROWS, rows), 0)
    grow_ref[...] = jnp.where(row < H_M, pre_r, _log_sigmoid(pre_r))


def _project(x2d, g, wt_main, wt_kv, wt_g, bcol, brow, *, batch, seq, kv_transposed, act_dtype):
    rows_total, d_model = x2d.shape
    tm = min(PROJ_ROWS, rows_total)
    steps = rows_total // tm
    per_seq = max(seq // tm, 1)
    row_spec = lambda w: pl.BlockSpec((tm, w), lambda i: (i, 0))
    in_specs = [row_spec(d_model), _const_spec(g.shape), _const_spec(wt_main.shape), _const_spec(wt_kv.shape),
                _const_spec(wt_g.shape), _const_spec(bcol.shape), _const_spec(brow.shape)]
    act = jax.ShapeDtypeStruct((rows_total, W_HEADS), act_dtype)
    act32 = jax.ShapeDtypeStruct((rows_total, W_HEADS), F32)
    out_shape = [act, act, act, act32, act]
    out_specs = [row_spec(W_HEADS)] * 5
    if kv_transposed:
        kv_spec = pl.BlockSpec((1, W_HEADS, tm), lambda i: (i // per_seq, 0, i % per_seq))
        out_shape += [jax.ShapeDtypeStruct((batch, W_HEADS, seq), F32)] * 2
        out_shape += [jax.ShapeDtypeStruct((batch, W_HEADS, seq), BF16)] * 2
        out_specs += [kv_spec] * 4
    else:
        out_shape += [act32, act32]
        out_specs += [row_spec(W_HEADS)] * 2
    out_shape += [jax.ShapeDtypeStruct((rows_total, LANES), F32),
                  jax.ShapeDtypeStruct((N_GATE_ROWS, rows_total), F32)]
    out_specs += [row_spec(LANES), pl.BlockSpec((N_GATE_ROWS, tm), lambda i: (0, i))]
    return pl.pallas_call(
        functools.partial(_proj_kernel, kv_transposed=kv_transposed),
        grid=(steps,),
        in_specs=in_specs,
        out_specs=out_specs,
        out_shape=out_shape,
        compiler_params=pltpu.CompilerParams(dimension_semantics=("arbitrary",), vmem_limit_bytes=48 * MIB),
        name="proj",
    )(x2d, g, wt_main, wt_kv, wt_g, bcol, brow)


def _mlstm_kernel(q_ref, k_ref, v_ref, om_ref, gc_ref, gr_ref, hg_ref, c0_ref, n0_ref, m0_ref,
                  hm_ref, c_ref, n_ref, m_ref, *, bb, chunk):
    @pl.when(pl.program_id(1) == 0)
    def _():
        c_ref[...] = c0_ref[...]
        n_ref[...] = n0_ref[...]
        m_ref[...] = m0_ref[...]

    t_idx = lax.broadcasted_iota(jnp.int32, (chunk, chunk), 0)
    s_idx = lax.broadcasted_iota(jnp.int32, (chunk, chunk), 1)
    causal = s_idx <= t_idx
    tril = causal.astype(F32)
    triu = (t_idx <= s_idx).astype(F32)

    def per_batch(b, carry):
        gc = gc_ref[b]
        gr = gr_ref[b]
        bc = jnp.dot(tril, gc, precision=HI, preferred_element_type=F32)
        br = jnp.dot(gr, triu, precision=HI, preferred_element_type=F32)
        q = q_ref[b]
        k = k_ref[b]
        v = v_ref[b]
        om = om_ref[b]
        for h in range(H_M):
            sl = slice(h * DH_M, (h + 1) * DH_M)
            li_c = gc[:, h:h + 1]
            li_r = gr[h:h + 1, :]
            b_c = bc[:, H_M + h:H_M + h + 1]
            b_r = br[H_M + h:H_M + h + 1, :]
            m_prev = m_ref[pl.ds(b, 1), h:h + 1]
            dmat = jnp.where(causal, b_c - b_r + li_r, -jnp.inf)
            inter = m_prev + b_c
            m_t = jnp.maximum(inter, jnp.max(dmat, axis=-1, keepdims=True))
            qh = q[:, sl].astype(BF16)
            kh = k[:, sl].astype(BF16)
            vh = v[:, sl].astype(BF16)
            smat = lax.dot_general(qh, kh, NT, preferred_element_type=F32) * jnp.exp(dmat - m_t)
            w_inter = jnp.exp(inter - m_t)
            c_prev = c_ref[b, h]
            cq = lax.dot_general(qh, c_prev.astype(BF16), NT, preferred_element_type=F32)
            num = w_inter * cq + jnp.dot(smat.astype(BF16), vh, preferred_element_type=F32)
            n_prev = n_ref[b, h:h + 1, :]
            nq = jnp.sum(qh.astype(F32) * n_prev, axis=-1, keepdims=True)
            den = w_inter * nq + jnp.sum(smat, axis=-1, keepdims=True)
            hh = num / jnp.maximum(jnp.abs(den), jnp.exp(-m_t))
            hn = hh * lax.rsqrt(jnp.mean(hh * hh, axis=-1, keepdims=True) + EPS) * hg_ref[:, sl]
            hm_ref[b, :, sl] = (hn * _sigmoid(om[:, sl].astype(F32))).astype(hm_ref.dtype)
            b_last = b_c[chunk - 1:chunk, :]
            m_new = m_t[chunk - 1:chunk, :]
            decay = jnp.exp(m_prev + b_last - m_new)
            w_s = jnp.exp(li_c + b_last - b_c - m_new)
            vw = (vh.astype(F32) * w_s).astype(BF16)
            c_ref[b, h] = decay * c_prev + lax.dot_general(vw, kh, TN, preferred_element_type=F32)
            n_ref[b, h:h + 1, :] = decay * n_prev + jnp.sum(kh.astype(F32) * w_s, axis=0, keepdims=True)
            m_ref[pl.ds(b, 1), h:h + 1] = m_new
        return carry

    lax.fori_loop(0, bb, per_batch, 0)


def _mlstm(qm, km, vm, om, gcol, grow3, head_g, c0, n0, m0, *, batch, seq, chunk, out_dtype):
    bb = MLSTM_BATCH
    n_chunks = seq // chunk
    as3 = lambda a: a.reshape(batch, seq, a.shape[-1])
    tok = lambda w: pl.BlockSpec((bb, chunk, w), lambda g, c: (g, c, 0))
    state = lambda shape: pl.BlockSpec((bb,) + shape, lambda g, c: (g,) + (0,) * len(shape))
    in_specs = [tok(W_HEADS), tok(W_HEADS), tok(W_HEADS), tok(W_HEADS), tok(LANES),
                pl.BlockSpec((bb, N_GATE_ROWS, chunk), lambda g, c: (g, 0, c)),
                _const_spec(head_g.shape),
                state((H_M, DH_M, DH_M)), state((H_M, DH_M)), state((H_M,))]
    out_specs = [tok(W_HEADS), state((H_M, DH_M, DH_M)), state((H_M, DH_M)), state((H_M,))]
    out_shape = [jax.ShapeDtypeStruct((batch, seq, W_HEADS), out_dtype),
                 jax.ShapeDtypeStruct(c0.shape, F32), jax.ShapeDtypeStruct(n0.shape, F32),
                 jax.ShapeDtypeStruct(m0.shape, F32)]
    hm, c_new, n_new, m_new = pl.pallas_call(
        functools.partial(_mlstm_kernel, bb=bb, chunk=chunk),
        grid=(batch // bb, n_chunks),
        in_specs=in_specs,
        out_specs=out_specs,
        out_shape=out_shape,
        compiler_params=pltpu.CompilerParams(dimension_semantics=("arbitrary", "arbitrary"),
                                             vmem_limit_bytes=48 * MIB),
        name="mlstm",
    )(as3(qm), as3(km), as3(vm), as3(om), as3(gcol), grow3, head_g, c0, n0, m0)
    return hm.reshape(batch * seq, W_HEADS), c_new, n_new, m_new


def _fox_cumsum_kernel(gc_ref, gr_ref, cc_ref, cr_ref):
    seq = gc_ref.shape[1]
    t_idx = lax.broadcasted_iota(jnp.int32, (LANES, LANES), 0)
    s_idx = lax.broadcasted_iota(jnp.int32, (LANES, LANES), 1)
    tril = (s_idx <= t_idx).astype(F32)
    triu = (t_idx <= s_idx).astype(F32)
    carry_c = jnp.zeros((1, LANES), F32)
    carry_r = jnp.zeros((N_GATE_ROWS, 1), F32)
    for j in range(seq // LANES):
        blk = slice(j * LANES, (j + 1) * LANES)
        cb = jnp.dot(tril, gc_ref[0, blk, :], precision=HI, preferred_element_type=F32) + carry_c
        cc_ref[0, blk, :] = cb
        carry_c = cb[LANES - 1:LANES, :]
        rb = jnp.dot(gr_ref[:, blk], triu, precision=HI, preferred_element_type=F32) + carry_r
        cr_ref[:, blk] = rb
        carry_r = rb[:, LANES - 1:LANES]


def _fox_cumsum(gcol, grow, *, batch, seq):
    return pl.pallas_call(
        _fox_cumsum_kernel,
        grid=(batch,),
        in_specs=[pl.BlockSpec((1, seq, LANES), lambda b: (b, 0, 0)),
                  pl.BlockSpec((N_GATE_ROWS, seq), lambda b: (0, b))],
        out_specs=[pl.BlockSpec((1, seq, LANES), lambda b: (b, 0, 0)),
                   pl.BlockSpec((N_GATE_ROWS, seq), lambda b: (0, b))],
        out_shape=[jax.ShapeDtypeStruct((batch, seq, LANES), F32),
                   jax.ShapeDtypeStruct((N_GATE_ROWS, batch * seq), F32)],
        compiler_params=pltpu.CompilerParams(dimension_semantics=("arbitrary",)),
        name="fox_cumsum",
    )(gcol.reshape(batch, seq, LANES), grow)


def _fox_prompt_kernel(q_ref, kt_ref, vt_ref, cc_ref, cr_ref, o_ref, *, blk):
    i = pl.program_id(1)
    lane = lax.broadcasted_iota(jnp.int32, (blk, LANES), 1)
    low_half = lane < DH_F
    r_idx = lax.broadcasted_iota(jnp.int32, (blk, blk), 0)
    c_idx = lax.broadcasted_iota(jnp.int32, (blk, blk), 1)
    diag_mask = c_idx <= r_idx
    gate0 = N_GATE_ROWS - H_F

    for pair in range(H_F // 2):
        rows = slice(pair * LANES, (pair + 1) * LANES)
        q_pair = q_ref[:, rows]
        outs = []
        for e in range(2):
            h = 2 * pair + e
            q_e = jnp.where(low_half if e == 0 else jnp.logical_not(low_half), q_pair, jnp.zeros_like(q_pair))
            cq = cc_ref[:, gate0 + h:gate0 + h + 1]

            def step(j, carry, masked, q_e=q_e, cq=cq, h=h, rows=rows):
                m_i, l_i, acc = carry
                cols = pl.ds(pl.multiple_of(j * blk, blk), blk)
                kj = kt_ref[0, rows, cols]
                vj = vt_ref[0, rows, cols]
                s = jnp.dot(q_e, kj, preferred_element_type=F32) + (cq - cr_ref[gate0 + h:gate0 + h + 1, cols])
                if masked:
                    s = jnp.where(diag_mask, s, -jnp.inf)
                m_new = jnp.maximum(m_i, jnp.max(s, axis=-1, keepdims=True))
                p = jnp.exp(s - m_new)
                alpha = jnp.exp(m_i - m_new)
                l_new = alpha * l_i + jnp.sum(p, axis=-1, keepdims=True)
                pv = lax.dot_general(p.astype(BF16), vj, NT, preferred_element_type=F32)
                return m_new, l_new, alpha * acc + pv

            init = (jnp.full((blk, 1), -jnp.inf, F32), jnp.zeros((blk, 1), F32), jnp.zeros((blk, LANES), F32))
            carry = lax.fori_loop(0, i, functools.partial(step, masked=False), init)
            _, l_i, acc = step(i, carry, True)
            outs.append(acc / l_i)
        o_ref[:, rows] = jnp.where(low_half, outs[0], outs[1]).astype(o_ref.dtype)


def _fox_prompt(qf, ktb, vtb, ccol, crow, *, batch, seq):
    blk = FOX_BLOCK
    nq = seq // blk
    return pl.pallas_call(
        functools.partial(_fox_prompt_kernel, blk=blk),
        grid=(batch, nq),
        in_specs=[pl.BlockSpec((blk, W_HEADS), lambda b, i: (b * nq + i, 0)),
                  pl.BlockSpec((1, W_HEADS, seq), lambda b, i: (b, 0, 0)),
                  pl.BlockSpec((1, W_HEADS, seq), lambda b, i: (b, 0, 0)),
                  pl.BlockSpec((blk, LANES), lambda b, i: (b * nq + i, 0)),
                  pl.BlockSpec((N_GATE_ROWS, seq), lambda b, i: (0, b))],
        out_specs=pl.BlockSpec((blk, W_HEADS), lambda b, i: (b * nq + i, 0)),
        out_shape=jax.ShapeDtypeStruct((batch * seq, W_HEADS), BF16),
        compiler_params=pltpu.CompilerParams(dimension_semantics=("arbitrary", "arbitrary"),
                                             vmem_limit_bytes=48 * MIB),
        name="fox_prompt",
    )(qf, ktb, vtb, ccol.reshape(batch * seq, LANES), crow)


def _page_bias_kernel(pt_ref, lf_hbm, o_ref, buf, sem):
    b = pl.program_id(0)
    nb = pl.num_programs(0)
    n_pages = buf.shape[1]
    slot = b % 2

    def page_copy(bi, p, sl):
        return pltpu.make_async_copy(lf_hbm.at[pt_ref[bi, p]], buf.at[sl, p], sem.at[sl])

    def start_all(bi, sl):
        def body(p, c):
            page_copy(bi, p, sl).start()
            return c
        lax.fori_loop(0, n_pages, body, 0)

    @pl.when(b == 0)
    def _():
        start_all(0, 0)

    @pl.when(b + 1 < nb)
    def _():
        start_all(b + 1, 1 - slot)

    def wait_body(p, c):
        page_copy(b, p, slot).wait()
        return c
    lax.fori_loop(0, n_pages, wait_body, 0)

    x = buf[slot].reshape(n_pages * H_F, LANES)
    t_idx = lax.broadcasted_iota(jnp.int32, (LANES, LANES), 0)
    s_idx = lax.broadcasted_iota(jnp.int32, (LANES, LANES), 1)
    later = (t_idx > s_idx).astype(F32)
    within = jnp.dot(x, later, precision=HI, preferred_element_type=F32)
    total = jnp.sum(x, axis=-1, keepdims=True)
    run = jnp.zeros((H_F, LANES), F32)
    for p in range(n_pages - 1, -1, -1):
        rows = slice(p * H_F, (p + 1) * H_F)
        o_ref[0, p] = -(within[rows] + run)
        run = run + total[rows]


def _page_bias(page_table, lf_pool):
    batch, n_pages = page_table.shape
    return pl.pallas_call(
        _page_bias_kernel,
        grid_spec=pltpu.PrefetchScalarGridSpec(
            num_scalar_prefetch=1,
            grid=(batch,),
            in_specs=[pl.BlockSpec(memory_space=pl.ANY)],
            out_specs=pl.BlockSpec((1, n_pages, H_F, LANES), lambda b, pt: (b, 0, 0, 0)),
            scratch_shapes=[pltpu.VMEM((2, n_pages, H_F, LANES), F32), pltpu.SemaphoreType.DMA((2,))],
        ),
        out_shape=jax.ShapeDtypeStruct((batch, n_pages, H_F, LANES), F32),
        compiler_params=pltpu.CompilerParams(dimension_semantics=("arbitrary",)),
        name="page_bias",
    )(page_table, lf_pool)


def _fox_sample_kernel(pt_ref, q_ref, gr_ref, kn_ref, vn_ref, cp_ref, k_hbm, v_hbm, o_ref,
                       kbuf, vbuf, sem, acc_ref, *, n_groups):
    b = pl.program_id(0)
    nb = pl.num_programs(0)
    group = PAGES_PER_GROUP
    t_new = q_ref.shape[0]
    n_rows = t_new * H_F

    def page_copies(bi, g, sl):
        cps = []
        for j in range(group):
            page = pt_ref[bi, g * group + j]
            cps.append(pltpu.make_async_copy(k_hbm.at[page], kbuf.at[sl, j], sem.at[sl, 0]))
            cps.append(pltpu.make_async_copy(v_hbm.at[page], vbuf.at[sl, j], sem.at[sl, 1]))
        return cps

    def start_group(bi, g, sl):
        for cp in page_copies(bi, g, sl):
            cp.start()

    @pl.when(b == 0)
    def _():
        start_group(0, 0, 0)

    sub = lax.broadcasted_iota(jnp.int32, (H_F, W_HEADS), 0)
    lane = lax.broadcasted_iota(jnp.int32, (H_F, W_HEADS), 1)
    own_head = (lane // DH_F) == sub
    q = q_ref[...].astype(F32)
    qbd = jnp.concatenate(
        [jnp.where(own_head, jnp.broadcast_to(q[t:t + 1, :], (H_F, W_HEADS)), 0.0) for t in range(t_new)],
        axis=0).astype(BF16)

    gate0 = N_GATE_ROWS - H_F
    lf_new = gr_ref[0, gate0:N_GATE_ROWS, :]
    a_idx = lax.broadcasted_iota(jnp.int32, (t_new, t_new), 0)
    b_idx = lax.broadcasted_iota(jnp.int32, (t_new, t_new), 1)
    c_new = jnp.dot(lf_new, (a_idx <= b_idx).astype(F32), precision=HI, preferred_element_type=F32)
    cq = jnp.concatenate([c_new[:, t:t + 1] for t in range(t_new)], axis=0)

    acc_ref[...] = jnp.zeros_like(acc_ref)

    def body(g, carry):
        m_i, l_i = carry
        slot = g % 2

        @pl.when(g + 1 < n_groups)
        def _():
            start_group(b, g + 1, 1 - slot)

        @pl.when(jnp.logical_and(g + 1 == n_groups, b + 1 < nb))
        def _():
            start_group(b + 1, 0, 1 - slot)

        for cp in page_copies(b, g, slot):
            cp.wait()

        kcat = jnp.concatenate([kbuf[slot, j].astype(BF16) for j in range(group)], axis=1)
        s = jnp.dot(qbd, kcat, preferred_element_type=F32)
        first = pl.multiple_of(g * group, group)
        cpg = cp_ref[0, pl.ds(first, group)]
        bias = jnp.concatenate(
            [jnp.broadcast_to(cpg[j][None], (t_new, H_F, LANES)).reshape(n_rows, LANES) for j in range(group)],
            axis=1)
        s = s + (cq - bias)
        m_new = jnp.maximum(m_i, jnp.max(s, axis=-1, keepdims=True))
        p = jnp.exp(s - m_new)
        alpha = jnp.exp(m_i - m_new)
        l_new = alpha * l_i + jnp.sum(p, axis=-1, keepdims=True)
        vcat = jnp.concatenate([vbuf[slot, j].astype(BF16) for j in range(group)], axis=1)
        pv = lax.dot_general(p.astype(BF16), vcat, NT, preferred_element_type=F32)
        acc_ref[...] = alpha * acc_ref[...] + pv
        return m_new, l_new

    init = (jnp.full((n_rows, 1), -jnp.inf, F32), jnp.zeros((n_rows, 1), F32))
    m_i, l_i = lax.fori_loop(0, n_groups, body, init)

    kn = kn_ref[...].astype(BF16)
    vn = vn_ref[...].astype(BF16)
    s = lax.dot_general(qbd, kn, NT, preferred_element_type=F32)
    ck = jnp.broadcast_to(c_new[None], (t_new, H_F, t_new)).reshape(n_rows, t_new)
    s = s + (cq - ck)
    r_idx = lax.broadcasted_iota(jnp.int32, (n_rows, t_new), 0)
    k_idx = lax.broadcasted_iota(jnp.int32, (n_rows, t_new), 1)
    s = jnp.where(k_idx <= r_idx // H_F, s, -jnp.inf)
    m_new = jnp.maximum(m_i, jnp.max(s, axis=-1, keepdims=True))
    p = jnp.exp(s - m_new)
    alpha = jnp.exp(m_i - m_new)
    l_fin = alpha * l_i + jnp.sum(p, axis=-1, keepdims=True)
    acc = alpha * acc_ref[...] + jnp.dot(p.astype(BF16), vn, preferred_element_type=F32)
    out = acc / l_fin
    o_ref[...] = jnp.concatenate(
        [jnp.sum(jnp.where(own_head, out[t * H_F:(t + 1) * H_F, :], 0.0), axis=0, keepdims=True)
         for t in range(t_new)], axis=0).astype(o_ref.dtype)


def _fox_sample(page_table, qf, grow3, k_new, v_new, page_bias, k_pool, v_pool):
    batch, n_pages = page_table.shape
    t_new = qf.shape[0] // batch
    n_groups = n_pages // PAGES_PER_GROUP
    page_rows, page_len = k_pool.shape[1], k_pool.shape[2]
    tok = lambda w: pl.BlockSpec((t_new, w), lambda b, pt: (b, 0))
    return pl.pallas_call(
        functools.partial(_fox_sample_kernel, n_groups=n_groups),
        grid_spec=pltpu.PrefetchScalarGridSpec(
            num_scalar_prefetch=1,
            grid=(batch,),
            in_specs=[tok(W_HEADS),
                      pl.BlockSpec((1, N_GATE_ROWS, t_new), lambda b, pt: (b, 0, 0)),
                      tok(W_HEADS), tok(W_HEADS),
                      pl.BlockSpec((1, n_pages, H_F, LANES), lambda b, pt: (b, 0, 0, 0)),
                      pl.BlockSpec(memory_space=pl.ANY), pl.BlockSpec(memory_space=pl.ANY)],
            out_specs=tok(W_HEADS),
            scratch_shapes=[pltpu.VMEM((2, PAGES_PER_GROUP, page_rows, page_len), F32),
                            pltpu.VMEM((2, PAGES_PER_GROUP, page_rows, page_len), F32),
                            pltpu.SemaphoreType.DMA((2, 2)),
                            pltpu.VMEM((t_new * H_F, W_HEADS), F32)],
        ),
        out_shape=jax.ShapeDtypeStruct((batch * t_new, W_HEADS), F32),
        compiler_params=pltpu.CompilerParams(dimension_semantics=("arbitrary",), vmem_limit_bytes=48 * MIB),
        name="fox_sample",
    )(page_table, qf, grow3, k_new, v_new, page_bias, k_pool, v_pool)


def _ffn_kernel(x_ref, hm_ref, hf_ref, wo_ref, g2_ref, wg_ref, wu_ref, wd_ref, g3_ref, y_ref, *, final_norm):
    x1 = (x_ref[...]
          + jnp.dot(hm_ref[...].astype(BF16), wo_ref[0:W_HEADS, :], preferred_element_type=F32)
          + jnp.dot(hf_ref[...].astype(BF16), wo_ref[W_HEADS:2 * W_HEADS, :], preferred_element_type=F32))
    h = _rms(x1, g2_ref[...]).astype(BF16)
    gate = jnp.dot(h, wg_ref[...], preferred_element_type=F32)
    up = jnp.dot(h, wu_ref[...], preferred_element_type=F32)
    act = (gate * _sigmoid(gate) * up).astype(BF16)
    x2 = x1 + jnp.dot(act, wd_ref[...], preferred_element_type=F32)
    y_ref[...] = _rms(x2, g3_ref[...]) if final_norm else x2


def _merge_ffn(x2d, hm, hf, wo, g2, wg, wu, wd, g3, *, final_norm):
    rows_total, d_model = x2d.shape
    tm = min(FFN_ROWS, rows_total)
    row_spec = lambda w: pl.BlockSpec((tm, w), lambda i: (i, 0))
    return pl.pallas_call(
        functools.partial(_ffn_kernel, final_norm=final_norm),
        grid=(rows_total // tm,),
        in_specs=[row_spec(d_model), row_spec(W_HEADS), row_spec(W_HEADS), _const_spec(wo.shape),
                  _const_spec(g2.shape), _const_spec(wg.shape), _const_spec(wu.shape), _const_spec(wd.shape),
                  _const_spec(g3.shape)],
        out_specs=row_spec(d_model),
        out_shape=jax.ShapeDtypeStruct((rows_total, d_model), F32),
        compiler_params=pltpu.CompilerParams(dimension_semantics=("arbitrary",), vmem_limit_bytes=56 * MIB),
        name="merge_ffn",
    )(x2d, hm, hf, wo, g2, wg, wu, wd, g3)


def kernel(x_prompt, x_sample, cache_fox_k, cache_fox_v, cache_fox_logf, page_table, state_mlstm_C,
           state_mlstm_n, state_mlstm_m, norm_mix_g, w_in, b_m_igate, b_m_fgate, b_f_fgate, mlstm_head_g,
           w_out, norm_ffn_g, w_gate, w_up, w_down, norm_final_g):
    depth = w_in.shape[0]
    batch, seq, d_model = x_prompt.shape
    dec_batch, dec_seq, _ = x_sample.shape
    n_pool, page_size = cache_fox_k.shape[1], cache_fox_k.shape[2]
    xp = x_prompt.reshape(batch * seq, d_model)
    xs = x_sample.reshape(dec_batch * dec_seq, d_model)
    g_final = norm_final_g.reshape(1, d_model)
    pk, pv, plf, pc, pn, pm = [], [], [], [], [], []
    sk, sv, slf, sc, sn, sm = [], [], [], [], [], []
    gate0 = N_GATE_ROWS - H_F
    o_gm = 4 * W_HEADS
    o_qf = o_gm + 2 * H_M
    o_gf = o_qf + 3 * W_HEADS
    for l in range(depth):
        wt = jnp.swapaxes(w_in[l], 0, 1)
        wt_main = jnp.concatenate([wt[0:o_gm], wt[o_qf:o_qf + W_HEADS]], axis=0).astype(BF16)
        wt_kv = wt[o_qf + W_HEADS:o_gf].astype(BF16)
        wt_g = jnp.concatenate([wt[o_gm:o_qf], wt[o_gf:o_gf + H_F],
                                jnp.zeros((LANES - N_GATE_ROWS, d_model), F32)], axis=0).astype(BF16)
        bias = jnp.concatenate([b_m_igate[l], b_m_fgate[l], b_f_fgate[l],
                                jnp.zeros((LANES - N_GATE_ROWS,), F32)]).astype(F32)
        bcol = bias.reshape(1, LANES)
        brow = bias[:N_GATE_ROWS].reshape(N_GATE_ROWS, 1)
        g_mix = norm_mix_g[l].reshape(1, d_model)
        g_ffn = norm_ffn_g[l].reshape(1, d_model)
        head_g = mlstm_head_g[l].reshape(1, W_HEADS)
        wo = w_out[l].astype(BF16)
        wg = w_gate[l].astype(BF16)
        wu = w_up[l].astype(BF16)
        wd = w_down[l].astype(BF16)

        qm, km, vm, om, qf, kt, vt, ktb, vtb, gcol, grow = _project(
            xp, g_mix, wt_main, wt_kv, wt_g, bcol, brow, batch=batch, seq=seq, kv_transposed=True, act_dtype=BF16)
        chunk = min(LANES, seq)
        grow3 = grow.reshape(N_GATE_ROWS, batch, seq).transpose(1, 0, 2)
        hm, c_p, n_p, m_p = _mlstm(
            qm, km, vm, om, gcol, grow3, head_g,
            jnp.zeros((batch, H_M, DH_M, DH_M), F32), jnp.zeros((batch, H_M, DH_M), F32),
            jnp.zeros((batch, H_M), F32), batch=batch, seq=seq, chunk=chunk, out_dtype=BF16)
        ccol, crow = _fox_cumsum(gcol, grow, batch=batch, seq=seq)
        hf = _fox_prompt(qf, ktb, vtb, ccol, crow, batch=batch, seq=seq)
        xp = _merge_ffn(xp, hm, hf, wo, g_ffn, wg, wu, wd, g_final, final_norm=(l == depth - 1))
        pk.append(kt.reshape(batch, H_F, DH_F, seq).transpose(0, 3, 1, 2))
        pv.append(vt.reshape(batch, H_F, DH_F, seq).transpose(0, 3, 1, 2))
        plf.append(grow3[:, gate0:, :].transpose(0, 2, 1))
        pc.append(c_p); pn.append(n_p); pm.append(m_p)

        qm, km, vm, om, qf, k_new, v_new, gcol, grow = _project(
            xs, g_mix, wt_main, wt_kv, wt_g, bcol, brow, batch=dec_batch, seq=dec_seq, kv_transposed=False,
            act_dtype=F32)
        grow3 = grow.reshape(N_GATE_ROWS, dec_batch, dec_seq).transpose(1, 0, 2)
        hm, c_s, n_s, m_s = _mlstm(
            qm, km, vm, om, gcol, grow3, head_g,
            state_mlstm_C[l].astype(F32), state_mlstm_n[l].astype(F32), state_mlstm_m[l].astype(F32),
            batch=dec_batch, seq=dec_seq, chunk=dec_seq, out_dtype=F32)
        k_pool = cache_fox_k[l].transpose(0, 2, 3, 1).reshape(n_pool, W_HEADS, page_size)
        v_pool = cache_fox_v[l].transpose(0, 2, 3, 1).reshape(n_pool, W_HEADS, page_size)
        lf_pool = cache_fox_logf[l].transpose(0, 2, 1)
        page_bias = _page_bias(page_table, lf_pool)
        hf = _fox_sample(page_table, qf, grow3, k_new, v_new, page_bias, k_pool, v_pool)
        xs = _merge_ffn(xs, hm, hf, wo, g_ffn, wg, wu, wd, g_final, final_norm=(l == depth - 1))
        sk.append(k_new.reshape(dec_batch, dec_seq, H_F, DH_F))
        sv.append(v_new.reshape(dec_batch, dec_seq, H_F, DH_F))
        slf.append(grow3[:, gate0:, :].transpose(0, 2, 1))
        sc.append(c_s); sn.append(n_s); sm.append(m_s)

    st = lambda a, ref: jnp.stack(a, axis=0).astype(ref.dtype)
    return (xp.reshape(batch, seq, d_model), xs.reshape(dec_batch, dec_seq, d_model),
            st(pk, cache_fox_k), st(pv, cache_fox_v), st(plf, cache_fox_logf),
            st(pc, state_mlstm_C), st(pn, state_mlstm_n), st(pm, state_mlstm_m),
            st(sk, cache_fox_k), st(sv, cache_fox_v), st(slf, cache_fox_logf),
            st(sc, state_mlstm_C), st(sn, state_mlstm_n), st(sm, state_mlstm_m))
```

```python
import functools

import jax
import jax.numpy as jnp
from jax import lax
from jax.experimental import pallas as pl
from jax.experimental.pallas import tpu as pltpu

F32 = jnp.float32
BF16 = jnp.bfloat16
HI = lax.Precision.HIGHEST
NT = (((1,), (1,)), ((), ()))
TN = (((0,), (0,)), ((), ()))

EPS = 1e-6
LOG2E = 1.4426950408889634
H_M = 4
DH_M = 128
H_F = 8
DH_F = 64
W_HEADS = 512
N_GATE_ROWS = 16
LANES = 128
MIB = 1024 * 1024

PROJ_ROWS = 512
FFN_ROWS = 512
MLSTM_BATCH = 8
FOX_BLOCK = 256
PAGES_PER_GROUP = 8


def _rms(x, g):
    return x * lax.rsqrt(jnp.mean(x * x, axis=-1, keepdims=True) + EPS) * g


def _log_sigmoid(x):
    return jnp.minimum(x, 0.0) - jnp.log1p(jnp.exp(-jnp.abs(x)))


def _sigmoid(x):
    return 1.0 / (1.0 + jnp.exp(-x))


def _const_spec(shape):
    return pl.BlockSpec(shape, lambda *_: (0,) * len(shape), pipeline_mode=pl.Buffered(1))


def _proj_kernel(x_ref, g_ref, wt_ref, wkv_ref, wg_ref, bcol_ref, brow_ref, *outs, kv_transposed):
    if kv_transposed:
        qm_ref, km_ref, vm_ref, om_ref, qf_ref, kf_ref, vf_ref, kfb_ref, vfb_ref, gcol_ref, grow_ref = outs
    else:
        qm_ref, km_ref, vm_ref, om_ref, qf_ref, kf_ref, vf_ref, gcol_ref, grow_ref = outs
    h = _rms(x_ref[...], g_ref[...]).astype(BF16)
    rows = h.shape[0]

    def mm(i):
        w = wt_ref[i * W_HEADS:(i + 1) * W_HEADS, :]
        return lax.dot_general(h, w, NT, preferred_element_type=F32)

    qm_ref[...] = mm(0).astype(qm_ref.dtype)
    km_ref[...] = (mm(1) * (DH_M ** -0.5)).astype(km_ref.dtype)
    vm_ref[...] = mm(2).astype(vm_ref.dtype)
    om_ref[...] = mm(3).astype(om_ref.dtype)
    qf_ref[...] = (mm(4) * (DH_F ** -0.5 * LOG2E)).astype(qf_ref.dtype)
    if kv_transposed:
        kt = lax.dot_general(wkv_ref[0:W_HEADS, :], h, NT, preferred_element_type=F32)
        kf_ref[0] = kt
        kfb_ref[0] = kt.astype(BF16)
        vt = lax.dot_general(wkv_ref[W_HEADS:2 * W_HEADS, :], h, NT, preferred_element_type=F32)
        vf_ref[0] = vt
        vfb_ref[0] = vt.astype(BF16)
    else:
        kf_ref[...] = lax.dot_general(h, wkv_ref[0:W_HEADS, :], NT, preferred_element_type=F32)
        vf_ref[...] = lax.dot_general(h, wkv_ref[W_HEADS:2 * W_HEADS, :], NT, preferred_element_type=F32)
    pre_c = lax.dot_general(h, wg_ref[...], NT, preferred_element_type=F32) + bcol_ref[...]
    lane = lax.broadcasted_iota(jnp.int32, (rows, LANES), 1)
    gcol_ref[...] = jnp.where(lane < H_M, pre_c, _log_sigmoid(pre_c))
    pre_r = lax.dot_general(wg_ref[0:N_GATE_ROWS, :], h, NT, preferred_element_type=F32) + brow_ref[...]
    row = lax.broadcasted_iota(jnp.int32, (N_GATE_ROWS, rows), 0)
    grow_ref[...] = jnp.where(row < H_M, pre_r, _log_sigmoid(pre_r))


def _project(x2d, g, wt_main, wt_kv, wt_g, bcol, brow, *, batch, seq, kv_transposed, act_dtype):
    rows_total, d_model = x2d.shape
    tm = min(PROJ_ROWS, rows_total)
    steps = rows_total // tm
    per_seq = max(seq // tm, 1)
    row_spec = lambda w: pl.BlockSpec((tm, w), lambda i: (i, 0))
    in_specs = [row_spec(d_model), _const_spec(g.shape), _const_spec(wt_main.shape), _const_spec(wt_kv.shape),
                _const_spec(wt_g.shape), _const_spec(bcol.shape), _const_spec(brow.shape)]
    act = jax.ShapeDtypeStruct((rows_total, W_HEADS), act_dtype)
    act32 = jax.ShapeDtypeStruct((rows_total, W_HEADS), F32)
    out_shape = [act, act, act, act32, act]
    out_specs = [row_spec(W_HEADS)] * 5
    if kv_transposed:
        kv_spec = pl.BlockSpec((1, W_HEADS, tm), lambda i: (i // per_seq, 0, i % per_seq))
        out_shape += [jax.ShapeDtypeStruct((batch, W_HEADS, seq), F32)] * 2
        out_shape += [jax.ShapeDtypeStruct((batch, W_HEADS, seq), BF16)] * 2
        out_specs += [kv_spec] * 4
    else:
        out_shape += [act32, act32]
        out_specs += [row_spec(W_HEADS)] * 2
    out_shape += [jax.ShapeDtypeStruct((rows_total, LANES), F32),
                  jax.ShapeDtypeStruct((N_GATE_ROWS, rows_total), F32)]
    out_specs += [row_spec(LANES), pl.BlockSpec((N_GATE_ROWS, tm), lambda i: (0, i))]
    return pl.pallas_call(
        functools.partial(_proj_kernel, kv_transposed=kv_transposed),
        grid=(steps,),
        in_specs=in_specs,
        out_specs=out_specs,
        out_shape=out_shape,
        compiler_params=pltpu.CompilerParams(dimension_semantics=("arbitrary",), vmem_limit_bytes=48 * MIB),
        name="proj",
    )(x2d, g, wt_main, wt_kv, wt_g, bcol, brow)


def _mlstm_kernel(q_ref, k_ref, v_ref, om_ref, gc_ref, gr_ref, hg_ref, c0_ref, n0_ref, m0_ref,
                  hm_ref, c_ref, n_ref, m_ref, *, bb, chunk):
    @pl.when(pl.program_id(1) == 0)
    def _():
        c_ref[...] = c0_ref[...]
        n_ref[...] = n0_ref[...]
        m_ref[...] = m0_ref[...]

    t_idx = lax.broadcasted_iota(jnp.int32, (chunk, chunk), 0)
    s_idx = lax.broadcasted_iota(jnp.int32, (chunk, chunk), 1)
    causal = s_idx <= t_idx
    tril = causal.astype(F32)
    triu = (t_idx <= s_idx).astype(F32)

    def per_batch(b, carry):
        gc = gc_ref[b]
        gr = gr_ref[b]
        bc = jnp.dot(tril, gc, precision=HI, preferred_element_type=F32)
        br = jnp.dot(gr, triu, precision=HI, preferred_element_type=F32)
        q = q_ref[b]
        k = k_ref[b]
        v = v_ref[b]
        om = om_ref[b]
        for h in range(H_M):
            sl = slice(h * DH_M, (h + 1) * DH_M)
            li_c = gc[:, h:h + 1]
            li_r = gr[h:h + 1, :]
            b_c = bc[:, H_M + h:H_M + h + 1]
            b_r = br[H_M + h:H_M + h + 1, :]
            m_prev = m_ref[pl.ds(b, 1), h:h + 1]
            dmat = jnp.where(causal, b_c - b_r + li_r, -jnp.inf)
            inter = m_prev + b_c
            m_t = jnp.maximum(inter, jnp.max(dmat, axis=-1, keepdims=True))
            qh = q[:, sl].astype(BF16)
            kh = k[:, sl].astype(BF16)
            vh = v[:, sl].astype(BF16)
            smat = lax.dot_general(qh, kh, NT, preferred_element_type=F32) * jnp.exp(dmat - m_t)
            w_inter = jnp.exp(inter - m_t)
            c_prev = c_ref[b, h]
            cq = lax.dot_general(qh, c_prev.astype(BF16), NT, preferred_element_type=F32)
            num = w_inter * cq + jnp.dot(smat.astype(BF16), vh, preferred_element_type=F32)
            n_prev = n_ref[b, h:h + 1, :]
            nq = jnp.sum(qh.astype(F32) * n_prev, axis=-1, keepdims=True)
            den = w_inter * nq + jnp.sum(smat, axis=-1, keepdims=True)
            hh = num / jnp.maximum(jnp.abs(den), jnp.exp(-m_t))
            hn = hh * lax.rsqrt(jnp.mean(hh * hh, axis=-1, keepdims=True) + EPS) * hg_ref[:, sl]
            hm_ref[b, :, sl] = (hn * _sigmoid(om[:, sl].astype(F32))).astype(hm_ref.dtype)
            b_last = b_c[chunk - 1:chunk, :]
            m_new = m_t[chunk - 1:chunk, :]
            decay = jnp.exp(m_prev + b_last - m_new)
            w_s = jnp.exp(li_c + b_last - b_c - m_new)
            vw = (vh.astype(F32) * w_s).astype(BF16)
            c_ref[b, h] = decay * c_prev + lax.dot_general(vw, kh, TN, preferred_element_type=F32)
            n_ref[b, h:h + 1, :] = decay * n_prev + jnp.sum(kh.astype(F32) * w_s, axis=0, keepdims=True)
            m_ref[pl.ds(b, 1), h:h + 1] = m_new
        return carry

    lax.fori_loop(0, bb, per_batch, 0)


def _mlstm(qm, km, vm, om, gcol, grow3, head_g, c0, n0, m0, *, batch, seq, chunk, out_dtype):
    bb = MLSTM_BATCH
    n_chunks = seq // chunk
    as3 = lambda a: a.reshape(batch, seq, a.shape[-1])
    tok = lambda w: pl.BlockSpec((bb, chunk, w), lambda g, c: (g, c, 0))
    state = lambda shape: pl.BlockSpec((bb,) + shape, lambda g, c: (g,) + (0,) * len(shape))
    in_specs = [tok(W_HEADS), tok(W_HEADS), tok(W_HEADS), tok(W_HEADS), tok(LANES),
                pl.BlockSpec((bb, N_GATE_ROWS, chunk), lambda g, c: (g, 0, c)),
                _const_spec(head_g.shape),
                state((H_M, DH_M, DH_M)), state((H_M, DH_M)), state((H_M,))]
    out_specs = [tok(W_HEADS), state((H_M, DH_M, DH_M)), state((H_M, DH_M)), state((H_M,))]
    out_shape = [jax.ShapeDtypeStruct((batch, seq, W_HEADS), out_dtype),
                 jax.ShapeDtypeStruct(c0.shape, F32), jax.ShapeDtypeStruct(n0.shape, F32),
                 jax.ShapeDtypeStruct(m0.shape, F32)]
    hm, c_new, n_new, m_new = pl.pallas_call(
        functools.partial(_mlstm_kernel, bb=bb, chunk=chunk),
        grid=(batch // bb, n_chunks),
        in_specs=in_specs,
        out_specs=out_specs,
        out_shape=out_shape,
        compiler_params=pltpu.CompilerParams(dimension_semantics=("arbitrary", "arbitrary"),
                                             vmem_limit_bytes=48 * MIB),
        name="mlstm",
    )(as3(qm), as3(km), as3(vm), as3(om), as3(gcol), grow3, head_g, c0, n0, m0)
    return hm.reshape(batch * seq, W_HEADS), c_new, n_new, m_new


def _fox_cumsum_kernel(gc_ref, gr_ref, cc_ref, cr_ref):
    seq = gc_ref.shape[1]
    t_idx = lax.broadcasted_iota(jnp.int32, (LANES, LANES), 0)
    s_idx = lax.broadcasted_iota(jnp.int32, (LANES, LANES), 1)
    tril = (s_idx <= t_idx).astype(F32)
    triu = (t_idx <= s_idx).astype(F32)
    carry_c = jnp.zeros((1, LANES), F32)
    carry_r = jnp.zeros((N_GATE_ROWS, 1), F32)
    for j in range(seq // LANES):
        blk = slice(j * LANES, (j + 1) * LANES)
        cb = jnp.dot(tril, gc_ref[0, blk, :], precision=HI, preferred_element_type=F32) + carry_c
        cc_ref[0, blk, :] = cb * LOG2E
        carry_c = cb[LANES - 1:LANES, :]
        rb = jnp.dot(gr_ref[:, blk], triu, precision=HI, preferred_element_type=F32) + carry_r
        cr_ref[:, blk] = rb * LOG2E
        carry_r = rb[:, LANES - 1:LANES]


def _fox_cumsum(gcol, grow, *, batch, seq):
    return pl.pallas_call(
        _fox_cumsum_kernel,
        grid=(batch,),
        in_specs=[pl.BlockSpec((1, seq, LANES), lambda b: (b, 0, 0)),
                  pl.BlockSpec((N_GATE_ROWS, seq), lambda b: (0, b))],
        out_specs=[pl.BlockSpec((1, seq, LANES), lambda b: (b, 0, 0)),
                   pl.BlockSpec((N_GATE_ROWS, seq), lambda b: (0, b))],
        out_shape=[jax.ShapeDtypeStruct((batch, seq, LANES), F32),
                   jax.ShapeDtypeStruct((N_GATE_ROWS, batch * seq), F32)],
        compiler_params=pltpu.CompilerParams(dimension_semantics=("arbitrary",)),
        name="fox_cumsum",
    )(gcol.reshape(batch, seq, LANES), grow)


def _fox_prompt_kernel(q_ref, kt_ref, vt_ref, cc_ref, cr_ref, o_ref, qs_ref, cq_ref, m_ref, l_ref, acc_ref, *, blk):
    i = pl.program_id(1)
    n_pairs = H_F // 2
    lane_tiles = blk // LANES
    lane = lax.broadcasted_iota(jnp.int32, (blk, LANES), 1)
    low_half = lane < DH_F
    r_idx = lax.broadcasted_iota(jnp.int32, (blk, blk), 0)
    c_idx = lax.broadcasted_iota(jnp.int32, (blk, blk), 1)
    diag_mask = jnp.concatenate([c_idx <= r_idx] * 2, axis=0)
    gate0 = N_GATE_ROWS - H_F

    for pair in range(n_pairs):
        rows = slice(pair * LANES, (pair + 1) * LANES)
        q_pair = q_ref[:, rows]
        zero = jnp.zeros_like(q_pair)
        qs_ref[pair, 0:blk, :] = jnp.where(low_half, q_pair, zero)
        qs_ref[pair, blk:2 * blk, :] = jnp.where(low_half, zero, q_pair)
        for e in range(2):
            col = gate0 + 2 * pair + e
            cq_ref[pair, e * blk:(e + 1) * blk, :] = jnp.broadcast_to(cc_ref[:, col:col + 1], (blk, LANES))
    m_ref[...] = jnp.full(m_ref.shape, -jnp.inf, F32)
    l_ref[...] = jnp.zeros(l_ref.shape, F32)
    acc_ref[...] = jnp.zeros(acc_ref.shape, F32)

    def step(j, masked):
        cols = pl.ds(pl.multiple_of(j * blk, blk), blk)
        for pair in range(n_pairs):
            rows = slice(pair * LANES, (pair + 1) * LANES)
            kj = kt_ref[0, rows, cols]
            vj = vt_ref[0, rows, cols]
            s = jnp.dot(qs_ref[pair], kj, preferred_element_type=F32)
            ck = jnp.concatenate(
                [jnp.broadcast_to(cr_ref[gate0 + 2 * pair + e:gate0 + 2 * pair + e + 1, cols], (blk, blk))
                 for e in range(2)], axis=0)
            s = s + (jnp.concatenate([cq_ref[pair]] * lane_tiles, axis=1) - ck)
            if masked:
                s = jnp.where(diag_mask, s, -jnp.inf)
            m_old = m_ref[pair]
            m_new = jnp.maximum(m_old, jnp.max(s, axis=-1, keepdims=True))
            p = jnp.exp2(s - jnp.concatenate([m_new] * lane_tiles, axis=1))
            alpha = jnp.exp2(m_old - m_new)
            p_lanes = p[:, 0:LANES]
            for t in range(1, lane_tiles):
                p_lanes = p_lanes + p[:, t * LANES:(t + 1) * LANES]
            l_ref[pair] = alpha * l_ref[pair] + p_lanes
            pv = lax.dot_general(p.astype(BF16), vj, NT, preferred_element_type=F32)
            acc_ref[pair] = alpha * acc_ref[pair] + pv
            m_ref[pair] = m_new

    def loop_body(j, carry):
        step(j, False)
        return carry

    lax.fori_loop(0, i, loop_body, 0)
    step(i, True)
    for pair in range(n_pairs):
        rows = slice(pair * LANES, (pair + 1) * LANES)
        out = acc_ref[pair] / jnp.sum(l_ref[pair], axis=-1, keepdims=True)
        o_ref[:, rows] = jnp.where(low_half, out[0:blk], out[blk:2 * blk]).astype(o_ref.dtype)


def _fox_prompt(qf, ktb, vtb, ccol, crow, *, batch, seq):
    blk = FOX_BLOCK
    nq = seq // blk
    return pl.pallas_call(
        functools.partial(_fox_prompt_kernel, blk=blk),
        grid=(batch, nq),
        in_specs=[pl.BlockSpec((blk, W_HEADS), lambda b, i: (b * nq + i, 0)),
                  pl.BlockSpec((1, W_HEADS, seq), lambda b, i: (b, 0, 0)),
                  pl.BlockSpec((1, W_HEADS, seq), lambda b, i: (b, 0, 0)),
                  pl.BlockSpec((blk, LANES), lambda b, i: (b * nq + i, 0)),
                  pl.BlockSpec((N_GATE_ROWS, seq), lambda b, i: (0, b))],
        out_specs=pl.BlockSpec((blk, W_HEADS), lambda b, i: (b * nq + i, 0)),
        out_shape=jax.ShapeDtypeStruct((batch * seq, W_HEADS), BF16),
        scratch_shapes=[pltpu.VMEM((H_F // 2, 2 * blk, LANES), BF16)] + [pltpu.VMEM((H_F // 2, 2 * blk, LANES), F32)] * 4,
        compiler_params=pltpu.CompilerParams(dimension_semantics=("arbitrary", "arbitrary"),
                                             vmem_limit_bytes=48 * MIB),
        name="fox_prompt",
    )(qf, ktb, vtb, ccol.reshape(batch * seq, LANES), crow)


def _page_bias_kernel(pt_ref, lf_hbm, o_ref, buf, sem):
    b = pl.program_id(0)
    nb = pl.num_programs(0)
    n_pages = buf.shape[1]
    slot = b % 2

    def page_copy(bi, p, sl):
        return pltpu.make_async_copy(lf_hbm.at[pt_ref[bi, p]], buf.at[sl, p], sem.at[sl])

    def start_all(bi, sl):
        def body(p, c):
            page_copy(bi, p, sl).start()
            return c
        lax.fori_loop(0, n_pages, body, 0)

    @pl.when(b == 0)
    def _():
        start_all(0, 0)

    @pl.when(b + 1 < nb)
    def _():
        start_all(b + 1, 1 - slot)

    def wait_body(p, c):
        page_copy(b, p, slot).wait()
        return c
    lax.fori_loop(0, n_pages, wait_body, 0)

    x = buf[slot].reshape(n_pages * H_F, LANES)
    t_idx = lax.broadcasted_iota(jnp.int32, (LANES, LANES), 0)
    s_idx = lax.broadcasted_iota(jnp.int32, (LANES, LANES), 1)
    later = (t_idx > s_idx).astype(F32)
    within = jnp.dot(x, later, precision=HI, preferred_element_type=F32)
    total = jnp.sum(x, axis=-1, keepdims=True)
    run = jnp.zeros((H_F, LANES), F32)
    for p in range(n_pages - 1, -1, -1):
        rows = slice(p * H_F, (p + 1) * H_F)
        o_ref[0, p] = (within[rows] + run) * (-LOG2E)
        run = run + total[rows]


def _page_bias(page_table, lf_pool):
    batch, n_pages = page_table.shape
    return pl.pallas_call(
        _page_bias_kernel,
        grid_spec=pltpu.PrefetchScalarGridSpec(
            num_scalar_prefetch=1,
            grid=(batch,),
            in_specs=[pl.BlockSpec(memory_space=pl.ANY)],
            out_specs=pl.BlockSpec((1, n_pages, H_F, LANES), lambda b, pt: (b, 0, 0, 0)),
            scratch_shapes=[pltpu.VMEM((2, n_pages, H_F, LANES), F32), pltpu.SemaphoreType.DMA((2,))],
        ),
        out_shape=jax.ShapeDtypeStruct((batch, n_pages, H_F, LANES), F32),
        compiler_params=pltpu.CompilerParams(dimension_semantics=("arbitrary",)),
        name="page_bias",
    )(page_table, lf_pool)


def _fox_sample_kernel(pt_ref, q_ref, gr_ref, kn_ref, vn_ref, cp_ref, k_hbm, v_hbm, o_ref,
                       kbuf, vbuf, sem, acc_ref, *, n_groups):
    b = pl.program_id(0)
    nb = pl.num_programs(0)
    group = PAGES_PER_GROUP
    t_new = q_ref.shape[0]
    n_rows = t_new * H_F

    def page_copies(bi, g, sl):
        cps = []
        for j in range(group):
            page = pt_ref[bi, g * group + j]
            cps.append(pltpu.make_async_copy(k_hbm.at[page], kbuf.at[sl, j], sem.at[sl, 0]))
            cps.append(pltpu.make_async_copy(v_hbm.at[page], vbuf.at[sl, j], sem.at[sl, 1]))
        return cps

    def start_group(bi, g, sl):
        for cp in page_copies(bi, g, sl):
            cp.start()

    @pl.when(b == 0)
    def _():
        start_group(0, 0, 0)

    sub = lax.broadcasted_iota(jnp.int32, (H_F, W_HEADS), 0)
    lane = lax.broadcasted_iota(jnp.int32, (H_F, W_HEADS), 1)
    own_head = (lane // DH_F) == sub
    q = q_ref[...].astype(F32)
    qbd = jnp.concatenate(
        [jnp.where(own_head, jnp.broadcast_to(q[t:t + 1, :], (H_F, W_HEADS)), 0.0) for t in range(t_new)],
        axis=0).astype(BF16)

    gate0 = N_GATE_ROWS - H_F
    lf_new = gr_ref[0, gate0:N_GATE_ROWS, :]
    a_idx = lax.broadcasted_iota(jnp.int32, (t_new, t_new), 0)
    b_idx = lax.broadcasted_iota(jnp.int32, (t_new, t_new), 1)
    c_new = jnp.dot(lf_new, (a_idx <= b_idx).astype(F32), precision=HI, preferred_element_type=F32) * LOG2E
    cq = jnp.concatenate([c_new[:, t:t + 1] for t in range(t_new)], axis=0)

    acc_ref[...] = jnp.zeros_like(acc_ref)

    def body(g, carry):
        m_i, l_i = carry
        slot = g % 2

        @pl.when(g + 1 < n_groups)
        def _():
            start_group(b, g + 1, 1 - slot)

        @pl.when(jnp.logical_and(g + 1 == n_groups, b + 1 < nb))
        def _():
            start_group(b + 1, 0, 1 - slot)

        for cp in page_copies(b, g, slot):
            cp.wait()

        kcat = jnp.concatenate([kbuf[slot, j].astype(BF16) for j in range(group)], axis=1)
        s = jnp.dot(qbd, kcat, preferred_element_type=F32)
        first = pl.multiple_of(g * group, group)
        cpg = cp_ref[0, pl.ds(first, group)]
        bias = jnp.concatenate(
            [jnp.broadcast_to(cpg[j][None], (t_new, H_F, LANES)).reshape(n_rows, LANES) for j in range(group)],
            axis=1)
        s = s + (cq - bias)
        m_new = jnp.maximum(m_i, jnp.max(s, axis=-1, keepdims=True))
        p = jnp.exp2(s - m_new)
        alpha = jnp.exp2(m_i - m_new)
        l_new = alpha * l_i + jnp.sum(p, axis=-1, keepdims=True)
        vcat = jnp.concatenate([vbuf[slot, j].astype(BF16) for j in range(group)], axis=1)
        pv = lax.dot_general(p.astype(BF16), vcat, NT, preferred_element_type=F32)
        acc_ref[...] = alpha * acc_ref[...] + pv
        return m_new, l_new

    init = (jnp.full((n_rows, 1), -jnp.inf, F32), jnp.zeros((n_rows, 1), F32))
    m_i, l_i = lax.fori_loop(0, n_groups, body, init)

    kn = kn_ref[...].astype(BF16)
    vn = vn_ref[...].astype(BF16)
    s = lax.dot_general(qbd, kn, NT, preferred_element_type=F32)
    ck = jnp.broadcast_to(c_new[None], (t_new, H_F, t_new)).reshape(n_rows, t_new)
    s = s + (cq - ck)
    r_idx = lax.broadcasted_iota(jnp.int32, (n_rows, t_new), 0)
    k_idx = lax.broadcasted_iota(jnp.int32, (n_rows, t_new), 1)
    s = jnp.where(k_idx <= r_idx // H_F, s, -jnp.inf)
    m_new = jnp.maximum(m_i, jnp.max(s, axis=-1, keepdims=True))
    p = jnp.exp2(s - m_new)
    alpha = jnp.exp2(m_i - m_new)
    l_fin = alpha * l_i + jnp.sum(p, axis=-1, keepdims=True)
    acc = alpha * acc_ref[...] + jnp.dot(p.astype(BF16), vn, preferred_element_type=F32)
    out = acc / l_fin
    o_ref[...] = jnp.concatenate(
        [jnp.sum(jnp.where(own_head, out[t * H_F:(t + 1) * H_F, :], 0.0), axis=0, keepdims=True)
         for t in range(t_new)], axis=0).astype(o_ref.dtype)


def _fox_sample(page_table, qf, grow3, k_new, v_new, page_bias, k_pool, v_pool):
    batch, n_pages = page_table.shape
    t_new = qf.shape[0] // batch
    n_groups = n_pages // PAGES_PER_GROUP
    page_rows, page_len = k_pool.shape[1], k_pool.shape[2]
    tok = lambda w: pl.BlockSpec((t_new, w), lambda b, pt: (b, 0))
    return pl.pallas_call(
        functools.partial(_fox_sample_kernel, n_groups=n_groups),
        grid_spec=pltpu.PrefetchScalarGridSpec(
            num_scalar_prefetch=1,
            grid=(batch,),
            in_specs=[tok(W_HEADS),
                      pl.BlockSpec((1, N_GATE_ROWS, t_new), lambda b, pt: (b, 0, 0)),
                      tok(W_HEADS), tok(W_HEADS),
                      pl.BlockSpec((1, n_pages, H_F, LANES), lambda b, pt: (b, 0, 0, 0)),
                      pl.BlockSpec(memory_space=pl.ANY), pl.BlockSpec(memory_space=pl.ANY)],
            out_specs=tok(W_HEADS),
            scratch_shapes=[pltpu.VMEM((2, PAGES_PER_GROUP, page_rows, page_len), F32),
                            pltpu.VMEM((2, PAGES_PER_GROUP, page_rows, page_len), F32),
                            pltpu.SemaphoreType.DMA((2, 2)),
                            pltpu.VMEM((t_new * H_F, W_HEADS), F32)],
        ),
        out_shape=jax.ShapeDtypeStruct((batch * t_new, W_HEADS), F32),
        compiler_params=pltpu.CompilerParams(dimension_semantics=("arbitrary",), vmem_limit_bytes=48 * MIB),
        name="fox_sample",
    )(page_table, qf, grow3, k_new, v_new, page_bias, k_pool, v_pool)


def _ffn_kernel(x_ref, hm_ref, hf_ref, wo_ref, g2_ref, wg_ref, wu_ref, wd_ref, g3_ref, y_ref, *, final_norm):
    x1 = (x_ref[...]
          + jnp.dot(hm_ref[...].astype(BF16), wo_ref[0:W_HEADS, :], preferred_element_type=F32)
          + jnp.dot(hf_ref[...].astype(BF16), wo_ref[W_HEADS:2 * W_HEADS, :], preferred_element_type=F32))
    h = _rms(x1, g2_ref[...]).astype(BF16)
    gate = jnp.dot(h, wg_ref[...], preferred_element_type=F32)
    up = jnp.dot(h, wu_ref[...], preferred_element_type=F32)
    act = (gate * _sigmoid(gate) * up).astype(BF16)
    x2 = x1 + jnp.dot(act, wd_ref[...], preferred_element_type=F32)
    y_ref[...] = _rms(x2, g3_ref[...]) if final_norm else x2


def _merge_ffn(x2d, hm, hf, wo, g2, wg, wu, wd, g3, *, final_norm):
    rows_total, d_model = x2d.shape
    tm = min(FFN_ROWS, rows_total)
    row_spec = lambda w: pl.BlockSpec((tm, w), lambda i: (i, 0))
    return pl.pallas_call(
        functools.partial(_ffn_kernel, final_norm=final_norm),
        grid=(rows_total // tm,),
        in_specs=[row_spec(d_model), row_spec(W_HEADS), row_spec(W_HEADS), _const_spec(wo.shape),
                  _const_spec(g2.shape), _const_spec(wg.shape), _const_spec(wu.shape), _const_spec(wd.shape),
                  _const_spec(g3.shape)],
        out_specs=row_spec(d_model),
        out_shape=jax.ShapeDtypeStruct((rows_total, d_model), F32),
        compiler_params=pltpu.CompilerParams(dimension_semantics=("arbitrary",), vmem_limit_bytes=56 * MIB),
        name="merge_ffn",
    )(x2d, hm, hf, wo, g2, wg, wu, wd, g3)


def kernel(x_prompt, x_sample, cache_fox_k, cache_fox_v, cache_fox_logf, page_table, state_mlstm_C,
           state_mlstm_n, state_mlstm_m, norm_mix_g, w_in, b_m_igate, b_m_fgate, b_f_fgate, mlstm_head_g,
           w_out, norm_ffn_g, w_gate, w_up, w_down, norm_final_g):
    depth = w_in.shape[0]
    batch, seq, d_model = x_prompt.shape
    dec_batch, dec_seq, _ = x_sample.shape
    n_pool, page_size = cache_fox_k.shape[1], cache_fox_k.shape[2]
    xp = x_prompt.reshape(batch * seq, d_model)
    xs = x_sample.reshape(dec_batch * dec_seq, d_model)
    g_final = norm_final_g.reshape(1, d_model)
    pk, pv, plf, pc, pn, pm = [], [], [], [], [], []
    sk, sv, slf, sc, sn, sm = [], [], [], [], [], []
    gate0 = N_GATE_ROWS - H_F
    o_gm = 4 * W_HEADS
    o_qf = o_gm + 2 * H_M
    o_gf = o_qf + 3 * W_HEADS
    for l in range(depth):
        wt = jnp.swapaxes(w_in[l], 0, 1)
        wt_main = jnp.concatenate([wt[0:o_gm], wt[o_qf:o_qf + W_HEADS]], axis=0).astype(BF16)
        wt_kv = wt[o_qf + W_HEADS:o_gf].astype(BF16)
        wt_g = jnp.concatenate([wt[o_gm:o_qf], wt[o_gf:o_gf + H_F],
                                jnp.zeros((LANES - N_GATE_ROWS, d_model), F32)], axis=0).astype(BF16)
        bias = jnp.concatenate([b_m_igate[l], b_m_fgate[l], b_f_fgate[l],
                                jnp.zeros((LANES - N_GATE_ROWS,), F32)]).astype(F32)
        bcol = bias.reshape(1, LANES)
        brow = bias[:N_GATE_ROWS].reshape(N_GATE_ROWS, 1)
        g_mix = norm_mix_g[l].reshape(1, d_model)
        g_ffn = norm_ffn_g[l].reshape(1, d_model)
        head_g = mlstm_head_g[l].reshape(1, W_HEADS)
        wo = w_out[l].astype(BF16)
        wg = w_gate[l].astype(BF16)
        wu = w_up[l].astype(BF16)
        wd = w_down[l].astype(BF16)

        qm, km, vm, om, qf, kt, vt, ktb, vtb, gcol, grow = _project(
            xp, g_mix, wt_main, wt_kv, wt_g, bcol, brow, batch=batch, seq=seq, kv_transposed=True, act_dtype=BF16)
        chunk = min(LANES, seq)
        grow3 = grow.reshape(N_GATE_ROWS, batch, seq).transpose(1, 0, 2)
        hm, c_p, n_p, m_p = _mlstm(
            qm, km, vm, om, gcol, grow3, head_g,
            jnp.zeros((batch, H_M, DH_M, DH_M), F32), jnp.zeros((batch, H_M, DH_M), F32),
            jnp.zeros((batch, H_M), F32), batch=batch, seq=seq, chunk=chunk, out_dtype=BF16)
        ccol, crow = _fox_cumsum(gcol, grow, batch=batch, seq=seq)
        hf = _fox_prompt(qf, ktb, vtb, ccol, crow, batch=batch, seq=seq)
        xp = _merge_ffn(xp, hm, hf, wo, g_ffn, wg, wu, wd, g_final, final_norm=(l == depth - 1))
        pk.append(kt.reshape(batch, H_F, DH_F, seq).transpose(0, 3, 1, 2))
        pv.append(vt.reshape(batch, H_F, DH_F, seq).transpose(0, 3, 1, 2))
        plf.append(grow3[:, gate0:, :].transpose(0, 2, 1))
        pc.append(c_p); pn.append(n_p); pm.append(m_p)

        qm, km, vm, om, qf, k_new, v_new, gcol, grow = _project(
            xs, g_mix, wt_main, wt_kv, wt_g, bcol, brow, batch=dec_batch, seq=dec_seq, kv_transposed=False,
            act_dtype=F32)
        grow3 = grow.reshape(N_GATE_ROWS, dec_batch, dec_seq).transpose(1, 0, 2)
        hm, c_s, n_s, m_s = _mlstm(
            qm, km, vm, om, gcol, grow3, head_g,
            state_mlstm_C[l].astype(F32), state_mlstm_n[l].astype(F32), state_mlstm_m[l].astype(F32),
            batch=dec_batch, seq=dec_seq, chunk=dec_seq, out_dtype=F32)
        k_pool = cache_fox_k[l].transpose(0, 2, 3, 1).reshape(n_pool, W_HEADS, page_size)
        v_pool = cache_fox_v[l].transpose(0, 2, 3, 1).reshape(n_pool, W_HEADS, page_size)
        lf_pool = cache_fox_logf[l].transpose(0, 2, 1)
        page_bias = _page_bias(page_table, lf_pool)
        hf = _fox_sample(page_table, qf, grow3, k_new, v_new, page_bias, k_pool, v_pool)
        xs = _merge_ffn(xs, hm, hf, wo, g_ffn, wg, wu, wd, g_final, final_norm=(l == depth - 1))
        sk.append(k_new.reshape(dec_batch, dec_seq, H_F, DH_F))
        sv.append(v_new.reshape(dec_batch, dec_seq, H_F, DH_F))
        slf.append(grow3[:, gate0:, :].transpose(0, 2, 1))
        sc.append(c_s); sn.append(n_s); sm.append(m_s)

    st = lambda a, ref: jnp.stack(a, axis=0).astype(ref.dtype)
    return (xp.reshape(batch, seq, d_model), xs.reshape(dec_batch, dec_seq, d_model),
            st(pk, cache_fox_k), st(pv, cache_fox_v), st(plf, cache_fox_logf),
            st(pc, state_mlstm_C), st(pn, state_mlstm_n), st(pm, state_mlstm_m),
            st(sk, cache_fox_k), st(sv, cache_fox_v), st(slf, cache_fox_logf),
            st(sc, state_mlstm_C), st(sn, state_mlstm_n), st(sm, state_mlstm_m))
```

```python
import functools

import jax
import jax.numpy as jnp
from jax import lax
from jax.experimental import pallas as pl
from jax.experimental.pallas import tpu as pltpu

F32 = jnp.float32
BF16 = jnp.bfloat16
HI = lax.Precision.HIGHEST
NT = (((1,), (1,)), ((), ()))
TN = (((0,), (0,)), ((), ()))

EPS = 1e-6
LOG2E = 1.4426950408889634
H_M = 4
DH_M = 128
H_F = 8
DH_F = 64
W_HEADS = 512
N_GATE_ROWS = 16
LANES = 128
MIB = 1024 * 1024

PROJ_ROWS = 512
FFN_ROWS = 512
MLSTM_BATCH = 8
FOX_BLOCK = 256
PAGES_PER_GROUP = 8


def _rms(x, g):
    return x * lax.rsqrt(jnp.mean(x * x, axis=-1, keepdims=True) + EPS) * g


def _log_sigmoid(x):
    return jnp.minimum(x, 0.0) - jnp.log1p(jnp.exp(-jnp.abs(x)))


def _sigmoid(x):
    return 1.0 / (1.0 + jnp.exp(-x))


def _const_spec(shape):
    return pl.BlockSpec(shape, lambda *_: (0,) * len(shape), pipeline_mode=pl.Buffered(1))


def _proj_kernel(x_ref, g_ref, wt_ref, wkv_ref, wg_ref, bcol_ref, brow_ref, *outs, kv_transposed):
    if kv_transposed:
        qm_ref, km_ref, vm_ref, om_ref, qf_ref, kf_ref, vf_ref, kfb_ref, vfb_ref, gcol_ref, grow_ref = outs
    else:
        qm_ref, km_ref, vm_ref, om_ref, qf_ref, kf_ref, vf_ref, gcol_ref, grow_ref = outs
    h = _rms(x_ref[...], g_ref[...]).astype(BF16)
    rows = h.shape[0]

    def mm(i):
        w = wt_ref[i * W_HEADS:(i + 1) * W_HEADS, :]
        return lax.dot_general(h, w, NT, preferred_element_type=F32)

    qm_ref[...] = mm(0).astype(qm_ref.dtype)
    km_ref[...] = (mm(1) * (DH_M ** -0.5)).astype(km_ref.dtype)
    vm_ref[...] = mm(2).astype(vm_ref.dtype)
    om_ref[...] = mm(3).astype(om_ref.dtype)
    qf_ref[...] = (mm(4) * (DH_F ** -0.5 * LOG2E)).astype(qf_ref.dtype)
    if kv_transposed:
        kt = lax.dot_general(wkv_ref[0:W_HEADS, :], h, NT, preferred_element_type=F32)
        kf_ref[0] = kt
        kfb_ref[0] = kt.astype(BF16)
        vt = lax.dot_general(wkv_ref[W_HEADS:2 * W_HEADS, :], h, NT, preferred_element_type=F32)
        vf_ref[0] = vt
        vfb_ref[0] = vt.astype(BF16)
    else:
        kf_ref[...] = lax.dot_general(h, wkv_ref[0:W_HEADS, :], NT, preferred_element_type=F32)
        vf_ref[...] = lax.dot_general(h, wkv_ref[W_HEADS:2 * W_HEADS, :], NT, preferred_element_type=F32)
    pre_c = lax.dot_general(h, wg_ref[...], NT, preferred_element_type=F32) + bcol_ref[...]
    lane = lax.broadcasted_iota(jnp.int32, (rows, LANES), 1)
    gcol_ref[...] = jnp.where(lane < H_M, pre_c, _log_sigmoid(pre_c))
    pre_r = lax.dot_general(wg_ref[0:N_GATE_ROWS, :], h, NT, preferred_element_type=F32) + brow_ref[...]
    row = lax.broadcasted_iota(jnp.int32, (N_GATE_ROWS, rows), 0)
    grow_ref[...] = jnp.where(row < H_M, pre_r, _log_sigmoid(pre_r))


def _project(x2d, g, wt_main, wt_kv, wt_g, bcol, brow, *, batch, seq, kv_transposed, act_dtype):
    rows_total, d_model = x2d.shape
    tm = min(PROJ_ROWS, rows_total)
    steps = rows_total // tm
    per_seq = max(seq // tm, 1)
    row_spec = lambda w: pl.BlockSpec((tm, w), lambda i: (i, 0))
    in_specs = [row_spec(d_model), _const_spec(g.shape), _const_spec(wt_main.shape), _const_spec(wt_kv.shape),
                _const_spec(wt_g.shape), _const_spec(bcol.shape), _const_spec(brow.shape)]
    act = jax.ShapeDtypeStruct((rows_total, W_HEADS), act_dtype)
    act32 = jax.ShapeDtypeStruct((rows_total, W_HEADS), F32)
    out_shape = [act, act, act, act32, act]
    out_specs = [row_spec(W_HEADS)] * 5
    if kv_transposed:
        kv_spec = pl.BlockSpec((1, W_HEADS, tm), lambda i: (i // per_seq, 0, i % per_seq))
        out_shape += [jax.ShapeDtypeStruct((batch, W_HEADS, seq), F32)] * 2
        out_shape += [jax.ShapeDtypeStruct((batch, W_HEADS, seq), BF16)] * 2
        out_specs += [kv_spec] * 4
    else:
        out_shape += [act32, act32]
        out_specs += [row_spec(W_HEADS)] * 2
    out_shape += [jax.ShapeDtypeStruct((rows_total, LANES), F32),
                  jax.ShapeDtypeStruct((N_GATE_ROWS, rows_total), F32)]
    out_specs += [row_spec(LANES), pl.BlockSpec((N_GATE_ROWS, tm), lambda i: (0, i))]
    return pl.pallas_call(
        functools.partial(_proj_kernel, kv_transposed=kv_transposed),
        grid=(steps,),
        in_specs=in_specs,
        out_specs=out_specs,
        out_shape=out_shape,
        compiler_params=pltpu.CompilerParams(dimension_semantics=("arbitrary",), vmem_limit_bytes=48 * MIB),
        name="proj",
    )(x2d, g, wt_main, wt_kv, wt_g, bcol, brow)


def _mlstm_kernel(q_ref, k_ref, v_ref, om_ref, gc_ref, gr_ref, hg_ref, c0_ref, n0_ref, m0_ref,
                  hm_ref, c_ref, n_ref, m_ref, *, bb, chunk):
    @pl.when(pl.program_id(1) == 0)
    def _():
        c_ref[...] = c0_ref[...]
        n_ref[...] = n0_ref[...]
        m_ref[...] = m0_ref[...]

    t_idx = lax.broadcasted_iota(jnp.int32, (chunk, chunk), 0)
    s_idx = lax.broadcasted_iota(jnp.int32, (chunk, chunk), 1)
    causal = s_idx <= t_idx
    tril = causal.astype(F32)
    triu = (t_idx <= s_idx).astype(F32)

    def per_batch(b, carry):
        gc = gc_ref[b]
        gr = gr_ref[b]
        bc = jnp.dot(tril, gc, precision=HI, preferred_element_type=F32)
        br = jnp.dot(gr, triu, precision=HI, preferred_element_type=F32)
        q = q_ref[b]
        k = k_ref[b]
        v = v_ref[b]
        om = om_ref[b]
        for h in range(H_M):
            sl = slice(h * DH_M, (h + 1) * DH_M)
            li_c = gc[:, h:h + 1]
            li_r = gr[h:h + 1, :]
            b_c = bc[:, H_M + h:H_M + h + 1]
            b_r = br[H_M + h:H_M + h + 1, :]
            m_prev = m_ref[pl.ds(b, 1), h:h + 1]
            dmat = jnp.where(causal, b_c - b_r + li_r, -jnp.inf)
            inter = m_prev + b_c
            m_t = jnp.maximum(inter, jnp.max(dmat, axis=-1, keepdims=True))
            qh = q[:, sl].astype(BF16)
            kh = k[:, sl].astype(BF16)
            vh = v[:, sl].astype(BF16)
            smat = lax.dot_general(qh, kh, NT, preferred_element_type=F32) * jnp.exp(dmat - m_t)
            w_inter = jnp.exp(inter - m_t)
            c_prev = c_ref[b, h]
            cq = lax.dot_general(qh, c_prev.astype(BF16), NT, preferred_element_type=F32)
            num = w_inter * cq + jnp.dot(smat.astype(BF16), vh, preferred_element_type=F32)
            n_prev = n_ref[b, h:h + 1, :]
            nq = jnp.sum(qh.astype(F32) * n_prev, axis=-1, keepdims=True)
            den = w_inter * nq + jnp.sum(smat, axis=-1, keepdims=True)
            hh = num / jnp.maximum(jnp.abs(den), jnp.exp(-m_t))
            hn = hh * lax.rsqrt(jnp.mean(hh * hh, axis=-1, keepdims=True) + EPS) * hg_ref[:, sl]
            hm_ref[b, :, sl] = (hn * _sigmoid(om[:, sl].astype(F32))).astype(hm_ref.dtype)
            b_last = b_c[chunk - 1:chunk, :]
            m_new = m_t[chunk - 1:chunk, :]
            decay = jnp.exp(m_prev + b_last - m_new)
            w_s = jnp.exp(li_c + b_last - b_c - m_new)
            vw = (vh.astype(F32) * w_s).astype(BF16)
            c_ref[b, h] = decay * c_prev + lax.dot_general(vw, kh, TN, preferred_element_type=F32)
            n_ref[b, h:h + 1, :] = decay * n_prev + jnp.sum(kh.astype(F32) * w_s, axis=0, keepdims=True)
            m_ref[pl.ds(b, 1), h:h + 1] = m_new
        return carry

    lax.fori_loop(0, bb, per_batch, 0)


def _mlstm(qm, km, vm, om, gcol, grow3, head_g, c0, n0, m0, *, batch, seq, chunk, out_dtype):
    bb = MLSTM_BATCH
    n_chunks = seq // chunk
    as3 = lambda a: a.reshape(batch, seq, a.shape[-1])
    tok = lambda w: pl.BlockSpec((bb, chunk, w), lambda g, c: (g, c, 0))
    state = lambda shape: pl.BlockSpec((bb,) + shape, lambda g, c: (g,) + (0,) * len(shape))
    in_specs = [tok(W_HEADS), tok(W_HEADS), tok(W_HEADS), tok(W_HEADS), tok(LANES),
                pl.BlockSpec((bb, N_GATE_ROWS, chunk), lambda g, c: (g, 0, c)),
                _const_spec(head_g.shape),
                state((H_M, DH_M, DH_M)), state((H_M, DH_M)), state((H_M,))]
    out_specs = [tok(W_HEADS), state((H_M, DH_M, DH_M)), state((H_M, DH_M)), state((H_M,))]
    out_shape = [jax.ShapeDtypeStruct((batch, seq, W_HEADS), out_dtype),
                 jax.ShapeDtypeStruct(c0.shape, F32), jax.ShapeDtypeStruct(n0.shape, F32),
                 jax.ShapeDtypeStruct(m0.shape, F32)]
    hm, c_new, n_new, m_new = pl.pallas_call(
        functools.partial(_mlstm_kernel, bb=bb, chunk=chunk),
        grid=(batch // bb, n_chunks),
        in_specs=in_specs,
        out_specs=out_specs,
        out_shape=out_shape,
        compiler_params=pltpu.CompilerParams(dimension_semantics=("arbitrary", "arbitrary"),
                                             vmem_limit_bytes=48 * MIB),
        name="mlstm",
    )(as3(qm), as3(km), as3(vm), as3(om), as3(gcol), grow3, head_g, c0, n0, m0)
    return hm.reshape(batch * seq, W_HEADS), c_new, n_new, m_new


def _fox_cumsum_kernel(gc_ref, gr_ref, cc_ref, cr_ref):
    seq = gc_ref.shape[1]
    t_idx = lax.broadcasted_iota(jnp.int32, (LANES, LANES), 0)
    s_idx = lax.broadcasted_iota(jnp.int32, (LANES, LANES), 1)
    tril = (s_idx <= t_idx).astype(F32)
    triu = (t_idx <= s_idx).astype(F32)
    carry_c = jnp.zeros((1, LANES), F32)
    carry_r = jnp.zeros((N_GATE_ROWS, 1), F32)
    for j in range(seq // LANES):
        blk = slice(j * LANES, (j + 1) * LANES)
        cb = jnp.dot(tril, gc_ref[0, blk, :], precision=HI, preferred_element_type=F32) + carry_c
        cc_ref[0, blk, :] = cb * LOG2E
        carry_c = cb[LANES - 1:LANES, :]
        rb = jnp.dot(gr_ref[:, blk], triu, precision=HI, preferred_element_type=F32) + carry_r
        cr_ref[:, blk] = rb * LOG2E
        carry_r = rb[:, LANES - 1:LANES]


def _fox_cumsum(gcol, grow, *, batch, seq):
    return pl.pallas_call(
        _fox_cumsum_kernel,
        grid=(batch,),
        in_specs=[pl.BlockSpec((1, seq, LANES), lambda b: (b, 0, 0)),
                  pl.BlockSpec((N_GATE_ROWS, seq), lambda b: (0, b))],
        out_specs=[pl.BlockSpec((1, seq, LANES), lambda b: (b, 0, 0)),
                   pl.BlockSpec((N_GATE_ROWS, seq), lambda b: (0, b))],
        out_shape=[jax.ShapeDtypeStruct((batch, seq, LANES), F32),
                   jax.ShapeDtypeStruct((N_GATE_ROWS, batch * seq), F32)],
        compiler_params=pltpu.CompilerParams(dimension_semantics=("arbitrary",)),
        name="fox_cumsum",
    )(gcol.reshape(batch, seq, LANES), grow)


def _fox_prompt_kernel(q_ref, kt_ref, vt_ref, cc_ref, cr_ref, o_ref, qs_ref, cq_ref, m_ref, l_ref, acc_ref, *, blk):
    i = pl.program_id(1)
    n_pairs = H_F // 2
    lane_tiles = blk // LANES
    lane = lax.broadcasted_iota(jnp.int32, (blk, LANES), 1)
    low_half = lane < DH_F
    r_idx = lax.broadcasted_iota(jnp.int32, (blk, blk), 0)
    c_idx = lax.broadcasted_iota(jnp.int32, (blk, blk), 1)
    diag_mask = jnp.concatenate([c_idx <= r_idx] * 2, axis=0)
    gate0 = N_GATE_ROWS - H_F

    for pair in range(n_pairs):
        rows = slice(pair * LANES, (pair + 1) * LANES)
        q_pair = q_ref[:, rows]
        zero = jnp.zeros_like(q_pair)
        qs_ref[pair, 0:blk, :] = jnp.where(low_half, q_pair, zero)
        qs_ref[pair, blk:2 * blk, :] = jnp.where(low_half, zero, q_pair)
        for e in range(2):
            col = gate0 + 2 * pair + e
            cq_ref[pair, e * blk:(e + 1) * blk, :] = jnp.broadcast_to(cc_ref[:, col:col + 1], (blk, LANES))
    m_ref[...] = jnp.full(m_ref.shape, -jnp.inf, F32)
    l_ref[...] = jnp.zeros(l_ref.shape, F32)
    acc_ref[...] = jnp.zeros(acc_ref.shape, F32)

    def step(j, masked):
        cols = pl.ds(pl.multiple_of(j * blk, blk), blk)
        for pair in range(n_pairs):
            rows = slice(pair * LANES, (pair + 1) * LANES)
            kj = kt_ref[0, rows, cols]
            vj = vt_ref[0, rows, cols]
            s = jnp.dot(qs_ref[pair], kj, preferred_element_type=F32)
            ck = jnp.concatenate(
                [jnp.broadcast_to(cr_ref[gate0 + 2 * pair + e:gate0 + 2 * pair + e + 1, cols], (blk, blk))
                 for e in range(2)], axis=0)
            s = s + (jnp.concatenate([cq_ref[pair]] * lane_tiles, axis=1) - ck)
            if masked:
                s = jnp.where(diag_mask, s, -jnp.inf)
            m_old = m_ref[pair]
            m_new = jnp.maximum(m_old, jnp.max(s, axis=-1, keepdims=True))
            p = jnp.exp2(s - jnp.concatenate([m_new] * lane_tiles, axis=1))
            alpha = jnp.exp2(m_old - m_new)
            p_lanes = p[:, 0:LANES]
            for t in range(1, lane_tiles):
                p_lanes = p_lanes + p[:, t * LANES:(t + 1) * LANES]
            l_ref[pair] = alpha * l_ref[pair] + p_lanes
            pv = lax.dot_general(p.astype(BF16), vj, NT, preferred_element_type=F32)
            acc_ref[pair] = alpha * acc_ref[pair] + pv
            m_ref[pair] = m_new

    def loop_body(j, carry):
        step(j, False)
        return carry

    lax.fori_loop(0, i, loop_body, 0)
    step(i, True)
    for pair in range(n_pairs):
        rows = slice(pair * LANES, (pair + 1) * LANES)
        out = acc_ref[pair] / jnp.sum(l_ref[pair], axis=-1, keepdims=True)
        o_ref[:, rows] = jnp.where(low_half, out[0:blk], out[blk:2 * blk]).astype(o_ref.dtype)


def _fox_prompt(qf, ktb, vtb, ccol, crow, *, batch, seq):
    blk = FOX_BLOCK
    nq = seq // blk
    return pl.pallas_call(
        functools.partial(_fox_prompt_kernel, blk=blk),
        grid=(batch, nq),
        in_specs=[pl.BlockSpec((blk, W_HEADS), lambda b, i: (b * nq + i, 0)),
                  pl.BlockSpec((1, W_HEADS, seq), lambda b, i: (b, 0, 0)),
                  pl.BlockSpec((1, W_HEADS, seq), lambda b, i: (b, 0, 0)),
                  pl.BlockSpec((blk, LANES), lambda b, i: (b * nq + i, 0)),
                  pl.BlockSpec((N_GATE_ROWS, seq), lambda b, i: (0, b))],
        out_specs=pl.BlockSpec((blk, W_HEADS), lambda b, i: (b * nq + i, 0)),
        out_shape=jax.ShapeDtypeStruct((batch * seq, W_HEADS), BF16),
        scratch_shapes=[pltpu.VMEM((H_F // 2, 2 * blk, LANES), BF16)] + [pltpu.VMEM((H_F // 2, 2 * blk, LANES), F32)] * 4,
        compiler_params=pltpu.CompilerParams(dimension_semantics=("arbitrary", "arbitrary"),
                                             vmem_limit_bytes=48 * MIB),
        name="fox_prompt",
    )(qf, ktb, vtb, ccol.reshape(batch * seq, LANES), crow)


def _page_bias_kernel(pt_ref, lf_hbm, o_ref, buf, sem):
    b = pl.program_id(0)
    nb = pl.num_programs(0)
    n_pages = buf.shape[1]
    slot = b % 2

    def page_copy(bi, p, sl):
        return pltpu.make_async_copy(lf_hbm.at[pt_ref[bi, p]], buf.at[sl, p], sem.at[sl])

    def start_all(bi, sl):
        def body(p, c):
            page_copy(bi, p, sl).start()
            return c
        lax.fori_loop(0, n_pages, body, 0)

    @pl.when(b == 0)
    def _():
        start_all(0, 0)

    @pl.when(b + 1 < nb)
    def _():
        start_all(b + 1, 1 - slot)

    def wait_body(p, c):
        page_copy(b, p, slot).wait()
        return c
    lax.fori_loop(0, n_pages, wait_body, 0)

    x = buf[slot].reshape(n_pages * H_F, LANES)
    t_idx = lax.broadcasted_iota(jnp.int32, (LANES, LANES), 0)
    s_idx = lax.broadcasted_iota(jnp.int32, (LANES, LANES), 1)
    later = (t_idx > s_idx).astype(F32)
    within = jnp.dot(x, later, precision=HI, preferred_element_type=F32)
    total = jnp.sum(x, axis=-1, keepdims=True)
    run = jnp.zeros((H_F, LANES), F32)
    for p in range(n_pages - 1, -1, -1):
        rows = slice(p * H_F, (p + 1) * H_F)
        o_ref[0, p] = (within[rows] + run) * (-LOG2E)
        run = run + total[rows]


def _page_bias(page_table, lf_pool):
    batch, n_pages = page_table.shape
    return pl.pallas_call(
        _page_bias_kernel,
        grid_spec=pltpu.PrefetchScalarGridSpec(
            num_scalar_prefetch=1,
            grid=(batch,),
            in_specs=[pl.BlockSpec(memory_space=pl.ANY)],
            out_specs=pl.BlockSpec((1, n_pages, H_F, LANES), lambda b, pt: (b, 0, 0, 0)),
            scratch_shapes=[pltpu.VMEM((2, n_pages, H_F, LANES), F32), pltpu.SemaphoreType.DMA((2,))],
        ),
        out_shape=jax.ShapeDtypeStruct((batch, n_pages, H_F, LANES), F32),
        compiler_params=pltpu.CompilerParams(dimension_semantics=("arbitrary",)),
        name="page_bias",
    )(page_table, lf_pool)


def _fox_sample_kernel(pt_ref, q_ref, gr_ref, kn_ref, vn_ref, cp_ref, k_hbm, v_hbm, o_ref,
                       kbuf, vbuf, sem, acc_ref, *, n_groups):
    b = pl.program_id(0)
    nb = pl.num_programs(0)
    group = PAGES_PER_GROUP
    t_new = q_ref.shape[0]
    n_rows = t_new * H_F

    def page_copies(bi, g, sl):
        cps = []
        for j in range(group):
            page = pt_ref[bi, g * group + j]
            cps.append(pltpu.make_async_copy(k_hbm.at[page], kbuf.at[sl, j], sem.at[sl, 0]))
            cps.append(pltpu.make_async_copy(v_hbm.at[page], vbuf.at[sl, j], sem.at[sl, 1]))
        return cps

    def start_group(bi, g, sl):
        for n, cp in enumerate(page_copies(bi, g, sl)):
            cp.start(priority=n % 2)

    @pl.when(b == 0)
    def _():
        start_group(0, 0, 0)

    sub = lax.broadcasted_iota(jnp.int32, (H_F, W_HEADS), 0)
    lane = lax.broadcasted_iota(jnp.int32, (H_F, W_HEADS), 1)
    own_head = (lane // DH_F) == sub
    q = q_ref[...].astype(F32)
    qbd = jnp.concatenate(
        [jnp.where(own_head, jnp.broadcast_to(q[t:t + 1, :], (H_F, W_HEADS)), 0.0) for t in range(t_new)],
        axis=0).astype(BF16)

    gate0 = N_GATE_ROWS - H_F
    lf_new = gr_ref[0, gate0:N_GATE_ROWS, :]
    a_idx = lax.broadcasted_iota(jnp.int32, (t_new, t_new), 0)
    b_idx = lax.broadcasted_iota(jnp.int32, (t_new, t_new), 1)
    c_new = jnp.dot(lf_new, (a_idx <= b_idx).astype(F32), precision=HI, preferred_element_type=F32) * LOG2E
    cq = jnp.concatenate([c_new[:, t:t + 1] for t in range(t_new)], axis=0)

    acc_ref[...] = jnp.zeros_like(acc_ref)

    def body(g, carry):
        m_i, l_i = carry
        slot = g % 2

        @pl.when(g + 1 < n_groups)
        def _():
            start_group(b, g + 1, 1 - slot)

        @pl.when(jnp.logical_and(g + 1 == n_groups, b + 1 < nb))
        def _():
            start_group(b + 1, 0, 1 - slot)

        for cp in page_copies(b, g, slot):
            cp.wait()

        kcat = jnp.concatenate([kbuf[slot, j].astype(BF16) for j in range(group)], axis=1)
        s = jnp.dot(qbd, kcat, preferred_element_type=F32)
        first = pl.multiple_of(g * group, group)
        cpg = cp_ref[0, pl.ds(first, group)]
        bias = jnp.concatenate(
            [jnp.broadcast_to(cpg[j][None], (t_new, H_F, LANES)).reshape(n_rows, LANES) for j in range(group)],
            axis=1)
        s = s + (cq - bias)
        m_new = jnp.maximum(m_i, jnp.max(s, axis=-1, keepdims=True))
        p = jnp.exp2(s - m_new)
        alpha = jnp.exp2(m_i - m_new)
        l_new = alpha * l_i + jnp.sum(p, axis=-1, keepdims=True)
        vcat = jnp.concatenate([vbuf[slot, j].astype(BF16) for j in range(group)], axis=1)
        pv = lax.dot_general(p.astype(BF16), vcat, NT, preferred_element_type=F32)
        acc_ref[...] = alpha * acc_ref[...] + pv
        return m_new, l_new

    init = (jnp.full((n_rows, 1), -jnp.inf, F32), jnp.zeros((n_rows, 1), F32))
    m_i, l_i = lax.fori_loop(0, n_groups, body, init)

    kn = kn_ref[...].astype(BF16)
    vn = vn_ref[...].astype(BF16)
    s = lax.dot_general(qbd, kn, NT, preferred_element_type=F32)
    ck = jnp.broadcast_to(c_new[None], (t_new, H_F, t_new)).reshape(n_rows, t_new)
    s = s + (cq - ck)
    r_idx = lax.broadcasted_iota(jnp.int32, (n_rows, t_new), 0)
    k_idx = lax.broadcasted_iota(jnp.int32, (n_rows, t_new), 1)
    s = jnp.where(k_idx <= r_idx // H_F, s, -jnp.inf)
    m_new = jnp.maximum(m_i, jnp.max(s, axis=-1, keepdims=True))
    p = jnp.exp2(s - m_new)
    alpha = jnp.exp2(m_i - m_new)
    l_fin = alpha * l_i + jnp.sum(p, axis=-1, keepdims=True)
    acc = alpha * acc_ref[...] + jnp.dot(p.astype(BF16), vn, preferred_element_type=F32)
    out = acc / l_fin
    o_ref[...] = jnp.concatenate(
        [jnp.sum(jnp.where(own_head, out[t * H_F:(t + 1) * H_F, :], 0.0), axis=0, keepdims=True)
         for t in range(t_new)], axis=0).astype(o_ref.dtype)


def _fox_sample(page_table, qf, grow3, k_new, v_new, page_bias, k_pool, v_pool):
    batch, n_pages = page_table.shape
    t_new = qf.shape[0] // batch
    n_groups = n_pages // PAGES_PER_GROUP
    page_rows, page_len = k_pool.shape[1], k_pool.shape[2]
    tok = lambda w: pl.BlockSpec((t_new, w), lambda b, pt: (b, 0))
    return pl.pallas_call(
        functools.partial(_fox_sample_kernel, n_groups=n_groups),
        grid_spec=pltpu.PrefetchScalarGridSpec(
            num_scalar_prefetch=1,
            grid=(batch,),
            in_specs=[tok(W_HEADS),
                      pl.BlockSpec((1, N_GATE_ROWS, t_new), lambda b, pt: (b, 0, 0)),
                      tok(W_HEADS), tok(W_HEADS),
                      pl.BlockSpec((1, n_pages, H_F, LANES), lambda b, pt: (b, 0, 0, 0)),
                      pl.BlockSpec(memory_space=pl.ANY), pl.BlockSpec(memory_space=pl.ANY)],
            out_specs=tok(W_HEADS),
            scratch_shapes=[pltpu.VMEM((2, PAGES_PER_GROUP, page_rows, page_len), F32),
                            pltpu.VMEM((2, PAGES_PER_GROUP, page_rows, page_len), F32),
                            pltpu.SemaphoreType.DMA((2, 2)),
                            pltpu.VMEM((t_new * H_F, W_HEADS), F32)],
        ),
        out_shape=jax.ShapeDtypeStruct((batch * t_new, W_HEADS), F32),
        compiler_params=pltpu.CompilerParams(dimension_semantics=("arbitrary",), vmem_limit_bytes=48 * MIB),
        name="fox_sample",
    )(page_table, qf, grow3, k_new, v_new, page_bias, k_pool, v_pool)


def _ffn_kernel(x_ref, hm_ref, hf_ref, wo_ref, g2_ref, wg_ref, wu_ref, wd_ref, g3_ref, y_ref, *, final_norm):
    x1 = (x_ref[...]
          + jnp.dot(hm_ref[...].astype(BF16), wo_ref[0:W_HEADS, :], preferred_element_type=F32)
          + jnp.dot(hf_ref[...].astype(BF16), wo_ref[W_HEADS:2 * W_HEADS, :], preferred_element_type=F32))
    h = _rms(x1, g2_ref[...]).astype(BF16)
    gate = jnp.dot(h, wg_ref[...], preferred_element_type=F32)
    up = jnp.dot(h, wu_ref[...], preferred_element_type=F32)
    act = (gate * _sigmoid(gate) * up).astype(BF16)
    x2 = x1 + jnp.dot(act, wd_ref[...], preferred_element_type=F32)
    y_ref[...] = _rms(x2, g3_ref[...]) if final_norm else x2


def _merge_ffn(x2d, hm, hf, wo, g2, wg, wu, wd, g3, *, final_norm):
    rows_total, d_model = x2d.shape
    tm = min(FFN_ROWS, rows_total)
    row_spec = lambda w: pl.BlockSpec((tm, w), lambda i: (i, 0))
    return pl.pallas_call(
        functools.partial(_ffn_kernel, final_norm=final_norm),
        grid=(rows_total // tm,),
        in_specs=[row_spec(d_model), row_spec(W_HEADS), row_spec(W_HEADS), _const_spec(wo.shape),
                  _const_spec(g2.shape), _const_spec(wg.shape), _const_spec(wu.shape), _const_spec(wd.shape),
                  _const_spec(g3.shape)],
        out_specs=row_spec(d_model),
        out_shape=jax.ShapeDtypeStruct((rows_total, d_model), F32),
        compiler_params=pltpu.CompilerParams(dimension_semantics=("arbitrary",), vmem_limit_bytes=56 * MIB),
        name="merge_ffn",
    )(x2d, hm, hf, wo, g2, wg, wu, wd, g3)


def kernel(x_prompt, x_sample, cache_fox_k, cache_fox_v, cache_fox_logf, page_table, state_mlstm_C,
           state_mlstm_n, state_mlstm_m, norm_mix_g, w_in, b_m_igate, b_m_fgate, b_f_fgate, mlstm_head_g,
           w_out, norm_ffn_g, w_gate, w_up, w_down, norm_final_g):
    depth = w_in.shape[0]
    batch, seq, d_model = x_prompt.shape
    dec_batch, dec_seq, _ = x_sample.shape
    n_pool, page_size = cache_fox_k.shape[1], cache_fox_k.shape[2]
    xp = x_prompt.reshape(batch * seq, d_model)
    xs = x_sample.reshape(dec_batch * dec_seq, d_model)
    g_final = norm_final_g.reshape(1, d_model)
    pk, pv, plf, pc, pn, pm = [], [], [], [], [], []
    sk, sv, slf, sc, sn, sm = [], [], [], [], [], []
    gate0 = N_GATE_ROWS - H_F
    o_gm = 4 * W_HEADS
    o_qf = o_gm + 2 * H_M
    o_gf = o_qf + 3 * W_HEADS
    for l in range(depth):
        wt = jnp.swapaxes(w_in[l], 0, 1)
        wt_main = jnp.concatenate([wt[0:o_gm], wt[o_qf:o_qf + W_HEADS]], axis=0).astype(BF16)
        wt_kv = wt[o_qf + W_HEADS:o_gf].astype(BF16)
        wt_g = jnp.concatenate([wt[o_gm:o_qf], wt[o_gf:o_gf + H_F],
                                jnp.zeros((LANES - N_GATE_ROWS, d_model), F32)], axis=0).astype(BF16)
        bias = jnp.concatenate([b_m_igate[l], b_m_fgate[l], b_f_fgate[l],
                                jnp.zeros((LANES - N_GATE_ROWS,), F32)]).astype(F32)
        bcol = bias.reshape(1, LANES)
        brow = bias[:N_GATE_ROWS].reshape(N_GATE_ROWS, 1)
        g_mix = norm_mix_g[l].reshape(1, d_model)
        g_ffn = norm_ffn_g[l].reshape(1, d_model)
        head_g = mlstm_head_g[l].reshape(1, W_HEADS)
        wo = w_out[l].astype(BF16)
        wg = w_gate[l].astype(BF16)
        wu = w_up[l].astype(BF16)
        wd = w_down[l].astype(BF16)

        qm, km, vm, om, qf, kt, vt, ktb, vtb, gcol, grow = _project(
            xp, g_mix, wt_main, wt_kv, wt_g, bcol, brow, batch=batch, seq=seq, kv_transposed=True, act_dtype=BF16)
        chunk = min(LANES, seq)
        grow3 = grow.reshape(N_GATE_ROWS, batch, seq).transpose(1, 0, 2)
        hm, c_p, n_p, m_p = _mlstm(
            qm, km, vm, om, gcol, grow3, head_g,
            jnp.zeros((batch, H_M, DH_M, DH_M), F32), jnp.zeros((batch, H_M, DH_M), F32),
            jnp.zeros((batch, H_M), F32), batch=batch, seq=seq, chunk=chunk, out_dtype=BF16)
        ccol, crow = _fox_cumsum(gcol, grow, batch=batch, seq=seq)
        hf = _fox_prompt(qf, ktb, vtb, ccol, crow, batch=batch, seq=seq)
        xp = _merge_ffn(xp, hm, hf, wo, g_ffn, wg, wu, wd, g_final, final_norm=(l == depth - 1))
        pk.append(kt.reshape(batch, H_F, DH_F, seq).transpose(0, 3, 1, 2))
        pv.append(vt.reshape(batch, H_F, DH_F, seq).transpose(0, 3, 1, 2))
        plf.append(grow3[:, gate0:, :].transpose(0, 2, 1))
        pc.append(c_p); pn.append(n_p); pm.append(m_p)

        qm, km, vm, om, qf, k_new, v_new, gcol, grow = _project(
            xs, g_mix, wt_main, wt_kv, wt_g, bcol, brow, batch=dec_batch, seq=dec_seq, kv_transposed=False,
            act_dtype=F32)
        grow3 = grow.reshape(N_GATE_ROWS, dec_batch, dec_seq).transpose(1, 0, 2)
        hm, c_s, n_s, m_s = _mlstm(
            qm, km, vm, om, gcol, grow3, head_g,
            state_mlstm_C[l].astype(F32), state_mlstm_n[l].astype(F32), state_mlstm_m[l].astype(F32),
            batch=dec_batch, seq=dec_seq, chunk=dec_seq, out_dtype=F32)
        k_pool = cache_fox_k[l].transpose(0, 2, 3, 1).reshape(n_pool, W_HEADS, page_size)
        v_pool = cache_fox_v[l].transpose(0, 2, 3, 1).reshape(n_pool, W_HEADS, page_size)
        lf_pool = cache_fox_logf[l].transpose(0, 2, 1)
        page_bias = _page_bias(page_table, lf_pool)
        hf = _fox_sample(page_table, qf, grow3, k_new, v_new, page_bias, k_pool, v_pool)
        xs = _merge_ffn(xs, hm, hf, wo, g_ffn, wg, wu, wd, g_final, final_norm=(l == depth - 1))
        sk.append(k_new.reshape(dec_batch, dec_seq, H_F, DH_F))
        sv.append(v_new.reshape(dec_batch, dec_seq, H_F, DH_F))
        slf.append(grow3[:, gate0:, :].transpose(0, 2, 1))
        sc.append(c_s); sn.append(n_s); sm.append(m_s)

    st = lambda a, ref: jnp.stack(a, axis=0).astype(ref.dtype)
    return (xp.reshape(batch, seq, d_model), xs.reshape(dec_batch, dec_seq, d_model),
            st(pk, cache_fox_k), st(pv, cache_fox_v), st(plf, cache_fox_logf),
            st(pc, state_mlstm_C), st(pn, state_mlstm_n), st(pm, state_mlstm_m),
            st(sk, cache_fox_k), st(sv, cache_fox_v), st(slf, cache_fox_logf),
            st(sc, state_mlstm_C), st(sn, state_mlstm_n), st(sm, state_mlstm_m))
```

```python
import functools

import jax
import jax.numpy as jnp
from jax import lax
from jax.experimental import pallas as pl
from jax.experimental.pallas import tpu as pltpu

F32 = jnp.float32
BF16 = jnp.bfloat16
HI = lax.Precision.HIGHEST
NT = (((1,), (1,)), ((), ()))
TN = (((0,), (0,)), ((), ()))

EPS = 1e-6
LOG2E = 1.4426950408889634
H_M = 4
DH_M = 128
H_F = 8
DH_F = 64
W_HEADS = 512
N_GATE_ROWS = 16
LANES = 128
MIB = 1024 * 1024

PROJ_ROWS = 512
FFN_ROWS = 512
MLSTM_BATCH = 8
FOX_BLOCK = 256
PAGES_PER_GROUP = 8
PAGE_SLOTS = 4


def _rms(x, g):
    return x * lax.rsqrt(jnp.mean(x * x, axis=-1, keepdims=True) + EPS) * g


def _log_sigmoid(x):
    return jnp.minimum(x, 0.0) - jnp.log1p(jnp.exp(-jnp.abs(x)))


def _sigmoid(x):
    return 1.0 / (1.0 + jnp.exp(-x))


def _const_spec(shape):
    return pl.BlockSpec(shape, lambda *_: (0,) * len(shape), pipeline_mode=pl.Buffered(1))


def _proj_kernel(x_ref, g_ref, wt_ref, wkv_ref, wg_ref, bcol_ref, brow_ref, *outs, kv_transposed):
    if kv_transposed:
        qm_ref, km_ref, vm_ref, om_ref, qf_ref, kf_ref, vf_ref, kfb_ref, vfb_ref, gcol_ref, grow_ref = outs
    else:
        qm_ref, km_ref, vm_ref, om_ref, qf_ref, kf_ref, vf_ref, gcol_ref, grow_ref = outs
    h = _rms(x_ref[...], g_ref[...]).astype(BF16)
    rows = h.shape[0]

    def mm(i):
        w = wt_ref[i * W_HEADS:(i + 1) * W_HEADS, :]
        return lax.dot_general(h, w, NT, preferred_element_type=F32)

    qm_ref[...] = mm(0).astype(qm_ref.dtype)
    km_ref[...] = (mm(1) * (DH_M ** -0.5)).astype(km_ref.dtype)
    vm_ref[...] = mm(2).astype(vm_ref.dtype)
    om_ref[...] = mm(3).astype(om_ref.dtype)
    qf_ref[...] = (mm(4) * (DH_F ** -0.5 * LOG2E)).astype(qf_ref.dtype)
    if kv_transposed:
        kt = lax.dot_general(wkv_ref[0:W_HEADS, :], h, NT, preferred_element_type=F32)
        kf_ref[0] = kt
        kfb_ref[0] = kt.astype(BF16)
        vt = lax.dot_general(wkv_ref[W_HEADS:2 * W_HEADS, :], h, NT, preferred_element_type=F32)
        vf_ref[0] = vt
        vfb_ref[0] = vt.astype(BF16)
    else:
        kf_ref[...] = lax.dot_general(h, wkv_ref[0:W_HEADS, :], NT, preferred_element_type=F32)
        vf_ref[...] = lax.dot_general(h, wkv_ref[W_HEADS:2 * W_HEADS, :], NT, preferred_element_type=F32)
    pre_c = lax.dot_general(h, wg_ref[...], NT, preferred_element_type=F32) + bcol_ref[...]
    lane = lax.broadcasted_iota(jnp.int32, (rows, LANES), 1)
    gcol_ref[...] = jnp.where(lane < H_M, pre_c, _log_sigmoid(pre_c))
    pre_r = lax.dot_general(wg_ref[0:N_GATE_ROWS, :], h, NT, preferred_element_type=F32) + brow_ref[...]
    row = lax.broadcasted_iota(jnp.int32, (N_GATE_ROWS, rows), 0)
    grow_ref[...] = jnp.where(row < H_M, pre_r, _log_sigmoid(pre_r))


def _project(x2d, g, wt_main, wt_kv, wt_g, bcol, brow, *, batch, seq, kv_transposed, act_dtype):
    rows_total, d_model = x2d.shape
    tm = min(PROJ_ROWS, rows_total)
    steps = rows_total // tm
    per_seq = max(seq // tm, 1)
    row_spec = lambda w: pl.BlockSpec((tm, w), lambda i: (i, 0))
    in_specs = [row_spec(d_model), _const_spec(g.shape), _const_spec(wt_main.shape), _const_spec(wt_kv.shape),
                _const_spec(wt_g.shape), _const_spec(bcol.shape), _const_spec(brow.shape)]
    act = jax.ShapeDtypeStruct((rows_total, W_HEADS), act_dtype)
    act32 = jax.ShapeDtypeStruct((rows_total, W_HEADS), F32)
    out_shape = [act, act, act, act32, act]
    out_specs = [row_spec(W_HEADS)] * 5
    if kv_transposed:
        kv_spec = pl.BlockSpec((1, W_HEADS, tm), lambda i: (i // per_seq, 0, i % per_seq))
        out_shape += [jax.ShapeDtypeStruct((batch, W_HEADS, seq), F32)] * 2
        out_shape += [jax.ShapeDtypeStruct((batch, W_HEADS, seq), BF16)] * 2
        out_specs += [kv_spec] * 4
    else:
        out_shape += [act32, act32]
        out_specs += [row_spec(W_HEADS)] * 2
    out_shape += [jax.ShapeDtypeStruct((rows_total, LANES), F32),
                  jax.ShapeDtypeStruct((N_GATE_ROWS, rows_total), F32)]
    out_specs += [row_spec(LANES), pl.BlockSpec((N_GATE_ROWS, tm), lambda i: (0, i))]
    return pl.pallas_call(
        functools.partial(_proj_kernel, kv_transposed=kv_transposed),
        grid=(steps,),
        in_specs=in_specs,
        out_specs=out_specs,
        out_shape=out_shape,
        compiler_params=pltpu.CompilerParams(dimension_semantics=("arbitrary",), vmem_limit_bytes=48 * MIB),
        name="proj",
    )(x2d, g, wt_main, wt_kv, wt_g, bcol, brow)


def _mlstm_kernel(q_ref, k_ref, v_ref, om_ref, gc_ref, gr_ref, hg_ref, c0_ref, n0_ref, m0_ref,
                  hm_ref, c_ref, n_ref, m_ref, *, bb, chunk):
    @pl.when(pl.program_id(1) == 0)
    def _():
        c_ref[...] = c0_ref[...]
        n_ref[...] = n0_ref[...]
        m_ref[...] = m0_ref[...]

    t_idx = lax.broadcasted_iota(jnp.int32, (chunk, chunk), 0)
    s_idx = lax.broadcasted_iota(jnp.int32, (chunk, chunk), 1)
    causal = s_idx <= t_idx
    tril = causal.astype(F32)
    triu = (t_idx <= s_idx).astype(F32)

    def per_batch(b, carry):
        gc = gc_ref[b]
        gr = gr_ref[b]
        bc = jnp.dot(tril, gc, precision=HI, preferred_element_type=F32)
        br = jnp.dot(gr, triu, precision=HI, preferred_element_type=F32)
        q = q_ref[b]
        k = k_ref[b]
        v = v_ref[b]
        om = om_ref[b]
        for h in range(H_M):
            sl = slice(h * DH_M, (h + 1) * DH_M)
            li_c = gc[:, h:h + 1]
            li_r = gr[h:h + 1, :]
            b_c = bc[:, H_M + h:H_M + h + 1]
            b_r = br[H_M + h:H_M + h + 1, :]
            m_prev = m_ref[pl.ds(b, 1), h:h + 1]
            dmat = jnp.where(causal, b_c - b_r + li_r, -jnp.inf)
            inter = m_prev + b_c
            m_t = jnp.maximum(inter, jnp.max(dmat, axis=-1, keepdims=True))
            qh = q[:, sl].astype(BF16)
            kh = k[:, sl].astype(BF16)
            vh = v[:, sl].astype(BF16)
            smat = lax.dot_general(qh, kh, NT, preferred_element_type=F32) * jnp.exp(dmat - m_t)
            w_inter = jnp.exp(inter - m_t)
            c_prev = c_ref[b, h]
            cq = lax.dot_general(qh, c_prev.astype(BF16), NT, preferred_element_type=F32)
            num = w_inter * cq + jnp.dot(smat.astype(BF16), vh, preferred_element_type=F32)
            n_prev = n_ref[b, h:h + 1, :]
            nq = jnp.sum(qh.astype(F32) * n_prev, axis=-1, keepdims=True)
            den = w_inter * nq + jnp.sum(smat, axis=-1, keepdims=True)
            hh = num / jnp.maximum(jnp.abs(den), jnp.exp(-m_t))
            hn = hh * lax.rsqrt(jnp.mean(hh * hh, axis=-1, keepdims=True) + EPS) * hg_ref[:, sl]
            hm_ref[b, :, sl] = (hn * _sigmoid(om[:, sl].astype(F32))).astype(hm_ref.dtype)
            b_last = b_c[chunk - 1:chunk, :]
            m_new = m_t[chunk - 1:chunk, :]
            decay = jnp.exp(m_prev + b_last - m_new)
            w_s = jnp.exp(li_c + b_last - b_c - m_new)
            vw = (vh.astype(F32) * w_s).astype(BF16)
            c_ref[b, h] = decay * c_prev + lax.dot_general(vw, kh, TN, preferred_element_type=F32)
            n_ref[b, h:h + 1, :] = decay * n_prev + jnp.sum(kh.astype(F32) * w_s, axis=0, keepdims=True)
            m_ref[pl.ds(b, 1), h:h + 1] = m_new
        return carry

    lax.fori_loop(0, bb, per_batch, 0)


def _mlstm(qm, km, vm, om, gcol, grow3, head_g, c0, n0, m0, *, batch, seq, chunk, out_dtype):
    bb = MLSTM_BATCH
    n_chunks = seq // chunk
    as3 = lambda a: a.reshape(batch, seq, a.shape[-1])
    tok = lambda w: pl.BlockSpec((bb, chunk, w), lambda g, c: (g, c, 0))
    state = lambda shape: pl.BlockSpec((bb,) + shape, lambda g, c: (g,) + (0,) * len(shape))
    in_specs = [tok(W_HEADS), tok(W_HEADS), tok(W_HEADS), tok(W_HEADS), tok(LANES),
                pl.BlockSpec((bb, N_GATE_ROWS, chunk), lambda g, c: (g, 0, c)),
                _const_spec(head_g.shape),
                state((H_M, DH_M, DH_M)), state((H_M, DH_M)), state((H_M,))]
    out_specs = [tok(W_HEADS), state((H_M, DH_M, DH_M)), state((H_M, DH_M)), state((H_M,))]
    out_shape = [jax.ShapeDtypeStruct((batch, seq, W_HEADS), out_dtype),
                 jax.ShapeDtypeStruct(c0.shape, F32), jax.ShapeDtypeStruct(n0.shape, F32),
                 jax.ShapeDtypeStruct(m0.shape, F32)]
    hm, c_new, n_new, m_new = pl.pallas_call(
        functools.partial(_mlstm_kernel, bb=bb, chunk=chunk),
        grid=(batch // bb, n_chunks),
        in_specs=in_specs,
        out_specs=out_specs,
        out_shape=out_shape,
        compiler_params=pltpu.CompilerParams(dimension_semantics=("arbitrary", "arbitrary"),
                                             vmem_limit_bytes=48 * MIB),
        name="mlstm",
    )(as3(qm), as3(km), as3(vm), as3(om), as3(gcol), grow3, head_g, c0, n0, m0)
    return hm.reshape(batch * seq, W_HEADS), c_new, n_new, m_new


def _fox_cumsum_kernel(gc_ref, gr_ref, cc_ref, cr_ref):
    seq = gc_ref.shape[1]
    t_idx = lax.broadcasted_iota(jnp.int32, (LANES, LANES), 0)
    s_idx = lax.broadcasted_iota(jnp.int32, (LANES, LANES), 1)
    tril = (s_idx <= t_idx).astype(F32)
    triu = (t_idx <= s_idx).astype(F32)
    carry_c = jnp.zeros((1, LANES), F32)
    carry_r = jnp.zeros((N_GATE_ROWS, 1), F32)
    for j in range(seq // LANES):
        blk = slice(j * LANES, (j + 1) * LANES)
        cb = jnp.dot(tril, gc_ref[0, blk, :], precision=HI, preferred_element_type=F32) + carry_c
        cc_ref[0, blk, :] = cb * LOG2E
        carry_c = cb[LANES - 1:LANES, :]
        rb = jnp.dot(gr_ref[:, blk], triu, precision=HI, preferred_element_type=F32) + carry_r
        cr_ref[:, blk] = rb * LOG2E
        carry_r = rb[:, LANES - 1:LANES]


def _fox_cumsum(gcol, grow, *, batch, seq):
    return pl.pallas_call(
        _fox_cumsum_kernel,
        grid=(batch,),
        in_specs=[pl.BlockSpec((1, seq, LANES), lambda b: (b, 0, 0)),
                  pl.BlockSpec((N_GATE_ROWS, seq), lambda b: (0, b))],
        out_specs=[pl.BlockSpec((1, seq, LANES), lambda b: (b, 0, 0)),
                   pl.BlockSpec((N_GATE_ROWS, seq), lambda b: (0, b))],
        out_shape=[jax.ShapeDtypeStruct((batch, seq, LANES), F32),
                   jax.ShapeDtypeStruct((N_GATE_ROWS, batch * seq), F32)],
        compiler_params=pltpu.CompilerParams(dimension_semantics=("arbitrary",)),
        name="fox_cumsum",
    )(gcol.reshape(batch, seq, LANES), grow)


def _fox_prompt_kernel(q_ref, kt_ref, vt_ref, cc_ref, cr_ref, o_ref, qs_ref, cq_ref, m_ref, l_ref, acc_ref, *, blk):
    i = pl.program_id(1)
    n_pairs = H_F // 2
    lane_tiles = blk // LANES
    lane = lax.broadcasted_iota(jnp.int32, (blk, LANES), 1)
    low_half = lane < DH_F
    r_idx = lax.broadcasted_iota(jnp.int32, (blk, blk), 0)
    c_idx = lax.broadcasted_iota(jnp.int32, (blk, blk), 1)
    diag_mask = jnp.concatenate([c_idx <= r_idx] * 2, axis=0)
    gate0 = N_GATE_ROWS - H_F

    for pair in range(n_pairs):
        rows = slice(pair * LANES, (pair + 1) * LANES)
        q_pair = q_ref[:, rows]
        zero = jnp.zeros_like(q_pair)
        qs_ref[pair, 0:blk, :] = jnp.where(low_half, q_pair, zero)
        qs_ref[pair, blk:2 * blk, :] = jnp.where(low_half, zero, q_pair)
        for e in range(2):
            col = gate0 + 2 * pair + e
            cq_ref[pair, e * blk:(e + 1) * blk, :] = jnp.broadcast_to(cc_ref[:, col:col + 1], (blk, LANES))
    m_ref[...] = jnp.full(m_ref.shape, -jnp.inf, F32)
    l_ref[...] = jnp.zeros(l_ref.shape, F32)
    acc_ref[...] = jnp.zeros(acc_ref.shape, F32)

    def step(j, masked):
        cols = pl.ds(pl.multiple_of(j * blk, blk), blk)
        for pair in range(n_pairs):
            rows = slice(pair * LANES, (pair + 1) * LANES)
            kj = kt_ref[0, rows, cols]
            vj = vt_ref[0, rows, cols]
            s = jnp.dot(qs_ref[pair], kj, preferred_element_type=F32)
            ck = jnp.concatenate(
                [jnp.broadcast_to(cr_ref[gate0 + 2 * pair + e:gate0 + 2 * pair + e + 1, cols], (blk, blk))
                 for e in range(2)], axis=0)
            s = s + (jnp.concatenate([cq_ref[pair]] * lane_tiles, axis=1) - ck)
            if masked:
                s = jnp.where(diag_mask, s, -jnp.inf)
            m_old = m_ref[pair]
            m_new = jnp.maximum(m_old, jnp.max(s, axis=-1, keepdims=True))
            p = jnp.exp2(s - jnp.concatenate([m_new] * lane_tiles, axis=1))
            alpha = jnp.exp2(m_old - m_new)
            p_lanes = p[:, 0:LANES]
            for t in range(1, lane_tiles):
                p_lanes = p_lanes + p[:, t * LANES:(t + 1) * LANES]
            l_ref[pair] = alpha * l_ref[pair] + p_lanes
            pv = lax.dot_general(p.astype(BF16), vj, NT, preferred_element_type=F32)
            acc_ref[pair] = alpha * acc_ref[pair] + pv
            m_ref[pair] = m_new

    def loop_body(j, carry):
        step(j, False)
        return carry

    lax.fori_loop(0, i, loop_body, 0)
    step(i, True)
    for pair in range(n_pairs):
        rows = slice(pair * LANES, (pair + 1) * LANES)
        out = acc_ref[pair] / jnp.sum(l_ref[pair], axis=-1, keepdims=True)
        o_ref[:, rows] = jnp.where(low_half, out[0:blk], out[blk:2 * blk]).astype(o_ref.dtype)


def _fox_prompt(qf, ktb, vtb, ccol, crow, *, batch, seq):
    blk = FOX_BLOCK
    nq = seq // blk
    return pl.pallas_call(
        functools.partial(_fox_prompt_kernel, blk=blk),
        grid=(batch, nq),
        in_specs=[pl.BlockSpec((blk, W_HEADS), lambda b, i: (b * nq + i, 0)),
                  pl.BlockSpec((1, W_HEADS, seq), lambda b, i: (b, 0, 0)),
                  pl.BlockSpec((1, W_HEADS, seq), lambda b, i: (b, 0, 0)),
                  pl.BlockSpec((blk, LANES), lambda b, i: (b * nq + i, 0)),
                  pl.BlockSpec((N_GATE_ROWS, seq), lambda b, i: (0, b))],
        out_specs=pl.BlockSpec((blk, W_HEADS), lambda b, i: (b * nq + i, 0)),
        out_shape=jax.ShapeDtypeStruct((batch * seq, W_HEADS), BF16),
        scratch_shapes=[pltpu.VMEM((H_F // 2, 2 * blk, LANES), BF16)] + [pltpu.VMEM((H_F // 2, 2 * blk, LANES), F32)] * 4,
        compiler_params=pltpu.CompilerParams(dimension_semantics=("arbitrary", "arbitrary"),
                                             vmem_limit_bytes=48 * MIB),
        name="fox_prompt",
    )(qf, ktb, vtb, ccol.reshape(batch * seq, LANES), crow)


def _page_bias_kernel(pt_ref, lf_hbm, o_ref, buf, sem):
    b = pl.program_id(0)
    nb = pl.num_programs(0)
    n_pages = buf.shape[1]
    slot = b % 2

    def page_copy(bi, p, sl):
        return pltpu.make_async_copy(lf_hbm.at[pt_ref[bi, p]], buf.at[sl, p], sem.at[sl])

    def start_all(bi, sl):
        def body(p, c):
            page_copy(bi, p, sl).start()
            return c
        lax.fori_loop(0, n_pages, body, 0)

    @pl.when(b == 0)
    def _():
        start_all(0, 0)

    @pl.when(b + 1 < nb)
    def _():
        start_all(b + 1, 1 - slot)

    def wait_body(p, c):
        page_copy(b, p, slot).wait()
        return c
    lax.fori_loop(0, n_pages, wait_body, 0)

    x = buf[slot].reshape(n_pages * H_F, LANES)
    t_idx = lax.broadcasted_iota(jnp.int32, (LANES, LANES), 0)
    s_idx = lax.broadcasted_iota(jnp.int32, (LANES, LANES), 1)
    later = (t_idx > s_idx).astype(F32)
    within = jnp.dot(x, later, precision=HI, preferred_element_type=F32)
    total = jnp.sum(x, axis=-1, keepdims=True)
    run = jnp.zeros((H_F, LANES), F32)
    for p in range(n_pages - 1, -1, -1):
        rows = slice(p * H_F, (p + 1) * H_F)
        o_ref[0, p] = (within[rows] + run) * (-LOG2E)
        run = run + total[rows]


def _page_bias(page_table, lf_pool):
    batch, n_pages = page_table.shape
    return pl.pallas_call(
        _page_bias_kernel,
        grid_spec=pltpu.PrefetchScalarGridSpec(
            num_scalar_prefetch=1,
            grid=(batch,),
            in_specs=[pl.BlockSpec(memory_space=pl.ANY)],
            out_specs=pl.BlockSpec((1, n_pages, H_F, LANES), lambda b, pt: (b, 0, 0, 0)),
            scratch_shapes=[pltpu.VMEM((2, n_pages, H_F, LANES), F32), pltpu.SemaphoreType.DMA((2,))],
        ),
        out_shape=jax.ShapeDtypeStruct((batch, n_pages, H_F, LANES), F32),
        compiler_params=pltpu.CompilerParams(dimension_semantics=("arbitrary",)),
        name="page_bias",
    )(page_table, lf_pool)


def _fox_sample_kernel(pt_ref, q_ref, gr_ref, kn_ref, vn_ref, cp_ref, k_hbm, v_hbm, o_ref,
                       kbuf, vbuf, sem, acc_ref, *, n_groups):
    b = pl.program_id(0)
    nb = pl.num_programs(0)
    group = PAGES_PER_GROUP
    t_new = q_ref.shape[0]
    n_rows = t_new * H_F

    def page_copies(bi, g, sl):
        cps = []
        for j in range(group):
            page = pt_ref[bi, g * group + j]
            cps.append(pltpu.make_async_copy(k_hbm.at[page], kbuf.at[sl, j], sem.at[sl, 0]))
            cps.append(pltpu.make_async_copy(v_hbm.at[page], vbuf.at[sl, j], sem.at[sl, 1]))
        return cps

    def start_group(bi, g, sl):
        for cp in page_copies(bi, g, sl):
            cp.start()

    @pl.when(b == 0)
    def _():
        for a in range(PAGE_SLOTS - 1):
            start_group(0, a, a)

    sub = lax.broadcasted_iota(jnp.int32, (H_F, W_HEADS), 0)
    lane = lax.broadcasted_iota(jnp.int32, (H_F, W_HEADS), 1)
    own_head = (lane // DH_F) == sub
    q = q_ref[...].astype(F32)
    qbd = jnp.concatenate(
        [jnp.where(own_head, jnp.broadcast_to(q[t:t + 1, :], (H_F, W_HEADS)), 0.0) for t in range(t_new)],
        axis=0).astype(BF16)

    gate0 = N_GATE_ROWS - H_F
    lf_new = gr_ref[0, gate0:N_GATE_ROWS, :]
    a_idx = lax.broadcasted_iota(jnp.int32, (t_new, t_new), 0)
    b_idx = lax.broadcasted_iota(jnp.int32, (t_new, t_new), 1)
    c_new = jnp.dot(lf_new, (a_idx <= b_idx).astype(F32), precision=HI, preferred_element_type=F32) * LOG2E
    cq = jnp.concatenate([c_new[:, t:t + 1] for t in range(t_new)], axis=0)

    acc_ref[...] = jnp.zeros_like(acc_ref)

    def body(g, carry):
        m_i, l_i = carry
        slot = g % PAGE_SLOTS
        ahead = g + (PAGE_SLOTS - 1)

        @pl.when(ahead < n_groups)
        def _():
            start_group(b, ahead, ahead % PAGE_SLOTS)

        @pl.when(jnp.logical_and(ahead >= n_groups, b + 1 < nb))
        def _():
            start_group(b + 1, ahead - n_groups, ahead % PAGE_SLOTS)

        for cp in page_copies(b, g, slot):
            cp.wait()

        kcat = jnp.concatenate([kbuf[slot, j].astype(BF16) for j in range(group)], axis=1)
        s = jnp.dot(qbd, kcat, preferred_element_type=F32)
        first = pl.multiple_of(g * group, group)
        cpg = cp_ref[0, pl.ds(first, group)]
        bias = jnp.concatenate(
            [jnp.broadcast_to(cpg[j][None], (t_new, H_F, LANES)).reshape(n_rows, LANES) for j in range(group)],
            axis=1)
        s = s + (cq - bias)
        m_new = jnp.maximum(m_i, jnp.max(s, axis=-1, keepdims=True))
        p = jnp.exp2(s - m_new)
        alpha = jnp.exp2(m_i - m_new)
        l_new = alpha * l_i + jnp.sum(p, axis=-1, keepdims=True)
        vcat = jnp.concatenate([vbuf[slot, j].astype(BF16) for j in range(group)], axis=1)
        pv = lax.dot_general(p.astype(BF16), vcat, NT, preferred_element_type=F32)
        acc_ref[...] = alpha * acc_ref[...] + pv
        return m_new, l_new

    init = (jnp.full((n_rows, 1), -jnp.inf, F32), jnp.zeros((n_rows, 1), F32))
    m_i, l_i = lax.fori_loop(0, n_groups, body, init)

    kn = kn_ref[...].astype(BF16)
    vn = vn_ref[...].astype(BF16)
    s = lax.dot_general(qbd, kn, NT, preferred_element_type=F32)
    ck = jnp.broadcast_to(c_new[None], (t_new, H_F, t_new)).reshape(n_rows, t_new)
    s = s + (cq - ck)
    r_idx = lax.broadcasted_iota(jnp.int32, (n_rows, t_new), 0)
    k_idx = lax.broadcasted_iota(jnp.int32, (n_rows, t_new), 1)
    s = jnp.where(k_idx <= r_idx // H_F, s, -jnp.inf)
    m_new = jnp.maximum(m_i, jnp.max(s, axis=-1, keepdims=True))
    p = jnp.exp2(s - m_new)
    alpha = jnp.exp2(m_i - m_new)
    l_fin = alpha * l_i + jnp.sum(p, axis=-1, keepdims=True)
    acc = alpha * acc_ref[...] + jnp.dot(p.astype(BF16), vn, preferred_element_type=F32)
    out = acc / l_fin
    o_ref[...] = jnp.concatenate(
        [jnp.sum(jnp.where(own_head, out[t * H_F:(t + 1) * H_F, :], 0.0), axis=0, keepdims=True)
         for t in range(t_new)], axis=0).astype(o_ref.dtype)


def _fox_sample(page_table, qf, grow3, k_new, v_new, page_bias, k_pool, v_pool):
    batch, n_pages = page_table.shape
    t_new = qf.shape[0] // batch
    n_groups = n_pages // PAGES_PER_GROUP
    assert n_pages % PAGES_PER_GROUP == 0 and n_groups % PAGE_SLOTS == 0
    page_rows, page_len = k_pool.shape[1], k_pool.shape[2]
    tok = lambda w: pl.BlockSpec((t_new, w), lambda b, pt: (b, 0))
    return pl.pallas_call(
        functools.partial(_fox_sample_kernel, n_groups=n_groups),
        grid_spec=pltpu.PrefetchScalarGridSpec(
            num_scalar_prefetch=1,
            grid=(batch,),
            in_specs=[tok(W_HEADS),
                      pl.BlockSpec((1, N_GATE_ROWS, t_new), lambda b, pt: (b, 0, 0)),
                      tok(W_HEADS), tok(W_HEADS),
                      pl.BlockSpec((1, n_pages, H_F, LANES), lambda b, pt: (b, 0, 0, 0)),
                      pl.BlockSpec(memory_space=pl.ANY), pl.BlockSpec(memory_space=pl.ANY)],
            out_specs=tok(W_HEADS),
            scratch_shapes=[pltpu.VMEM((PAGE_SLOTS, PAGES_PER_GROUP, page_rows, page_len), F32),
                            pltpu.VMEM((PAGE_SLOTS, PAGES_PER_GROUP, page_rows, page_len), F32),
                            pltpu.SemaphoreType.DMA((PAGE_SLOTS, 2)),
                            pltpu.VMEM((t_new * H_F, W_HEADS), F32)],
        ),
        out_shape=jax.ShapeDtypeStruct((batch * t_new, W_HEADS), F32),
        compiler_params=pltpu.CompilerParams(dimension_semantics=("arbitrary",), vmem_limit_bytes=48 * MIB),
        name="fox_sample",
    )(page_table, qf, grow3, k_new, v_new, page_bias, k_pool, v_pool)


def _ffn_kernel(x_ref, hm_ref, hf_ref, wo_ref, g2_ref, wg_ref, wu_ref, wd_ref, g3_ref, y_ref, *, final_norm):
    x1 = (x_ref[...]
          + jnp.dot(hm_ref[...].astype(BF16), wo_ref[0:W_HEADS, :], preferred_element_type=F32)
          + jnp.dot(hf_ref[...].astype(BF16), wo_ref[W_HEADS:2 * W_HEADS, :], preferred_element_type=F32))
    h = _rms(x1, g2_ref[...]).astype(BF16)
    gate = jnp.dot(h, wg_ref[...], preferred_element_type=F32)
    up = jnp.dot(h, wu_ref[...], preferred_element_type=F32)
    act = (gate * _sigmoid(gate) * up).astype(BF16)
    x2 = x1 + jnp.dot(act, wd_ref[...], preferred_element_type=F32)
    y_ref[...] = _rms(x2, g3_ref[...]) if final_norm else x2


def _merge_ffn(x2d, hm, hf, wo, g2, wg, wu, wd, g3, *, final_norm):
    rows_total, d_model = x2d.shape
    tm = min(FFN_ROWS, rows_total)
    row_spec = lambda w: pl.BlockSpec((tm, w), lambda i: (i, 0))
    return pl.pallas_call(
        functools.partial(_ffn_kernel, final_norm=final_norm),
        grid=(rows_total // tm,),
        in_specs=[row_spec(d_model), row_spec(W_HEADS), row_spec(W_HEADS), _const_spec(wo.shape),
                  _const_spec(g2.shape), _const_spec(wg.shape), _const_spec(wu.shape), _const_spec(wd.shape),
                  _const_spec(g3.shape)],
        out_specs=row_spec(d_model),
        out_shape=jax.ShapeDtypeStruct((rows_total, d_model), F32),
        compiler_params=pltpu.CompilerParams(dimension_semantics=("arbitrary",), vmem_limit_bytes=56 * MIB),
        name="merge_ffn",
    )(x2d, hm, hf, wo, g2, wg, wu, wd, g3)


def kernel(x_prompt, x_sample, cache_fox_k, cache_fox_v, cache_fox_logf, page_table, state_mlstm_C,
           state_mlstm_n, state_mlstm_m, norm_mix_g, w_in, b_m_igate, b_m_fgate, b_f_fgate, mlstm_head_g,
           w_out, norm_ffn_g, w_gate, w_up, w_down, norm_final_g):
    depth = w_in.shape[0]
    batch, seq, d_model = x_prompt.shape
    dec_batch, dec_seq, _ = x_sample.shape
    n_pool, page_size = cache_fox_k.shape[1], cache_fox_k.shape[2]
    xp = x_prompt.reshape(batch * seq, d_model)
    xs = x_sample.reshape(dec_batch * dec_seq, d_model)
    g_final = norm_final_g.reshape(1, d_model)
    pk, pv, plf, pc, pn, pm = [], [], [], [], [], []
    sk, sv, slf, sc, sn, sm = [], [], [], [], [], []
    gate0 = N_GATE_ROWS - H_F
    o_gm = 4 * W_HEADS
    o_qf = o_gm + 2 * H_M
    o_gf = o_qf + 3 * W_HEADS
    for l in range(depth):
        wt = jnp.swapaxes(w_in[l], 0, 1)
        wt_main = jnp.concatenate([wt[0:o_gm], wt[o_qf:o_qf + W_HEADS]], axis=0).astype(BF16)
        wt_kv = wt[o_qf + W_HEADS:o_gf].astype(BF16)
        wt_g = jnp.concatenate([wt[o_gm:o_qf], wt[o_gf:o_gf + H_F],
                                jnp.zeros((LANES - N_GATE_ROWS, d_model), F32)], axis=0).astype(BF16)
        bias = jnp.concatenate([b_m_igate[l], b_m_fgate[l], b_f_fgate[l],
                                jnp.zeros((LANES - N_GATE_ROWS,), F32)]).astype(F32)
        bcol = bias.reshape(1, LANES)
        brow = bias[:N_GATE_ROWS].reshape(N_GATE_ROWS, 1)
        g_mix = norm_mix_g[l].reshape(1, d_model)
        g_ffn = norm_ffn_g[l].reshape(1, d_model)
        head_g = mlstm_head_g[l].reshape(1, W_HEADS)
        wo = w_out[l].astype(BF16)
        wg = w_gate[l].astype(BF16)
        wu = w_up[l].astype(BF16)
        wd = w_down[l].astype(BF16)

        qm, km, vm, om, qf, kt, vt, ktb, vtb, gcol, grow = _project(
            xp, g_mix, wt_main, wt_kv, wt_g, bcol, brow, batch=batch, seq=seq, kv_transposed=True, act_dtype=BF16)
        chunk = min(LANES, seq)
        grow3 = grow.reshape(N_GATE_ROWS, batch, seq).transpose(1, 0, 2)
        hm, c_p, n_p, m_p = _mlstm(
            qm, km, vm, om, gcol, grow3, head_g,
            jnp.zeros((batch, H_M, DH_M, DH_M), F32), jnp.zeros((batch, H_M, DH_M), F32),
            jnp.zeros((batch, H_M), F32), batch=batch, seq=seq, chunk=chunk, out_dtype=BF16)
        ccol, crow = _fox_cumsum(gcol, grow, batch=batch, seq=seq)
        hf = _fox_prompt(qf, ktb, vtb, ccol, crow, batch=batch, seq=seq)
        xp = _merge_ffn(xp, hm, hf, wo, g_ffn, wg, wu, wd, g_final, final_norm=(l == depth - 1))
        pk.append(kt.reshape(batch, H_F, DH_F, seq).transpose(0, 3, 1, 2))
        pv.append(vt.reshape(batch, H_F, DH_F, seq).transpose(0, 3, 1, 2))
        plf.append(grow3[:, gate0:, :].transpose(0, 2, 1))
        pc.append(c_p); pn.append(n_p); pm.append(m_p)

        qm, km, vm, om, qf, k_new, v_new, gcol, grow = _project(
            xs, g_mix, wt_main, wt_kv, wt_g, bcol, brow, batch=dec_batch, seq=dec_seq, kv_transposed=False,
            act_dtype=F32)
        grow3 = grow.reshape(N_GATE_ROWS, dec_batch, dec_seq).transpose(1, 0, 2)
        hm, c_s, n_s, m_s = _mlstm(
            qm, km, vm, om, gcol, grow3, head_g,
            state_mlstm_C[l].astype(F32), state_mlstm_n[l].astype(F32), state_mlstm_m[l].astype(F32),
            batch=dec_batch, seq=dec_seq, chunk=dec_seq, out_dtype=F32)
        k_pool = cache_fox_k[l].transpose(0, 2, 3, 1).reshape(n_pool, W_HEADS, page_size)
        v_pool = cache_fox_v[l].transpose(0, 2, 3, 1).reshape(n_pool, W_HEADS, page_size)
        lf_pool = cache_fox_logf[l].transpose(0, 2, 1)
        page_bias = _page_bias(page_table, lf_pool)
        hf = _fox_sample(page_table, qf, grow3, k_new, v_new, page_bias, k_pool, v_pool)
        xs = _merge_ffn(xs, hm, hf, wo, g_ffn, wg, wu, wd, g_final, final_norm=(l == depth - 1))
        sk.append(k_new.reshape(dec_batch, dec_seq, H_F, DH_F))
        sv.append(v_new.reshape(dec_batch, dec_seq, H_F, DH_F))
        slf.append(grow3[:, gate0:, :].transpose(0, 2, 1))
        sc.append(c_s); sn.append(n_s); sm.append(m_s)

    st = lambda a, ref: jnp.stack(a, axis=0).astype(ref.dtype)
    return (xp.reshape(batch, seq, d_model), xs.reshape(dec_batch, dec_seq, d_model),
            st(pk, cache_fox_k), st(pv, cache_fox_v), st(plf, cache_fox_logf),
            st(pc, state_mlstm_C), st(pn, state_mlstm_n), st(pm, state_mlstm_m),
            st(sk, cache_fox_k), st(sv, cache_fox_v), st(slf, cache_fox_logf),
            st(sc, state_mlstm_C), st(sn, state_mlstm_n), st(sm, state_mlstm_m))
```

```python
import functools

import jax
import jax.numpy as jnp
from jax import lax
from jax.experimental import pallas as pl
from jax.experimental.pallas import tpu as pltpu

F32 = jnp.float32
BF16 = jnp.bfloat16
HI = lax.Precision.HIGHEST
NT = (((1,), (1,)), ((), ()))
TN = (((0,), (0,)), ((), ()))

EPS = 1e-6
LOG2E = 1.4426950408889634
H_M = 4
DH_M = 128
H_F = 8
DH_F = 64
W_HEADS = 512
N_GATE_ROWS = 16
LANES = 128
MIB = 1024 * 1024

PROJ_ROWS = 512
FFN_ROWS = 512
FFN_CHUNK = 256
MLSTM_BATCH = 8
FOX_BLOCK = 256
PAGES_PER_GROUP = 8
PAGE_SLOTS = 4


def _rms(x, g):
    return x * lax.rsqrt(jnp.mean(x * x, axis=-1, keepdims=True) + EPS) * g


def _log_sigmoid(x):
    return jnp.minimum(x, 0.0) - jnp.log1p(jnp.exp(-jnp.abs(x)))


def _sigmoid(x):
    return 1.0 / (1.0 + jnp.exp(-x))


def _const_spec(shape):
    return pl.BlockSpec(shape, lambda *_: (0,) * len(shape), pipeline_mode=pl.Buffered(1))


def _proj_kernel(x_ref, g_ref, wt_ref, wkv_ref, wg_ref, bcol_ref, brow_ref, *outs, kv_transposed):
    if kv_transposed:
        qm_ref, km_ref, vm_ref, om_ref, qf_ref, kf_ref, vf_ref, kfb_ref, vfb_ref, gcol_ref, grow_ref = outs
    else:
        qm_ref, km_ref, vm_ref, om_ref, qf_ref, kf_ref, vf_ref, gcol_ref, grow_ref = outs
    h = _rms(x_ref[...], g_ref[...]).astype(BF16)
    rows = h.shape[0]

    def mm(i):
        w = wt_ref[i * W_HEADS:(i + 1) * W_HEADS, :]
        return lax.dot_general(h, w, NT, preferred_element_type=F32)

    qm_ref[...] = mm(0).astype(qm_ref.dtype)
    km_ref[...] = (mm(1) * (DH_M ** -0.5)).astype(km_ref.dtype)
    vm_ref[...] = mm(2).astype(vm_ref.dtype)
    om_ref[...] = mm(3).astype(om_ref.dtype)
    qf_ref[...] = (mm(4) * (DH_F ** -0.5 * LOG2E)).astype(qf_ref.dtype)
    if kv_transposed:
        kt = lax.dot_general(wkv_ref[0:W_HEADS, :], h, NT, preferred_element_type=F32)
        kf_ref[0] = kt
        kfb_ref[0] = kt.astype(BF16)
        vt = lax.dot_general(wkv_ref[W_HEADS:2 * W_HEADS, :], h, NT, preferred_element_type=F32)
        vf_ref[0] = vt
        vfb_ref[0] = vt.astype(BF16)
    else:
        kf_ref[...] = lax.dot_general(h, wkv_ref[0:W_HEADS, :], NT, preferred_element_type=F32)
        vf_ref[...] = lax.dot_general(h, wkv_ref[W_HEADS:2 * W_HEADS, :], NT, preferred_element_type=F32)
    pre_c = lax.dot_general(h, wg_ref[...], NT, preferred_element_type=F32) + bcol_ref[...]
    lane = lax.broadcasted_iota(jnp.int32, (rows, LANES), 1)
    gcol_ref[...] = jnp.where(lane < H_M, pre_c, _log_sigmoid(pre_c))
    pre_r = lax.dot_general(wg_ref[0:N_GATE_ROWS, :], h, NT, preferred_element_type=F32) + brow_ref[...]
    row = lax.broadcasted_iota(jnp.int32, (N_GATE_ROWS, rows), 0)
    grow_ref[...] = jnp.where(row < H_M, pre_r, _log_sigmoid(pre_r))


def _project(x2d, g, wt_main, wt_kv, wt_g, bcol, brow, *, batch, seq, kv_transposed, act_dtype):
    rows_total, d_model = x2d.shape
    tm = min(PROJ_ROWS, rows_total)
    steps = rows_total // tm
    per_seq = max(seq // tm, 1)
    row_spec = lambda w: pl.BlockSpec((tm, w), lambda i: (i, 0))
    in_specs = [row_spec(d_model), _const_spec(g.shape), _const_spec(wt_main.shape), _const_spec(wt_kv.shape),
                _const_spec(wt_g.shape), _const_spec(bcol.shape), _const_spec(brow.shape)]
    act = jax.ShapeDtypeStruct((rows_total, W_HEADS), act_dtype)
    act32 = jax.ShapeDtypeStruct((rows_total, W_HEADS), F32)
    out_shape = [act, act, act, act32, act]
    out_specs = [row_spec(W_HEADS)] * 5
    if kv_transposed:
        kv_spec = pl.BlockSpec((1, W_HEADS, tm), lambda i: (i // per_seq, 0, i % per_seq))
        out_shape += [jax.ShapeDtypeStruct((batch, W_HEADS, seq), F32)] * 2
        out_shape += [jax.ShapeDtypeStruct((batch, W_HEADS, seq), BF16)] * 2
        out_specs += [kv_spec] * 4
    else:
        out_shape += [act32, act32]
        out_specs += [row_spec(W_HEADS)] * 2
    out_shape += [jax.ShapeDtypeStruct((rows_total, LANES), F32),
                  jax.ShapeDtypeStruct((N_GATE_ROWS, rows_total), F32)]
    out_specs += [row_spec(LANES), pl.BlockSpec((N_GATE_ROWS, tm), lambda i: (0, i))]
    return pl.pallas_call(
        functools.partial(_proj_kernel, kv_transposed=kv_transposed),
        grid=(steps,),
        in_specs=in_specs,
        out_specs=out_specs,
        out_shape=out_shape,
        compiler_params=pltpu.CompilerParams(dimension_semantics=("arbitrary",), vmem_limit_bytes=48 * MIB),
        name="proj",
    )(x2d, g, wt_main, wt_kv, wt_g, bcol, brow)


def _mlstm_kernel(q_ref, k_ref, v_ref, om_ref, gc_ref, gr_ref, hg_ref, c0_ref, n0_ref, m0_ref,
                  hm_ref, c_ref, n_ref, m_ref, *, bb, chunk):
    @pl.when(pl.program_id(1) == 0)
    def _():
        c_ref[...] = c0_ref[...]
        n_ref[...] = n0_ref[...]
        m_ref[...] = m0_ref[...]

    t_idx = lax.broadcasted_iota(jnp.int32, (chunk, chunk), 0)
    s_idx = lax.broadcasted_iota(jnp.int32, (chunk, chunk), 1)
    causal = s_idx <= t_idx
    tril = causal.astype(F32)
    triu = (t_idx <= s_idx).astype(F32)

    def per_batch(b, carry):
        gc = gc_ref[b]
        gr = gr_ref[b]
        bc = jnp.dot(tril, gc, precision=HI, preferred_element_type=F32)
        br = jnp.dot(gr, triu, precision=HI, preferred_element_type=F32)
        q = q_ref[b]
        k = k_ref[b]
        v = v_ref[b]
        om = om_ref[b]
        for h in range(H_M):
            sl = slice(h * DH_M, (h + 1) * DH_M)
            li_c = gc[:, h:h + 1]
            li_r = gr[h:h + 1, :]
            b_c = bc[:, H_M + h:H_M + h + 1]
            b_r = br[H_M + h:H_M + h + 1, :]
            m_prev = m_ref[pl.ds(b, 1), h:h + 1]
            dmat = jnp.where(causal, b_c - b_r + li_r, -jnp.inf)
            inter = m_prev + b_c
            m_t = jnp.maximum(inter, jnp.max(dmat, axis=-1, keepdims=True))
            qh = q[:, sl].astype(BF16)
            kh = k[:, sl].astype(BF16)
            vh = v[:, sl].astype(BF16)
            smat = lax.dot_general(qh, kh, NT, preferred_element_type=F32) * jnp.exp(dmat - m_t)
            w_inter = jnp.exp(inter - m_t)
            c_prev = c_ref[b, h]
            cq = lax.dot_general(qh, c_prev.astype(BF16), NT, preferred_element_type=F32)
            num = w_inter * cq + jnp.dot(smat.astype(BF16), vh, preferred_element_type=F32)
            n_prev = n_ref[b, h:h + 1, :]
            nq = jnp.sum(qh.astype(F32) * n_prev, axis=-1, keepdims=True)
            den = w_inter * nq + jnp.sum(smat, axis=-1, keepdims=True)
            hh = num / jnp.maximum(jnp.abs(den), jnp.exp(-m_t))
            hn = hh * lax.rsqrt(jnp.mean(hh * hh, axis=-1, keepdims=True) + EPS) * hg_ref[:, sl]
            hm_ref[b, :, sl] = (hn * _sigmoid(om[:, sl].astype(F32))).astype(hm_ref.dtype)
            b_last = b_c[chunk - 1:chunk, :]
            m_new = m_t[chunk - 1:chunk, :]
            decay = jnp.exp(m_prev + b_last - m_new)
            w_s = jnp.exp(li_c + b_last - b_c - m_new)
            vw = (vh.astype(F32) * w_s).astype(BF16)
            c_ref[b, h] = decay * c_prev + lax.dot_general(vw, kh, TN, preferred_element_type=F32)
            n_ref[b, h:h + 1, :] = decay * n_prev + jnp.sum(kh.astype(F32) * w_s, axis=0, keepdims=True)
            m_ref[pl.ds(b, 1), h:h + 1] = m_new
        return carry

    lax.fori_loop(0, bb, per_batch, 0)


def _mlstm(qm, km, vm, om, gcol, grow3, head_g, c0, n0, m0, *, batch, seq, chunk, out_dtype):
    bb = MLSTM_BATCH
    n_chunks = seq // chunk
    as3 = lambda a: a.reshape(batch, seq, a.shape[-1])
    tok = lambda w: pl.BlockSpec((bb, chunk, w), lambda g, c: (g, c, 0))
    state = lambda shape: pl.BlockSpec((bb,) + shape, lambda g, c: (g,) + (0,) * len(shape))
    in_specs = [tok(W_HEADS), tok(W_HEADS), tok(W_HEADS), tok(W_HEADS), tok(LANES),
                pl.BlockSpec((bb, N_GATE_ROWS, chunk), lambda g, c: (g, 0, c)),
                _const_spec(head_g.shape),
                state((H_M, DH_M, DH_M)), state((H_M, DH_M)), state((H_M,))]
    out_specs = [tok(W_HEADS), state((H_M, DH_M, DH_M)), state((H_M, DH_M)), state((H_M,))]
    out_shape = [jax.ShapeDtypeStruct((batch, seq, W_HEADS), out_dtype),
                 jax.ShapeDtypeStruct(c0.shape, F32), jax.ShapeDtypeStruct(n0.shape, F32),
                 jax.ShapeDtypeStruct(m0.shape, F32)]
    hm, c_new, n_new, m_new = pl.pallas_call(
        functools.partial(_mlstm_kernel, bb=bb, chunk=chunk),
        grid=(batch // bb, n_chunks),
        in_specs=in_specs,
        out_specs=out_specs,
        out_shape=out_shape,
        compiler_params=pltpu.CompilerParams(dimension_semantics=("arbitrary", "arbitrary"),
                                             vmem_limit_bytes=48 * MIB),
        name="mlstm",
    )(as3(qm), as3(km), as3(vm), as3(om), as3(gcol), grow3, head_g, c0, n0, m0)
    return hm.reshape(batch * seq, W_HEADS), c_new, n_new, m_new


def _fox_cumsum_kernel(gc_ref, gr_ref, cc_ref, cr_ref):
    seq = gc_ref.shape[1]
    t_idx = lax.broadcasted_iota(jnp.int32, (LANES, LANES), 0)
    s_idx = lax.broadcasted_iota(jnp.int32, (LANES, LANES), 1)
    tril = (s_idx <= t_idx).astype(F32)
    triu = (t_idx <= s_idx).astype(F32)
    carry_c = jnp.zeros((1, LANES), F32)
    carry_r = jnp.zeros((N_GATE_ROWS, 1), F32)
    for j in range(seq // LANES):
        blk = slice(j * LANES, (j + 1) * LANES)
        cb = jnp.dot(tril, gc_ref[0, blk, :], precision=HI, preferred_element_type=F32) + carry_c
        cc_ref[0, blk, :] = cb * LOG2E
        carry_c = cb[LANES - 1:LANES, :]
        rb = jnp.dot(gr_ref[:, blk], triu, precision=HI, preferred_element_type=F32) + carry_r
        cr_ref[:, blk] = rb * LOG2E
        carry_r = rb[:, LANES - 1:LANES]


def _fox_cumsum(gcol, grow, *, batch, seq):
    return pl.pallas_call(
        _fox_cumsum_kernel,
        grid=(batch,),
        in_specs=[pl.BlockSpec((1, seq, LANES), lambda b: (b, 0, 0)),
                  pl.BlockSpec((N_GATE_ROWS, seq), lambda b: (0, b))],
        out_specs=[pl.BlockSpec((1, seq, LANES), lambda b: (b, 0, 0)),
                   pl.BlockSpec((N_GATE_ROWS, seq), lambda b: (0, b))],
        out_shape=[jax.ShapeDtypeStruct((batch, seq, LANES), F32),
                   jax.ShapeDtypeStruct((N_GATE_ROWS, batch * seq), F32)],
        compiler_params=pltpu.CompilerParams(dimension_semantics=("arbitrary",)),
        name="fox_cumsum",
    )(gcol.reshape(batch, seq, LANES), grow)


def _fox_prompt_kernel(q_ref, kt_ref, vt_ref, cc_ref, cr_ref, o_ref, qs_ref, cq_ref, m_ref, l_ref, acc_ref, *, blk):
    i = pl.program_id(1)
    n_pairs = H_F // 2
    lane_tiles = blk // LANES
    lane = lax.broadcasted_iota(jnp.int32, (blk, LANES), 1)
    low_half = lane < DH_F
    r_idx = lax.broadcasted_iota(jnp.int32, (blk, blk), 0)
    c_idx = lax.broadcasted_iota(jnp.int32, (blk, blk), 1)
    diag_mask = jnp.concatenate([c_idx <= r_idx] * 2, axis=0)
    gate0 = N_GATE_ROWS - H_F

    for pair in range(n_pairs):
        rows = slice(pair * LANES, (pair + 1) * LANES)
        q_pair = q_ref[:, rows]
        zero = jnp.zeros_like(q_pair)
        qs_ref[pair, 0:blk, :] = jnp.where(low_half, q_pair, zero)
        qs_ref[pair, blk:2 * blk, :] = jnp.where(low_half, zero, q_pair)
        for e in range(2):
            col = gate0 + 2 * pair + e
            cq_ref[pair, e * blk:(e + 1) * blk, :] = jnp.broadcast_to(cc_ref[:, col:col + 1], (blk, LANES))
    m_ref[...] = jnp.full(m_ref.shape, -jnp.inf, F32)
    l_ref[...] = jnp.zeros(l_ref.shape, F32)
    acc_ref[...] = jnp.zeros(acc_ref.shape, F32)

    def step(j, masked):
        cols = pl.ds(pl.multiple_of(j * blk, blk), blk)
        for pair in range(n_pairs):
            rows = slice(pair * LANES, (pair + 1) * LANES)
            kj = kt_ref[0, rows, cols]
            vj = vt_ref[0, rows, cols]
            s = jnp.dot(qs_ref[pair], kj, preferred_element_type=F32)
            ck = jnp.concatenate(
                [jnp.broadcast_to(cr_ref[gate0 + 2 * pair + e:gate0 + 2 * pair + e + 1, cols], (blk, blk))
                 for e in range(2)], axis=0)
            s = s + (jnp.concatenate([cq_ref[pair]] * lane_tiles, axis=1) - ck)
            if masked:
                s = jnp.where(diag_mask, s, -jnp.inf)
            m_old = m_ref[pair]
            m_new = jnp.maximum(m_old, jnp.max(s, axis=-1, keepdims=True))
            p = jnp.exp2(s - jnp.concatenate([m_new] * lane_tiles, axis=1))
            alpha = jnp.exp2(m_old - m_new)
            p_lanes = p[:, 0:LANES]
            for t in range(1, lane_tiles):
                p_lanes = p_lanes + p[:, t * LANES:(t + 1) * LANES]
            l_ref[pair] = alpha * l_ref[pair] + p_lanes
            pv = lax.dot_general(p.astype(BF16), vj, NT, preferred_element_type=F32)
            acc_ref[pair] = alpha * acc_ref[pair] + pv
            m_ref[pair] = m_new

    def loop_body(j, carry):
        step(j, False)
        return carry

    lax.fori_loop(0, i, loop_body, 0)
    step(i, True)
    for pair in range(n_pairs):
        rows = slice(pair * LANES, (pair + 1) * LANES)
        out = acc_ref[pair] / jnp.sum(l_ref[pair], axis=-1, keepdims=True)
        o_ref[:, rows] = jnp.where(low_half, out[0:blk], out[blk:2 * blk]).astype(o_ref.dtype)


def _fox_prompt(qf, ktb, vtb, ccol, crow, *, batch, seq):
    blk = FOX_BLOCK
    nq = seq // blk
    return pl.pallas_call(
        functools.partial(_fox_prompt_kernel, blk=blk),
        grid=(batch, nq),
        in_specs=[pl.BlockSpec((blk, W_HEADS), lambda b, i: (b * nq + i, 0)),
                  pl.BlockSpec((1, W_HEADS, seq), lambda b, i: (b, 0, 0)),
                  pl.BlockSpec((1, W_HEADS, seq), lambda b, i: (b, 0, 0)),
                  pl.BlockSpec((blk, LANES), lambda b, i: (b * nq + i, 0)),
                  pl.BlockSpec((N_GATE_ROWS, seq), lambda b, i: (0, b))],
        out_specs=pl.BlockSpec((blk, W_HEADS), lambda b, i: (b * nq + i, 0)),
        out_shape=jax.ShapeDtypeStruct((batch * seq, W_HEADS), BF16),
        scratch_shapes=[pltpu.VMEM((H_F // 2, 2 * blk, LANES), BF16)] + [pltpu.VMEM((H_F // 2, 2 * blk, LANES), F32)] * 4,
        compiler_params=pltpu.CompilerParams(dimension_semantics=("arbitrary", "arbitrary"),
                                             vmem_limit_bytes=48 * MIB),
        name="fox_prompt",
    )(qf, ktb, vtb, ccol.reshape(batch * seq, LANES), crow)


def _page_bias_kernel(pt_ref, lf_hbm, o_ref, buf, sem):
    b = pl.program_id(0)
    nb = pl.num_programs(0)
    n_pages = buf.shape[1]
    slot = b % 2

    def page_copy(bi, p, sl):
        return pltpu.make_async_copy(lf_hbm.at[pt_ref[bi, p]], buf.at[sl, p], sem.at[sl])

    def start_all(bi, sl):
        def body(p, c):
            page_copy(bi, p, sl).start()
            return c
        lax.fori_loop(0, n_pages, body, 0)

    @pl.when(b == 0)
    def _():
        start_all(0, 0)

    @pl.when(b + 1 < nb)
    def _():
        start_all(b + 1, 1 - slot)

    def wait_body(p, c):
        page_copy(b, p, slot).wait()
        return c
    lax.fori_loop(0, n_pages, wait_body, 0)

    x = buf[slot].reshape(n_pages * H_F, LANES)
    t_idx = lax.broadcasted_iota(jnp.int32, (LANES, LANES), 0)
    s_idx = lax.broadcasted_iota(jnp.int32, (LANES, LANES), 1)
    later = (t_idx > s_idx).astype(F32)
    within = jnp.dot(x, later, precision=HI, preferred_element_type=F32)
    total = jnp.sum(x, axis=-1, keepdims=True)
    run = jnp.zeros((H_F, LANES), F32)
    for p in range(n_pages - 1, -1, -1):
        rows = slice(p * H_F, (p + 1) * H_F)
        o_ref[0, p] = (within[rows] + run) * (-LOG2E)
        run = run + total[rows]


def _page_bias(page_table, lf_pool):
    batch, n_pages = page_table.shape
    return pl.pallas_call(
        _page_bias_kernel,
        grid_spec=pltpu.PrefetchScalarGridSpec(
            num_scalar_prefetch=1,
            grid=(batch,),
            in_specs=[pl.BlockSpec(memory_space=pl.ANY)],
            out_specs=pl.BlockSpec((1, n_pages, H_F, LANES), lambda b, pt: (b, 0, 0, 0)),
            scratch_shapes=[pltpu.VMEM((2, n_pages, H_F, LANES), F32), pltpu.SemaphoreType.DMA((2,))],
        ),
        out_shape=jax.ShapeDtypeStruct((batch, n_pages, H_F, LANES), F32),
        compiler_params=pltpu.CompilerParams(dimension_semantics=("arbitrary",)),
        name="page_bias",
    )(page_table, lf_pool)


def _paged_attention(pt_ref, q_ref, gr_ref, kn_ref, vn_ref, cp_ref, k_hbm, v_hbm, o_ref,
                     kbuf, vbuf, sem, acc_ref, *, n_groups, side_work, n_side):
    b = pl.program_id(0)
    nb = pl.num_programs(0)
    group = PAGES_PER_GROUP
    t_new = q_ref.shape[0]
    n_rows = t_new * H_F

    def page_copies(bi, g, sl):
        cps = []
        for j in range(group):
            page = pt_ref[bi, g * group + j]
            cps.append(pltpu.make_async_copy(k_hbm.at[page], kbuf.at[sl, j], sem.at[sl, 0]))
            cps.append(pltpu.make_async_copy(v_hbm.at[page], vbuf.at[sl, j], sem.at[sl, 1]))
        return cps

    def start_group(bi, g, sl):
        for cp in page_copies(bi, g, sl):
            cp.start()

    @pl.when(b == 0)
    def _():
        for a in range(PAGE_SLOTS - 1):
            start_group(0, a, a)

    sub = lax.broadcasted_iota(jnp.int32, (H_F, W_HEADS), 0)
    lane = lax.broadcasted_iota(jnp.int32, (H_F, W_HEADS), 1)
    own_head = (lane // DH_F) == sub
    q = q_ref[...].astype(F32)
    qbd = jnp.concatenate(
        [jnp.where(own_head, jnp.broadcast_to(q[t:t + 1, :], (H_F, W_HEADS)), 0.0) for t in range(t_new)],
        axis=0).astype(BF16)

    gate0 = N_GATE_ROWS - H_F
    lf_new = gr_ref[0, gate0:N_GATE_ROWS, :]
    a_idx = lax.broadcasted_iota(jnp.int32, (t_new, t_new), 0)
    b_idx = lax.broadcasted_iota(jnp.int32, (t_new, t_new), 1)
    c_new = jnp.dot(lf_new, (a_idx <= b_idx).astype(F32), precision=HI, preferred_element_type=F32) * LOG2E
    cq = jnp.concatenate([c_new[:, t:t + 1] for t in range(t_new)], axis=0)

    acc_ref[...] = jnp.zeros_like(acc_ref)

    def body(g, carry, with_side):
        m_i, l_i = carry
        slot = g % PAGE_SLOTS
        ahead = g + (PAGE_SLOTS - 1)

        @pl.when(ahead < n_groups)
        def _():
            start_group(b, ahead, ahead % PAGE_SLOTS)

        @pl.when(jnp.logical_and(ahead >= n_groups, b + 1 < nb))
        def _():
            start_group(b + 1, ahead - n_groups, ahead % PAGE_SLOTS)

        for cp in page_copies(b, g, slot):
            cp.wait()

        kcat = jnp.concatenate([kbuf[slot, j].astype(BF16) for j in range(group)], axis=1)
        s = jnp.dot(qbd, kcat, preferred_element_type=F32)
        first = pl.multiple_of(g * group, group)
        cpg = cp_ref[0, pl.ds(first, group)]
        bias = jnp.concatenate(
            [jnp.broadcast_to(cpg[j][None], (t_new, H_F, LANES)).reshape(n_rows, LANES) for j in range(group)],
            axis=1)
        s = s + (cq - bias)
        m_new = jnp.maximum(m_i, jnp.max(s, axis=-1, keepdims=True))
        p = jnp.exp2(s - m_new)
        alpha = jnp.exp2(m_i - m_new)
        l_new = alpha * l_i + jnp.sum(p, axis=-1, keepdims=True)
        vcat = jnp.concatenate([vbuf[slot, j].astype(BF16) for j in range(group)], axis=1)
        pv = lax.dot_general(p.astype(BF16), vcat, NT, preferred_element_type=F32)
        acc_ref[...] = alpha * acc_ref[...] + pv
        if with_side:
            side_work(g)
        return m_new, l_new

    init = (jnp.full((n_rows, 1), -jnp.inf, F32), jnp.zeros((n_rows, 1), F32))
    carry = lax.fori_loop(0, n_side, functools.partial(body, with_side=True), init)
    m_i, l_i = lax.fori_loop(n_side, n_groups, functools.partial(body, with_side=False), carry)

    kn = kn_ref[...].astype(BF16)
    vn = vn_ref[...].astype(BF16)
    s = lax.dot_general(qbd, kn, NT, preferred_element_type=F32)
    ck = jnp.broadcast_to(c_new[None], (t_new, H_F, t_new)).reshape(n_rows, t_new)
    s = s + (cq - ck)
    r_idx = lax.broadcasted_iota(jnp.int32, (n_rows, t_new), 0)
    k_idx = lax.broadcasted_iota(jnp.int32, (n_rows, t_new), 1)
    s = jnp.where(k_idx <= r_idx // H_F, s, -jnp.inf)
    m_new = jnp.maximum(m_i, jnp.max(s, axis=-1, keepdims=True))
    p = jnp.exp2(s - m_new)
    alpha = jnp.exp2(m_i - m_new)
    l_fin = alpha * l_i + jnp.sum(p, axis=-1, keepdims=True)
    acc = alpha * acc_ref[...] + jnp.dot(p.astype(BF16), vn, preferred_element_type=F32)
    out = acc / l_fin
    o_ref[...] = jnp.concatenate(
        [jnp.sum(jnp.where(own_head, out[t * H_F:(t + 1) * H_F, :], 0.0), axis=0, keepdims=True)
         for t in range(t_new)], axis=0).astype(o_ref.dtype)


def _merge_residual(x_ref, hm_ref, hf_ref, wo_ref, g2_ref):
    x1 = (x_ref[...]
          + jnp.dot(hm_ref[...].astype(BF16), wo_ref[0:W_HEADS, :], preferred_element_type=F32)
          + jnp.dot(hf_ref[...].astype(BF16), wo_ref[W_HEADS:2 * W_HEADS, :], preferred_element_type=F32))
    return x1, _rms(x1, g2_ref[...]).astype(BF16)


def _ffn_sample_kernel(pt_ref, x_ref, hm_ref, hf_ref, wo_ref, g2_ref, wg_ref, wu_ref, wd_ref, g3_ref,
                       q_ref, gr_ref, kn_ref, vn_ref, cp_ref, k_hbm, v_hbm, y_ref, o_ref,
                       h_ref, x2_ref, kbuf, vbuf, sem, acc_ref, *, n_groups, final_norm):
    x1, h = _merge_residual(x_ref, hm_ref, hf_ref, wo_ref, g2_ref)
    h_ref[...] = h
    x2_ref[...] = x1

    def ffn_chunk(c):
        cols = pl.ds(pl.multiple_of(c * FFN_CHUNK, FFN_CHUNK), FFN_CHUNK)
        hh = h_ref[...]
        gate = jnp.dot(hh, wg_ref[:, cols], preferred_element_type=F32)
        up = jnp.dot(hh, wu_ref[:, cols], preferred_element_type=F32)
        act = (gate * _sigmoid(gate) * up).astype(BF16)
        x2_ref[...] += jnp.dot(act, wd_ref[cols, :], preferred_element_type=F32)

    _paged_attention(pt_ref, q_ref, gr_ref, kn_ref, vn_ref, cp_ref, k_hbm, v_hbm, o_ref, kbuf, vbuf, sem, acc_ref,
                     n_groups=n_groups, side_work=ffn_chunk, n_side=wg_ref.shape[1] // FFN_CHUNK)
    x2 = x2_ref[...]
    y_ref[...] = _rms(x2, g3_ref[...]) if final_norm else x2


def _merge_ffn_and_fox_sample(x2d, hm, hf, wo, g2, wg, wu, wd, g3, page_table, qf, grow3, k_new, v_new, page_bias,
                              k_pool, v_pool, *, final_norm):
    rows_total, d_model = x2d.shape
    batch, n_pages = page_table.shape
    t_new = qf.shape[0] // batch
    n_groups = n_pages // PAGES_PER_GROUP
    tm = rows_total // batch
    d_ff = wg.shape[1]
    assert n_pages % PAGES_PER_GROUP == 0 and n_groups % PAGE_SLOTS == 0
    assert rows_total % batch == 0 and tm % 16 == 0 and d_ff % FFN_CHUNK == 0 and d_ff // FFN_CHUNK <= n_groups
    page_rows, page_len = k_pool.shape[1], k_pool.shape[2]
    row = lambda w: pl.BlockSpec((tm, w), lambda b, pt: (b, 0))
    tok = lambda w: pl.BlockSpec((t_new, w), lambda b, pt: (b, 0))
    return pl.pallas_call(
        functools.partial(_ffn_sample_kernel, n_groups=n_groups, final_norm=final_norm),
        grid_spec=pltpu.PrefetchScalarGridSpec(
            num_scalar_prefetch=1,
            grid=(batch,),
            in_specs=[row(d_model), row(W_HEADS), row(W_HEADS), _const_spec(wo.shape), _const_spec(g2.shape),
                      _const_spec(wg.shape), _const_spec(wu.shape), _const_spec(wd.shape), _const_spec(g3.shape),
                      tok(W_HEADS),
                      pl.BlockSpec((1, N_GATE_ROWS, t_new), lambda b, pt: (b, 0, 0)),
                      tok(W_HEADS), tok(W_HEADS),
                      pl.BlockSpec((1, n_pages, H_F, LANES), lambda b, pt: (b, 0, 0, 0)),
                      pl.BlockSpec(memory_space=pl.ANY), pl.BlockSpec(memory_space=pl.ANY)],
            out_specs=[row(d_model), tok(W_HEADS)],
            scratch_shapes=[pltpu.VMEM((tm, d_model), BF16),
                            pltpu.VMEM((tm, d_model), F32),
                            pltpu.VMEM((PAGE_SLOTS, PAGES_PER_GROUP, page_rows, page_len), F32),
                            pltpu.VMEM((PAGE_SLOTS, PAGES_PER_GROUP, page_rows, page_len), F32),
                            pltpu.SemaphoreType.DMA((PAGE_SLOTS, 2)),
                            pltpu.VMEM((t_new * H_F, W_HEADS), F32)],
        ),
        out_shape=[jax.ShapeDtypeStruct((rows_total, d_model), F32),
                   jax.ShapeDtypeStruct((batch * t_new, W_HEADS), F32)],
        compiler_params=pltpu.CompilerParams(dimension_semantics=("arbitrary",), vmem_limit_bytes=58 * MIB),
        name="ffn_and_fox_sample",
    )(page_table, x2d, hm, hf, wo, g2, wg, wu, wd, g3, qf, grow3, k_new, v_new, page_bias, k_pool, v_pool)


def _ffn_kernel(x_ref, hm_ref, hf_ref, wo_ref, g2_ref, wg_ref, wu_ref, wd_ref, g3_ref, y_ref, *, final_norm):
    x1, h = _merge_residual(x_ref, hm_ref, hf_ref, wo_ref, g2_ref)
    gate = jnp.dot(h, wg_ref[...], preferred_element_type=F32)
    up = jnp.dot(h, wu_ref[...], preferred_element_type=F32)
    act = (gate * _sigmoid(gate) * up).astype(BF16)
    x2 = x1 + jnp.dot(act, wd_ref[...], preferred_element_type=F32)
    y_ref[...] = _rms(x2, g3_ref[...]) if final_norm else x2


def _merge_ffn(x2d, hm, hf, wo, g2, wg, wu, wd, g3, *, final_norm):
    rows_total, d_model = x2d.shape
    tm = min(FFN_ROWS, rows_total)
    row_spec = lambda w: pl.BlockSpec((tm, w), lambda i: (i, 0))
    return pl.pallas_call(
        functools.partial(_ffn_kernel, final_norm=final_norm),
        grid=(rows_total // tm,),
        in_specs=[row_spec(d_model), row_spec(W_HEADS), row_spec(W_HEADS), _const_spec(wo.shape),
                  _const_spec(g2.shape), _const_spec(wg.shape), _const_spec(wu.shape), _const_spec(wd.shape),
                  _const_spec(g3.shape)],
        out_specs=row_spec(d_model),
        out_shape=jax.ShapeDtypeStruct((rows_total, d_model), F32),
        compiler_params=pltpu.CompilerParams(dimension_semantics=("arbitrary",), vmem_limit_bytes=56 * MIB),
        name="merge_ffn",
    )(x2d, hm, hf, wo, g2, wg, wu, wd, g3)


def kernel(x_prompt, x_sample, cache_fox_k, cache_fox_v, cache_fox_logf, page_table, state_mlstm_C,
           state_mlstm_n, state_mlstm_m, norm_mix_g, w_in, b_m_igate, b_m_fgate, b_f_fgate, mlstm_head_g,
           w_out, norm_ffn_g, w_gate, w_up, w_down, norm_final_g):
    depth = w_in.shape[0]
    batch, seq, d_model = x_prompt.shape
    dec_batch, dec_seq, _ = x_sample.shape
    n_pool, page_size = cache_fox_k.shape[1], cache_fox_k.shape[2]
    xp = x_prompt.reshape(batch * seq, d_model)
    xs = x_sample.reshape(dec_batch * dec_seq, d_model)
    g_final = norm_final_g.reshape(1, d_model)
    pk, pv, plf, pc, pn, pm = [], [], [], [], [], []
    sk, sv, slf, sc, sn, sm = [], [], [], [], [], []
    gate0 = N_GATE_ROWS - H_F
    o_gm = 4 * W_HEADS
    o_qf = o_gm + 2 * H_M
    o_gf = o_qf + 3 * W_HEADS
    for l in range(depth):
        wt = jnp.swapaxes(w_in[l], 0, 1)
        wt_main = jnp.concatenate([wt[0:o_gm], wt[o_qf:o_qf + W_HEADS]], axis=0).astype(BF16)
        wt_kv = wt[o_qf + W_HEADS:o_gf].astype(BF16)
        wt_g = jnp.concatenate([wt[o_gm:o_qf], wt[o_gf:o_gf + H_F],
                                jnp.zeros((LANES - N_GATE_ROWS, d_model), F32)], axis=0).astype(BF16)
        bias = jnp.concatenate([b_m_igate[l], b_m_fgate[l], b_f_fgate[l],
                                jnp.zeros((LANES - N_GATE_ROWS,), F32)]).astype(F32)
        bcol = bias.reshape(1, LANES)
        brow = bias[:N_GATE_ROWS].reshape(N_GATE_ROWS, 1)
        g_mix = norm_mix_g[l].reshape(1, d_model)
        g_ffn = norm_ffn_g[l].reshape(1, d_model)
        head_g = mlstm_head_g[l].reshape(1, W_HEADS)
        wo = w_out[l].astype(BF16)
        wg = w_gate[l].astype(BF16)
        wu = w_up[l].astype(BF16)
        wd = w_down[l].astype(BF16)

        qm, km, vm, om, qf, kt, vt, ktb, vtb, gcol, grow = _project(
            xp, g_mix, wt_main, wt_kv, wt_g, bcol, brow, batch=batch, seq=seq, kv_transposed=True, act_dtype=BF16)
        chunk = min(LANES, seq)
        grow3 = grow.reshape(N_GATE_ROWS, batch, seq).transpose(1, 0, 2)
        hm, c_p, n_p, m_p = _mlstm(
            qm, km, vm, om, gcol, grow3, head_g,
            jnp.zeros((batch, H_M, DH_M, DH_M), F32), jnp.zeros((batch, H_M, DH_M), F32),
            jnp.zeros((batch, H_M), F32), batch=batch, seq=seq, chunk=chunk, out_dtype=BF16)
        ccol, crow = _fox_cumsum(gcol, grow, batch=batch, seq=seq)
        hf = _fox_prompt(qf, ktb, vtb, ccol, crow, batch=batch, seq=seq)
        hm_p, hf_p = hm, hf
        pk.append(kt.reshape(batch, H_F, DH_F, seq).transpose(0, 3, 1, 2))
        pv.append(vt.reshape(batch, H_F, DH_F, seq).transpose(0, 3, 1, 2))
        plf.append(grow3[:, gate0:, :].transpose(0, 2, 1))
        pc.append(c_p); pn.append(n_p); pm.append(m_p)

        qm, km, vm, om, qf, k_new, v_new, gcol, grow = _project(
            xs, g_mix, wt_main, wt_kv, wt_g, bcol, brow, batch=dec_batch, seq=dec_seq, kv_transposed=False,
            act_dtype=F32)
        grow3 = grow.reshape(N_GATE_ROWS, dec_batch, dec_seq).transpose(1, 0, 2)
        hm, c_s, n_s, m_s = _mlstm(
            qm, km, vm, om, gcol, grow3, head_g,
            state_mlstm_C[l].astype(F32), state_mlstm_n[l].astype(F32), state_mlstm_m[l].astype(F32),
            batch=dec_batch, seq=dec_seq, chunk=dec_seq, out_dtype=F32)
        k_pool = cache_fox_k[l].transpose(0, 2, 3, 1).reshape(n_pool, W_HEADS, page_size)
        v_pool = cache_fox_v[l].transpose(0, 2, 3, 1).reshape(n_pool, W_HEADS, page_size)
        lf_pool = cache_fox_logf[l].transpose(0, 2, 1)
        page_bias = _page_bias(page_table, lf_pool)
        xp, hf = _merge_ffn_and_fox_sample(
            xp, hm_p, hf_p, wo, g_ffn, wg, wu, wd, g_final, page_table, qf, grow3, k_new, v_new, page_bias,
            k_pool, v_pool, final_norm=(l == depth - 1))
        xs = _merge_ffn(xs, hm, hf, wo, g_ffn, wg, wu, wd, g_final, final_norm=(l == depth - 1))
        sk.append(k_new.reshape(dec_batch, dec_seq, H_F, DH_F))
        sv.append(v_new.reshape(dec_batch, dec_seq, H_F, DH_F))
        slf.append(grow3[:, gate0:, :].transpose(0, 2, 1))
        sc.append(c_s); sn.append(n_s); sm.append(m_s)

    st = lambda a, ref: jnp.stack(a, axis=0).astype(ref.dtype)
    return (xp.reshape(batch, seq, d_model), xs.reshape(dec_batch, dec_seq, d_model),
            st(pk, cache_fox_k), st(pv, cache_fox_v), st(plf, cache_fox_logf),
            st(pc, state_mlstm_C), st(pn, state_mlstm_n), st(pm, state_mlstm_m),
            st(sk, cache_fox_k), st(sv, cache_fox_v), st(slf, cache_fox_logf),
            st(sc, state_mlstm_C), st(sn, state_mlstm_n), st(sm, state_mlstm_m))
```

```python
import functools

import jax
import jax.numpy as jnp
from jax import lax
from jax.experimental import pallas as pl
from jax.experimental.pallas import tpu as pltpu

F32 = jnp.float32
BF16 = jnp.bfloat16
HI = lax.Precision.HIGHEST
NT = (((1,), (1,)), ((), ()))
TN = (((0,), (0,)), ((), ()))

EPS = 1e-6
LOG2E = 1.4426950408889634
H_M = 4
DH_M = 128
H_F = 8
DH_F = 64
W_HEADS = 512
N_GATE_ROWS = 16
LANES = 128
MIB = 1024 * 1024

PROJ_ROWS = 512
FFN_ROWS = 512
FFN_CHUNK = 256
MLSTM_BATCH = 8
FOX_BLOCK = 512
PAGES_PER_GROUP = 8
PAGE_SLOTS = 4


def _rms(x, g):
    return x * lax.rsqrt(jnp.mean(x * x, axis=-1, keepdims=True) + EPS) * g


def _log_sigmoid(x):
    return jnp.minimum(x, 0.0) - jnp.log1p(jnp.exp(-jnp.abs(x)))


def _sigmoid(x):
    return 1.0 / (1.0 + jnp.exp(-x))


def _const_spec(shape):
    return pl.BlockSpec(shape, lambda *_: (0,) * len(shape), pipeline_mode=pl.Buffered(1))


def _proj_kernel(x_ref, g_ref, wt_ref, wkv_ref, wg_ref, bcol_ref, brow_ref, *outs, kv_transposed):
    if kv_transposed:
        qm_ref, km_ref, vm_ref, om_ref, qf_ref, kf_ref, vf_ref, kfb_ref, vfb_ref, gcol_ref, grow_ref = outs
    else:
        qm_ref, km_ref, vm_ref, om_ref, qf_ref, kf_ref, vf_ref, gcol_ref, grow_ref = outs
    h = _rms(x_ref[...], g_ref[...]).astype(BF16)
    rows = h.shape[0]

    def mm(i):
        w = wt_ref[i * W_HEADS:(i + 1) * W_HEADS, :]
        return lax.dot_general(h, w, NT, preferred_element_type=F32)

    qm_ref[...] = mm(0).astype(qm_ref.dtype)
    km_ref[...] = (mm(1) * (DH_M ** -0.5)).astype(km_ref.dtype)
    vm_ref[...] = mm(2).astype(vm_ref.dtype)
    om_ref[...] = mm(3).astype(om_ref.dtype)
    qf_ref[...] = (mm(4) * (DH_F ** -0.5 * LOG2E)).astype(qf_ref.dtype)
    if kv_transposed:
        kt = lax.dot_general(wkv_ref[0:W_HEADS, :], h, NT, preferred_element_type=F32)
        kf_ref[0] = kt
        kfb_ref[0] = kt.astype(BF16)
        vt = lax.dot_general(wkv_ref[W_HEADS:2 * W_HEADS, :], h, NT, preferred_element_type=F32)
        vf_ref[0] = vt
        vfb_ref[0] = vt.astype(BF16)
    else:
        kf_ref[...] = lax.dot_general(h, wkv_ref[0:W_HEADS, :], NT, preferred_element_type=F32)
        vf_ref[...] = lax.dot_general(h, wkv_ref[W_HEADS:2 * W_HEADS, :], NT, preferred_element_type=F32)
    pre_c = lax.dot_general(h, wg_ref[...], NT, preferred_element_type=F32) + bcol_ref[...]
    lane = lax.broadcasted_iota(jnp.int32, (rows, LANES), 1)
    gcol_ref[...] = jnp.where(lane < H_M, pre_c, _log_sigmoid(pre_c))
    pre_r = lax.dot_general(wg_ref[0:N_GATE_ROWS, :], h, NT, preferred_element_type=F32) + brow_ref[...]
    row = lax.broadcasted_iota(jnp.int32, (N_GATE_ROWS, rows), 0)
    grow_ref[...] = jnp.where(row < H_M, pre_r, _log_sigmoid(pre_r))


def _project(x2d, g, wt_main, wt_kv, wt_g, bcol, brow, *, batch, seq, kv_transposed, act_dtype):
    rows_total, d_model = x2d.shape
    tm = min(PROJ_ROWS, rows_total)
    steps = rows_total // tm
    per_seq = max(seq // tm, 1)
    row_spec = lambda w: pl.BlockSpec((tm, w), lambda i: (i, 0))
    in_specs = [row_spec(d_model), _const_spec(g.shape), _const_spec(wt_main.shape), _const_spec(wt_kv.shape),
                _const_spec(wt_g.shape), _const_spec(bcol.shape), _const_spec(brow.shape)]
    act = jax.ShapeDtypeStruct((rows_total, W_HEADS), act_dtype)
    act32 = jax.ShapeDtypeStruct((rows_total, W_HEADS), F32)
    out_shape = [act, act, act, act32, act]
    out_specs = [row_spec(W_HEADS)] * 5
    if kv_transposed:
        kv_spec = pl.BlockSpec((1, W_HEADS, tm), lambda i: (i // per_seq, 0, i % per_seq))
        out_shape += [jax.ShapeDtypeStruct((batch, W_HEADS, seq), F32)] * 2
        out_shape += [jax.ShapeDtypeStruct((batch, W_HEADS, seq), BF16)] * 2
        out_specs += [kv_spec] * 4
    else:
        out_shape += [act32, act32]
        out_specs += [row_spec(W_HEADS)] * 2
    out_shape += [jax.ShapeDtypeStruct((rows_total, LANES), F32),
                  jax.ShapeDtypeStruct((N_GATE_ROWS, rows_total), F32)]
    out_specs += [row_spec(LANES), pl.BlockSpec((N_GATE_ROWS, tm), lambda i: (0, i))]
    return pl.pallas_call(
        functools.partial(_proj_kernel, kv_transposed=kv_transposed),
        grid=(steps,),
        in_specs=in_specs,
        out_specs=out_specs,
        out_shape=out_shape,
        compiler_params=pltpu.CompilerParams(dimension_semantics=("arbitrary",), vmem_limit_bytes=48 * MIB),
        name="proj",
    )(x2d, g, wt_main, wt_kv, wt_g, bcol, brow)


def _mlstm_kernel(q_ref, k_ref, v_ref, om_ref, gc_ref, gr_ref, hg_ref, c0_ref, n0_ref, m0_ref,
                  hm_ref, c_ref, n_ref, m_ref, *, bb, chunk):
    @pl.when(pl.program_id(1) == 0)
    def _():
        c_ref[...] = c0_ref[...]
        n_ref[...] = n0_ref[...]
        m_ref[...] = m0_ref[...]

    t_idx = lax.broadcasted_iota(jnp.int32, (chunk, chunk), 0)
    s_idx = lax.broadcasted_iota(jnp.int32, (chunk, chunk), 1)
    causal = s_idx <= t_idx
    tril = causal.astype(F32)
    triu = (t_idx <= s_idx).astype(F32)

    def per_batch(b, carry):
        gc = gc_ref[b]
        gr = gr_ref[b]
        bc = jnp.dot(tril, gc, precision=HI, preferred_element_type=F32)
        br = jnp.dot(gr, triu, precision=HI, preferred_element_type=F32)
        q = q_ref[b]
        k = k_ref[b]
        v = v_ref[b]
        om = om_ref[b]
        for h in range(H_M):
            sl = slice(h * DH_M, (h + 1) * DH_M)
            li_c = gc[:, h:h + 1]
            li_r = gr[h:h + 1, :]
            b_c = bc[:, H_M + h:H_M + h + 1]
            b_r = br[H_M + h:H_M + h + 1, :]
            m_prev = m_ref[pl.ds(b, 1), h:h + 1]
            dmat = jnp.where(causal, b_c - b_r + li_r, -jnp.inf)
            inter = m_prev + b_c
            m_t = jnp.maximum(inter, jnp.max(dmat, axis=-1, keepdims=True))
            qh = q[:, sl].astype(BF16)
            kh = k[:, sl].astype(BF16)
            vh = v[:, sl].astype(BF16)
            smat = lax.dot_general(qh, kh, NT, preferred_element_type=F32) * jnp.exp(dmat - m_t)
            w_inter = jnp.exp(inter - m_t)
            c_prev = c_ref[b, h]
            cq = lax.dot_general(qh, c_prev.astype(BF16), NT, preferred_element_type=F32)
            num = w_inter * cq + jnp.dot(smat.astype(BF16), vh, preferred_element_type=F32)
            n_prev = n_ref[b, h:h + 1, :]
            nq = jnp.sum(qh.astype(F32) * n_prev, axis=-1, keepdims=True)
            den = w_inter * nq + jnp.sum(smat, axis=-1, keepdims=True)
            hh = num / jnp.maximum(jnp.abs(den), jnp.exp(-m_t))
            hn = hh * lax.rsqrt(jnp.mean(hh * hh, axis=-1, keepdims=True) + EPS) * hg_ref[:, sl]
            hm_ref[b, :, sl] = (hn * _sigmoid(om[:, sl].astype(F32))).astype(hm_ref.dtype)
            b_last = b_c[chunk - 1:chunk, :]
            m_new = m_t[chunk - 1:chunk, :]
            decay = jnp.exp(m_prev + b_last - m_new)
            w_s = jnp.exp(li_c + b_last - b_c - m_new)
            vw = (vh.astype(F32) * w_s).astype(BF16)
            c_ref[b, h] = decay * c_prev + lax.dot_general(vw, kh, TN, preferred_element_type=F32)
            n_ref[b, h:h + 1, :] = decay * n_prev + jnp.sum(kh.astype(F32) * w_s, axis=0, keepdims=True)
            m_ref[pl.ds(b, 1), h:h + 1] = m_new
        return carry

    lax.fori_loop(0, bb, per_batch, 0)


def _mlstm(qm, km, vm, om, gcol, grow3, head_g, c0, n0, m0, *, batch, seq, chunk, out_dtype):
    bb = MLSTM_BATCH
    n_chunks = seq // chunk
    as3 = lambda a: a.reshape(batch, seq, a.shape[-1])
    tok = lambda w: pl.BlockSpec((bb, chunk, w), lambda g, c: (g, c, 0))
    state = lambda shape: pl.BlockSpec((bb,) + shape, lambda g, c: (g,) + (0,) * len(shape))
    in_specs = [tok(W_HEADS), tok(W_HEADS), tok(W_HEADS), tok(W_HEADS), tok(LANES),
                pl.BlockSpec((bb, N_GATE_ROWS, chunk), lambda g, c: (g, 0, c)),
                _const_spec(head_g.shape),
                state((H_M, DH_M, DH_M)), state((H_M, DH_M)), state((H_M,))]
    out_specs = [tok(W_HEADS), state((H_M, DH_M, DH_M)), state((H_M, DH_M)), state((H_M,))]
    out_shape = [jax.ShapeDtypeStruct((batch, seq, W_HEADS), out_dtype),
                 jax.ShapeDtypeStruct(c0.shape, F32), jax.ShapeDtypeStruct(n0.shape, F32),
                 jax.ShapeDtypeStruct(m0.shape, F32)]
    hm, c_new, n_new, m_new = pl.pallas_call(
        functools.partial(_mlstm_kernel, bb=bb, chunk=chunk),
        grid=(batch // bb, n_chunks),
        in_specs=in_specs,
        out_specs=out_specs,
        out_shape=out_shape,
        compiler_params=pltpu.CompilerParams(dimension_semantics=("arbitrary", "arbitrary"),
                                             vmem_limit_bytes=48 * MIB),
        name="mlstm",
    )(as3(qm), as3(km), as3(vm), as3(om), as3(gcol), grow3, head_g, c0, n0, m0)
    return hm.reshape(batch * seq, W_HEADS), c_new, n_new, m_new


def _fox_cumsum_kernel(gc_ref, gr_ref, cc_ref, cr_ref):
    seq = gc_ref.shape[1]
    t_idx = lax.broadcasted_iota(jnp.int32, (LANES, LANES), 0)
    s_idx = lax.broadcasted_iota(jnp.int32, (LANES, LANES), 1)
    tril = (s_idx <= t_idx).astype(F32)
    triu = (t_idx <= s_idx).astype(F32)
    carry_c = jnp.zeros((1, LANES), F32)
    carry_r = jnp.zeros((N_GATE_ROWS, 1), F32)
    for j in range(seq // LANES):
        blk = slice(j * LANES, (j + 1) * LANES)
        cb = jnp.dot(tril, gc_ref[0, blk, :], precision=HI, preferred_element_type=F32) + carry_c
        cc_ref[0, blk, :] = cb * LOG2E
        carry_c = cb[LANES - 1:LANES, :]
        rb = jnp.dot(gr_ref[:, blk], triu, precision=HI, preferred_element_type=F32) + carry_r
        cr_ref[:, blk] = rb * LOG2E
        carry_r = rb[:, LANES - 1:LANES]


def _fox_cumsum(gcol, grow, *, batch, seq):
    return pl.pallas_call(
        _fox_cumsum_kernel,
        grid=(batch,),
        in_specs=[pl.BlockSpec((1, seq, LANES), lambda b: (b, 0, 0)),
                  pl.BlockSpec((N_GATE_ROWS, seq), lambda b: (0, b))],
        out_specs=[pl.BlockSpec((1, seq, LANES), lambda b: (b, 0, 0)),
                   pl.BlockSpec((N_GATE_ROWS, seq), lambda b: (0, b))],
        out_shape=[jax.ShapeDtypeStruct((batch, seq, LANES), F32),
                   jax.ShapeDtypeStruct((N_GATE_ROWS, batch * seq), F32)],
        compiler_params=pltpu.CompilerParams(dimension_semantics=("arbitrary",)),
        name="fox_cumsum",
    )(gcol.reshape(batch, seq, LANES), grow)


def _fox_prompt_kernel(q_ref, kt_ref, vt_ref, cc_ref, cr_ref, o_ref, qs_ref, cq_ref, m_ref, l_ref, acc_ref, *, blk):
    i = pl.program_id(1)
    n_pairs = H_F // 2
    lane_tiles = blk // LANES
    lane = lax.broadcasted_iota(jnp.int32, (blk, LANES), 1)
    low_half = lane < DH_F
    r_idx = lax.broadcasted_iota(jnp.int32, (blk, blk), 0)
    c_idx = lax.broadcasted_iota(jnp.int32, (blk, blk), 1)
    diag_mask = jnp.concatenate([c_idx <= r_idx] * 2, axis=0)
    gate0 = N_GATE_ROWS - H_F

    for pair in range(n_pairs):
        rows = slice(pair * LANES, (pair + 1) * LANES)
        q_pair = q_ref[:, rows]
        zero = jnp.zeros_like(q_pair)
        qs_ref[pair, 0:blk, :] = jnp.where(low_half, q_pair, zero)
        qs_ref[pair, blk:2 * blk, :] = jnp.where(low_half, zero, q_pair)
        for e in range(2):
            col = gate0 + 2 * pair + e
            cq_ref[pair, e * blk:(e + 1) * blk, :] = jnp.broadcast_to(cc_ref[:, col:col + 1], (blk, LANES))
    m_ref[...] = jnp.full(m_ref.shape, -jnp.inf, F32)
    l_ref[...] = jnp.zeros(l_ref.shape, F32)
    acc_ref[...] = jnp.zeros(acc_ref.shape, F32)

    def step(j, masked):
        cols = pl.ds(pl.multiple_of(j * blk, blk), blk)
        for pair in range(n_pairs):
            rows = slice(pair * LANES, (pair + 1) * LANES)
            kj = kt_ref[0, rows, cols]
            vj = vt_ref[0, rows, cols]
            s = jnp.dot(qs_ref[pair], kj, preferred_element_type=F32)
            ck = jnp.concatenate(
                [jnp.broadcast_to(cr_ref[gate0 + 2 * pair + e:gate0 + 2 * pair + e + 1, cols], (blk, blk))
                 for e in range(2)], axis=0)
            s = s + (jnp.concatenate([cq_ref[pair]] * lane_tiles, axis=1) - ck)
            if masked:
                s = jnp.where(diag_mask, s, -jnp.inf)
            m_old = m_ref[pair]
            m_new = jnp.maximum(m_old, jnp.max(s, axis=-1, keepdims=True))
            p = jnp.exp2(s - jnp.concatenate([m_new] * lane_tiles, axis=1))
            alpha = jnp.exp2(m_old - m_new)
            p_lanes = p[:, 0:LANES]
            for t in range(1, lane_tiles):
                p_lanes = p_lanes + p[:, t * LANES:(t + 1) * LANES]
            l_ref[pair] = alpha * l_ref[pair] + p_lanes
            pv = lax.dot_general(p.astype(BF16), vj, NT, preferred_element_type=F32)
            acc_ref[pair] = alpha * acc_ref[pair] + pv
            m_ref[pair] = m_new

    def loop_body(j, carry):
        step(j, False)
        return carry

    lax.fori_loop(0, i, loop_body, 0)
    step(i, True)
    for pair in range(n_pairs):
        rows = slice(pair * LANES, (pair + 1) * LANES)
        out = acc_ref[pair] / jnp.sum(l_ref[pair], axis=-1, keepdims=True)
        o_ref[:, rows] = jnp.where(low_half, out[0:blk], out[blk:2 * blk]).astype(o_ref.dtype)


def _fox_prompt(qf, ktb, vtb, ccol, crow, *, batch, seq):
    blk = FOX_BLOCK
    nq = seq // blk
    return pl.pallas_call(
        functools.partial(_fox_prompt_kernel, blk=blk),
        grid=(batch, nq),
        in_specs=[pl.BlockSpec((blk, W_HEADS), lambda b, i: (b * nq + i, 0)),
                  pl.BlockSpec((1, W_HEADS, seq), lambda b, i: (b, 0, 0)),
                  pl.BlockSpec((1, W_HEADS, seq), lambda b, i: (b, 0, 0)),
                  pl.BlockSpec((blk, LANES), lambda b, i: (b * nq + i, 0)),
                  pl.BlockSpec((N_GATE_ROWS, seq), lambda b, i: (0, b))],
        out_specs=pl.BlockSpec((blk, W_HEADS), lambda b, i: (b * nq + i, 0)),
        out_shape=jax.ShapeDtypeStruct((batch * seq, W_HEADS), BF16),
        scratch_shapes=[pltpu.VMEM((H_F // 2, 2 * blk, LANES), BF16)] + [pltpu.VMEM((H_F // 2, 2 * blk, LANES), F32)] * 4,
        compiler_params=pltpu.CompilerParams(dimension_semantics=("arbitrary", "arbitrary"),
                                             vmem_limit_bytes=48 * MIB),
        name="fox_prompt",
    )(qf, ktb, vtb, ccol.reshape(batch * seq, LANES), crow)


def _page_bias_kernel(pt_ref, lf_hbm, o_ref, buf, sem):
    b = pl.program_id(0)
    nb = pl.num_programs(0)
    n_pages = buf.shape[1]
    slot = b % 2

    def page_copy(bi, p, sl):
        return pltpu.make_async_copy(lf_hbm.at[pt_ref[bi, p]], buf.at[sl, p], sem.at[sl])

    def start_all(bi, sl):
        def body(p, c):
            page_copy(bi, p, sl).start()
            return c
        lax.fori_loop(0, n_pages, body, 0)

    @pl.when(b == 0)
    def _():
        start_all(0, 0)

    @pl.when(b + 1 < nb)
    def _():
        start_all(b + 1, 1 - slot)

    def wait_body(p, c):
        page_copy(b, p, slot).wait()
        return c
    lax.fori_loop(0, n_pages, wait_body, 0)

    x = buf[slot].reshape(n_pages * H_F, LANES)
    t_idx = lax.broadcasted_iota(jnp.int32, (LANES, LANES), 0)
    s_idx = lax.broadcasted_iota(jnp.int32, (LANES, LANES), 1)
    later = (t_idx > s_idx).astype(F32)
    within = jnp.dot(x, later, precision=HI, preferred_element_type=F32)
    total = jnp.sum(x, axis=-1, keepdims=True)
    run = jnp.zeros((H_F, LANES), F32)
    for p in range(n_pages - 1, -1, -1):
        rows = slice(p * H_F, (p + 1) * H_F)
        o_ref[0, p] = (within[rows] + run) * (-LOG2E)
        run = run + total[rows]


def _page_bias(page_table, lf_pool):
    batch, n_pages = page_table.shape
    return pl.pallas_call(
        _page_bias_kernel,
        grid_spec=pltpu.PrefetchScalarGridSpec(
            num_scalar_prefetch=1,
            grid=(batch,),
            in_specs=[pl.BlockSpec(memory_space=pl.ANY)],
            out_specs=pl.BlockSpec((1, n_pages, H_F, LANES), lambda b, pt: (b, 0, 0, 0)),
            scratch_shapes=[pltpu.VMEM((2, n_pages, H_F, LANES), F32), pltpu.SemaphoreType.DMA((2,))],
        ),
        out_shape=jax.ShapeDtypeStruct((batch, n_pages, H_F, LANES), F32),
        compiler_params=pltpu.CompilerParams(dimension_semantics=("arbitrary",)),
        name="page_bias",
    )(page_table, lf_pool)


def _paged_attention(pt_ref, q_ref, gr_ref, kn_ref, vn_ref, cp_ref, k_hbm, v_hbm, o_ref,
                     kbuf, vbuf, sem, acc_ref, *, n_groups, side_work, n_side):
    b = pl.program_id(0)
    nb = pl.num_programs(0)
    group = PAGES_PER_GROUP
    t_new = q_ref.shape[0]
    n_rows = t_new * H_F

    def page_copies(bi, g, sl):
        cps = []
        for j in range(group):
            page = pt_ref[bi, g * group + j]
            cps.append(pltpu.make_async_copy(k_hbm.at[page], kbuf.at[sl, j], sem.at[sl, 0]))
            cps.append(pltpu.make_async_copy(v_hbm.at[page], vbuf.at[sl, j], sem.at[sl, 1]))
        return cps

    def start_group(bi, g, sl):
        for cp in page_copies(bi, g, sl):
            cp.start()

    @pl.when(b == 0)
    def _():
        for a in range(PAGE_SLOTS - 1):
            start_group(0, a, a)

    sub = lax.broadcasted_iota(jnp.int32, (H_F, W_HEADS), 0)
    lane = lax.broadcasted_iota(jnp.int32, (H_F, W_HEADS), 1)
    own_head = (lane // DH_F) == sub
    q = q_ref[...].astype(F32)
    qbd = jnp.concatenate(
        [jnp.where(own_head, jnp.broadcast_to(q[t:t + 1, :], (H_F, W_HEADS)), 0.0) for t in range(t_new)],
        axis=0).astype(BF16)

    gate0 = N_GATE_ROWS - H_F
    lf_new = gr_ref[0, gate0:N_GATE_ROWS, :]
    a_idx = lax.broadcasted_iota(jnp.int32, (t_new, t_new), 0)
    b_idx = lax.broadcasted_iota(jnp.int32, (t_new, t_new), 1)
    c_new = jnp.dot(lf_new, (a_idx <= b_idx).astype(F32), precision=HI, preferred_element_type=F32) * LOG2E
    cq = jnp.concatenate([c_new[:, t:t + 1] for t in range(t_new)], axis=0)

    acc_ref[...] = jnp.zeros_like(acc_ref)

    def body(g, carry, with_side):
        m_i, l_i = carry
        slot = g % PAGE_SLOTS
        ahead = g + (PAGE_SLOTS - 1)

        @pl.when(ahead < n_groups)
        def _():
            start_group(b, ahead, ahead % PAGE_SLOTS)

        @pl.when(jnp.logical_and(ahead >= n_groups, b + 1 < nb))
        def _():
            start_group(b + 1, ahead - n_groups, ahead % PAGE_SLOTS)

        for cp in page_copies(b, g, slot):
            cp.wait()
        side = side_work(g) if with_side else iter(())

        kcat = jnp.concatenate([kbuf[slot, j].astype(BF16) for j in range(group)], axis=1)
        s = jnp.dot(qbd, kcat, preferred_element_type=F32)
        next(side, None)
        first = pl.multiple_of(g * group, group)
        cpg = cp_ref[0, pl.ds(first, group)]
        bias = jnp.concatenate(
            [jnp.broadcast_to(cpg[j][None], (t_new, H_F, LANES)).reshape(n_rows, LANES) for j in range(group)],
            axis=1)
        s = s + (cq - bias)
        m_new = jnp.maximum(m_i, jnp.max(s, axis=-1, keepdims=True))
        p = jnp.exp2(s - m_new)
        alpha = jnp.exp2(m_i - m_new)
        l_new = alpha * l_i + jnp.sum(p, axis=-1, keepdims=True)
        next(side, None)
        vcat = jnp.concatenate([vbuf[slot, j].astype(BF16) for j in range(group)], axis=1)
        pv = lax.dot_general(p.astype(BF16), vcat, NT, preferred_element_type=F32)
        next(side, None)
        acc_ref[...] = alpha * acc_ref[...] + pv
        return m_new, l_new

    init = (jnp.full((n_rows, 1), -jnp.inf, F32), jnp.zeros((n_rows, 1), F32))
    carry = lax.fori_loop(0, n_side, functools.partial(body, with_side=True), init)
    m_i, l_i = lax.fori_loop(n_side, n_groups, functools.partial(body, with_side=False), carry)

    kn = kn_ref[...].astype(BF16)
    vn = vn_ref[...].astype(BF16)
    s = lax.dot_general(qbd, kn, NT, preferred_element_type=F32)
    ck = jnp.broadcast_to(c_new[None], (t_new, H_F, t_new)).reshape(n_rows, t_new)
    s = s + (cq - ck)
    r_idx = lax.broadcasted_iota(jnp.int32, (n_rows, t_new), 0)
    k_idx = lax.broadcasted_iota(jnp.int32, (n_rows, t_new), 1)
    s = jnp.where(k_idx <= r_idx // H_F, s, -jnp.inf)
    m_new = jnp.maximum(m_i, jnp.max(s, axis=-1, keepdims=True))
    p = jnp.exp2(s - m_new)
    alpha = jnp.exp2(m_i - m_new)
    l_fin = alpha * l_i + jnp.sum(p, axis=-1, keepdims=True)
    acc = alpha * acc_ref[...] + jnp.dot(p.astype(BF16), vn, preferred_element_type=F32)
    out = acc / l_fin
    o_ref[...] = jnp.concatenate(
        [jnp.sum(jnp.where(own_head, out[t * H_F:(t + 1) * H_F, :], 0.0), axis=0, keepdims=True)
         for t in range(t_new)], axis=0).astype(o_ref.dtype)


def _merge_residual(x_ref, hm_ref, hf_ref, wo_ref, g2_ref):
    x1 = (x_ref[...]
          + jnp.dot(hm_ref[...].astype(BF16), wo_ref[0:W_HEADS, :], preferred_element_type=F32)
          + jnp.dot(hf_ref[...].astype(BF16), wo_ref[W_HEADS:2 * W_HEADS, :], preferred_element_type=F32))
    return x1, _rms(x1, g2_ref[...]).astype(BF16)


def _ffn_sample_kernel(pt_ref, x_ref, hm_ref, hf_ref, wo_ref, g2_ref, wg_ref, wu_ref, wd_ref, g3_ref,
                       q_ref, gr_ref, kn_ref, vn_ref, cp_ref, k_hbm, v_hbm, y_ref, o_ref,
                       h_ref, x2_ref, kbuf, vbuf, sem, acc_ref, *, n_groups, final_norm):
    x1, h = _merge_residual(x_ref, hm_ref, hf_ref, wo_ref, g2_ref)
    h_ref[...] = h
    x2_ref[...] = x1

    def ffn_chunk(c):
        cols = pl.ds(pl.multiple_of(c * FFN_CHUNK, FFN_CHUNK), FFN_CHUNK)
        hh = h_ref[...]
        gate = jnp.dot(hh, wg_ref[:, cols], preferred_element_type=F32)
        yield
        up = jnp.dot(hh, wu_ref[:, cols], preferred_element_type=F32)
        act = (gate * _sigmoid(gate) * up).astype(BF16)
        yield
        x2_ref[...] += jnp.dot(act, wd_ref[cols, :], preferred_element_type=F32)
        yield

    _paged_attention(pt_ref, q_ref, gr_ref, kn_ref, vn_ref, cp_ref, k_hbm, v_hbm, o_ref, kbuf, vbuf, sem, acc_ref,
                     n_groups=n_groups, side_work=ffn_chunk, n_side=wg_ref.shape[1] // FFN_CHUNK)
    x2 = x2_ref[...]
    y_ref[...] = _rms(x2, g3_ref[...]) if final_norm else x2


def _merge_ffn_and_fox_sample(x2d, hm, hf, wo, g2, wg, wu, wd, g3, page_table, qf, grow3, k_new, v_new, page_bias,
                              k_pool, v_pool, *, final_norm):
    rows_total, d_model = x2d.shape
    batch, n_pages = page_table.shape
    t_new = qf.shape[0] // batch
    n_groups = n_pages // PAGES_PER_GROUP
    tm = rows_total // batch
    d_ff = wg.shape[1]
    assert n_pages % PAGES_PER_GROUP == 0 and n_groups % PAGE_SLOTS == 0
    assert rows_total % batch == 0 and tm % 16 == 0 and d_ff % FFN_CHUNK == 0 and d_ff // FFN_CHUNK <= n_groups
    page_rows, page_len = k_pool.shape[1], k_pool.shape[2]
    row = lambda w: pl.BlockSpec((tm, w), lambda b, pt: (b, 0))
    tok = lambda w: pl.BlockSpec((t_new, w), lambda b, pt: (b, 0))
    return pl.pallas_call(
        functools.partial(_ffn_sample_kernel, n_groups=n_groups, final_norm=final_norm),
        grid_spec=pltpu.PrefetchScalarGridSpec(
            num_scalar_prefetch=1,
            grid=(batch,),
            in_specs=[row(d_model), row(W_HEADS), row(W_HEADS), _const_spec(wo.shape), _const_spec(g2.shape),
                      _const_spec(wg.shape), _const_spec(wu.shape), _const_spec(wd.shape), _const_spec(g3.shape),
                      tok(W_HEADS),
                      pl.BlockSpec((1, N_GATE_ROWS, t_new), lambda b, pt: (b, 0, 0)),
                      tok(W_HEADS), tok(W_HEADS),
                      pl.BlockSpec((1, n_pages, H_F, LANES), lambda b, pt: (b, 0, 0, 0)),
                      pl.BlockSpec(memory_space=pl.ANY), pl.BlockSpec(memory_space=pl.ANY)],
            out_specs=[row(d_model), tok(W_HEADS)],
            scratch_shapes=[pltpu.VMEM((tm, d_model), BF16),
                            pltpu.VMEM((tm, d_model), F32),
                            pltpu.VMEM((PAGE_SLOTS, PAGES_PER_GROUP, page_rows, page_len), F32),
                            pltpu.VMEM((PAGE_SLOTS, PAGES_PER_GROUP, page_rows, page_len), F32),
                            pltpu.SemaphoreType.DMA((PAGE_SLOTS, 2)),
                            pltpu.VMEM((t_new * H_F, W_HEADS), F32)],
        ),
        out_shape=[jax.ShapeDtypeStruct((rows_total, d_model), F32),
                   jax.ShapeDtypeStruct((batch * t_new, W_HEADS), F32)],
        compiler_params=pltpu.CompilerParams(dimension_semantics=("arbitrary",), vmem_limit_bytes=58 * MIB),
        name="ffn_and_fox_sample",
    )(page_table, x2d, hm, hf, wo, g2, wg, wu, wd, g3, qf, grow3, k_new, v_new, page_bias, k_pool, v_pool)


def _ffn_kernel(x_ref, hm_ref, hf_ref, wo_ref, g2_ref, wg_ref, wu_ref, wd_ref, g3_ref, y_ref, *, final_norm):
    x1, h = _merge_residual(x_ref, hm_ref, hf_ref, wo_ref, g2_ref)
    gate = jnp.dot(h, wg_ref[...], preferred_element_type=F32)
    up = jnp.dot(h, wu_ref[...], preferred_element_type=F32)
    act = (gate * _sigmoid(gate) * up).astype(BF16)
    x2 = x1 + jnp.dot(act, wd_ref[...], preferred_element_type=F32)
    y_ref[...] = _rms(x2, g3_ref[...]) if final_norm else x2


def _merge_ffn(x2d, hm, hf, wo, g2, wg, wu, wd, g3, *, final_norm):
    rows_total, d_model = x2d.shape
    tm = min(FFN_ROWS, rows_total)
    row_spec = lambda w: pl.BlockSpec((tm, w), lambda i: (i, 0))
    return pl.pallas_call(
        functools.partial(_ffn_kernel, final_norm=final_norm),
        grid=(rows_total // tm,),
        in_specs=[row_spec(d_model), row_spec(W_HEADS), row_spec(W_HEADS), _const_spec(wo.shape),
                  _const_spec(g2.shape), _const_spec(wg.shape), _const_spec(wu.shape), _const_spec(wd.shape),
                  _const_spec(g3.shape)],
        out_specs=row_spec(d_model),
        out_shape=jax.ShapeDtypeStruct((rows_total, d_model), F32),
        compiler_params=pltpu.CompilerParams(dimension_semantics=("arbitrary",), vmem_limit_bytes=56 * MIB),
        name="merge_ffn",
    )(x2d, hm, hf, wo, g2, wg, wu, wd, g3)


def kernel(x_prompt, x_sample, cache_fox_k, cache_fox_v, cache_fox_logf, page_table, state_mlstm_C,
           state_mlstm_n, state_mlstm_m, norm_mix_g, w_in, b_m_igate, b_m_fgate, b_f_fgate, mlstm_head_g,
           w_out, norm_ffn_g, w_gate, w_up, w_down, norm_final_g):
    depth = w_in.shape[0]
    batch, seq, d_model = x_prompt.shape
    dec_batch, dec_seq, _ = x_sample.shape
    n_pool, page_size = cache_fox_k.shape[1], cache_fox_k.shape[2]
    xp = x_prompt.reshape(batch * seq, d_model)
    xs = x_sample.reshape(dec_batch * dec_seq, d_model)
    g_final = norm_final_g.reshape(1, d_model)
    pk, pv, plf, pc, pn, pm = [], [], [], [], [], []
    sk, sv, slf, sc, sn, sm = [], [], [], [], [], []
    gate0 = N_GATE_ROWS - H_F
    o_gm = 4 * W_HEADS
    o_qf = o_gm + 2 * H_M
    o_gf = o_qf + 3 * W_HEADS
    for l in range(depth):
        wt = jnp.swapaxes(w_in[l], 0, 1)
        wt_main = jnp.concatenate([wt[0:o_gm], wt[o_qf:o_qf + W_HEADS]], axis=0).astype(BF16)
        wt_kv = wt[o_qf + W_HEADS:o_gf].astype(BF16)
        wt_g = jnp.concatenate([wt[o_gm:o_qf], wt[o_gf:o_gf + H_F],
                                jnp.zeros((LANES - N_GATE_ROWS, d_model), F32)], axis=0).astype(BF16)
        bias = jnp.concatenate([b_m_igate[l], b_m_fgate[l], b_f_fgate[l],
                                jnp.zeros((LANES - N_GATE_ROWS,), F32)]).astype(F32)
        bcol = bias.reshape(1, LANES)
        brow = bias[:N_GATE_ROWS].reshape(N_GATE_ROWS, 1)
        g_mix = norm_mix_g[l].reshape(1, d_model)
        g_ffn = norm_ffn_g[l].reshape(1, d_model)
        head_g = mlstm_head_g[l].reshape(1, W_HEADS)
        wo = w_out[l].astype(BF16)
        wg = w_gate[l].astype(BF16)
        wu = w_up[l].astype(BF16)
        wd = w_down[l].astype(BF16)

        qm, km, vm, om, qf, kt, vt, ktb, vtb, gcol, grow = _project(
            xp, g_mix, wt_main, wt_kv, wt_g, bcol, brow, batch=batch, seq=seq, kv_transposed=True, act_dtype=BF16)
        chunk = min(LANES, seq)
        grow3 = grow.reshape(N_GATE_ROWS, batch, seq).transpose(1, 0, 2)
        hm, c_p, n_p, m_p = _mlstm(
            qm, km, vm, om, gcol, grow3, head_g,
            jnp.zeros((batch, H_M, DH_M, DH_M), F32), jnp.zeros((batch, H_M, DH_M), F32),
            jnp.zeros((batch, H_M), F32), batch=batch, seq=seq, chunk=chunk, out_dtype=BF16)
        ccol, crow = _fox_cumsum(gcol, grow, batch=batch, seq=seq)
        hf = _fox_prompt(qf, ktb, vtb, ccol, crow, batch=batch, seq=seq)
        hm_p, hf_p = hm, hf
        pk.append(kt.reshape(batch, H_F, DH_F, seq).transpose(0, 3, 1, 2))
        pv.append(vt.reshape(batch, H_F, DH_F, seq).transpose(0, 3, 1, 2))
        plf.append(grow3[:, gate0:, :].transpose(0, 2, 1))
        pc.append(c_p); pn.append(n_p); pm.append(m_p)

        qm, km, vm, om, qf, k_new, v_new, gcol, grow = _project(
            xs, g_mix, wt_main, wt_kv, wt_g, bcol, brow, batch=dec_batch, seq=dec_seq, kv_transposed=False,
            act_dtype=F32)
        grow3 = grow.reshape(N_GATE_ROWS, dec_batch, dec_seq).transpose(1, 0, 2)
        hm, c_s, n_s, m_s = _mlstm(
            qm, km, vm, om, gcol, grow3, head_g,
            state_mlstm_C[l].astype(F32), state_mlstm_n[l].astype(F32), state_mlstm_m[l].astype(F32),
            batch=dec_batch, seq=dec_seq, chunk=dec_seq, out_dtype=F32)
        k_pool = cache_fox_k[l].transpose(0, 2, 3, 1).reshape(n_pool, W_HEADS, page_size)
        v_pool = cache_fox_v[l].transpose(0, 2, 3, 1).reshape(n_pool, W_HEADS, page_size)
        lf_pool = cache_fox_logf[l].transpose(0, 2, 1)
        page_bias = _page_bias(page_table, lf_pool)
        xp, hf = _merge_ffn_and_fox_sample(
            xp, hm_p, hf_p, wo, g_ffn, wg, wu, wd, g_final, page_table, qf, grow3, k_new, v_new, page_bias,
            k_pool, v_pool, final_norm=(l == depth - 1))
        xs = _merge_ffn(xs, hm, hf, wo, g_ffn, wg, wu, wd, g_final, final_norm=(l == depth - 1))
        sk.append(k_new.reshape(dec_batch, dec_seq, H_F, DH_F))
        sv.append(v_new.reshape(dec_batch, dec_seq, H_F, DH_F))
        slf.append(grow3[:, gate0:, :].transpose(0, 2, 1))
        sc.append(c_s); sn.append(n_s); sm.append(m_s)

    st = lambda a, ref: jnp.stack(a, axis=0).astype(ref.dtype)
    return (xp.reshape(batch, seq, d_model), xs.reshape(dec_batch, dec_seq, d_model),
            st(pk, cache_fox_k), st(pv, cache_fox_v), st(plf, cache_fox_logf),
            st(pc, state_mlstm_C), st(pn, state_mlstm_n), st(pm, state_mlstm_m),
            st(sk, cache_fox_k), st(sv, cache_fox_v), st(slf, cache_fox_logf),
            st(sc, state_mlstm_C), st(sn, state_mlstm_n), st(sm, state_mlstm_m))
```

```python
import functools

import jax
import jax.numpy as jnp
from jax import lax
from jax.experimental import pallas as pl
from jax.experimental.pallas import tpu as pltpu

F32 = jnp.float32
BF16 = jnp.bfloat16
HI = lax.Precision.HIGHEST
NT = (((1,), (1,)), ((), ()))
TN = (((0,), (0,)), ((), ()))

EPS = 1e-6
LOG2E = 1.4426950408889634
H_M = 4
DH_M = 128
H_F = 8
DH_F = 64
W_HEADS = 512
N_GATE_ROWS = 16
LANES = 128
MIB = 1024 * 1024

PROJ_ROWS = 512
FFN_ROWS = 512
FFN_CHUNK = 256
MLSTM_BATCH = 8
MLSTM_INTERLEAVE = 4
FOX_BLOCK = 512
PAGES_PER_GROUP = 8
PAGE_SLOTS = 4


def _rms(x, g):
    return x * lax.rsqrt(jnp.mean(x * x, axis=-1, keepdims=True) + EPS) * g


def _log_sigmoid(x):
    return jnp.minimum(x, 0.0) - jnp.log1p(jnp.exp(-jnp.abs(x)))


def _sigmoid(x):
    return 1.0 / (1.0 + jnp.exp(-x))


def _const_spec(shape):
    return pl.BlockSpec(shape, lambda *_: (0,) * len(shape), pipeline_mode=pl.Buffered(1))


def _proj_kernel(x_ref, g_ref, wt_ref, wkv_ref, wg_ref, bcol_ref, brow_ref, *outs, kv_transposed):
    if kv_transposed:
        qm_ref, km_ref, vm_ref, om_ref, qf_ref, kf_ref, vf_ref, kfb_ref, vfb_ref, gcol_ref, grow_ref = outs
    else:
        qm_ref, km_ref, vm_ref, om_ref, qf_ref, kf_ref, vf_ref, gcol_ref, grow_ref = outs
    h = _rms(x_ref[...], g_ref[...]).astype(BF16)
    rows = h.shape[0]

    def mm(i):
        w = wt_ref[i * W_HEADS:(i + 1) * W_HEADS, :]
        return lax.dot_general(h, w, NT, preferred_element_type=F32)

    def mm_t(w):
        return lax.dot_general(w, h, NT, preferred_element_type=F32)

    qm_ref[...] = mm(0).astype(qm_ref.dtype)
    km_ref[...] = (mm(1) * (DH_M ** -0.5)).astype(km_ref.dtype)
    if kv_transposed:
        vm_ref[0] = mm_t(wt_ref[2 * W_HEADS:3 * W_HEADS, :]).astype(vm_ref.dtype)
        om_ref[0] = mm_t(wt_ref[3 * W_HEADS:4 * W_HEADS, :]).astype(om_ref.dtype)
    else:
        vm_ref[...] = mm(2).astype(vm_ref.dtype)
        om_ref[...] = mm(3).astype(om_ref.dtype)
    qf_ref[...] = (mm(4) * (DH_F ** -0.5 * LOG2E)).astype(qf_ref.dtype)
    if kv_transposed:
        kt = lax.dot_general(wkv_ref[0:W_HEADS, :], h, NT, preferred_element_type=F32)
        kf_ref[0] = kt
        kfb_ref[0] = kt.astype(BF16)
        vt = lax.dot_general(wkv_ref[W_HEADS:2 * W_HEADS, :], h, NT, preferred_element_type=F32)
        vf_ref[0] = vt
        vfb_ref[0] = vt.astype(BF16)
    else:
        kf_ref[...] = lax.dot_general(h, wkv_ref[0:W_HEADS, :], NT, preferred_element_type=F32)
        vf_ref[...] = lax.dot_general(h, wkv_ref[W_HEADS:2 * W_HEADS, :], NT, preferred_element_type=F32)
    pre_c = lax.dot_general(h, wg_ref[...], NT, preferred_element_type=F32) + bcol_ref[...]
    lane = lax.broadcasted_iota(jnp.int32, (rows, LANES), 1)
    gcol_ref[...] = jnp.where(lane < H_M, pre_c, _log_sigmoid(pre_c))
    pre_r = lax.dot_general(wg_ref[0:N_GATE_ROWS, :], h, NT, preferred_element_type=F32) + brow_ref[...]
    row = lax.broadcasted_iota(jnp.int32, (N_GATE_ROWS, rows), 0)
    grow_ref[...] = jnp.where(row < H_M, pre_r, _log_sigmoid(pre_r))


def _project(x2d, g, wt_main, wt_kv, wt_g, bcol, brow, *, batch, seq, kv_transposed, act_dtype):
    rows_total, d_model = x2d.shape
    tm = min(PROJ_ROWS, rows_total)
    steps = rows_total // tm
    per_seq = max(seq // tm, 1)
    row_spec = lambda w: pl.BlockSpec((tm, w), lambda i: (i, 0))
    in_specs = [row_spec(d_model), _const_spec(g.shape), _const_spec(wt_main.shape), _const_spec(wt_kv.shape),
                _const_spec(wt_g.shape), _const_spec(bcol.shape), _const_spec(brow.shape)]
    act = jax.ShapeDtypeStruct((rows_total, W_HEADS), act_dtype)
    act32 = jax.ShapeDtypeStruct((rows_total, W_HEADS), F32)
    out_shape = [act, act, act, act32, act]
    out_specs = [row_spec(W_HEADS)] * 5
    if kv_transposed:
        kv_spec = pl.BlockSpec((1, W_HEADS, tm), lambda i: (i // per_seq, 0, i % per_seq))
        out_shape[2] = jax.ShapeDtypeStruct((batch, W_HEADS, seq), act_dtype)
        out_shape[3] = jax.ShapeDtypeStruct((batch, W_HEADS, seq), F32)
        out_specs[2] = out_specs[3] = kv_spec
        out_shape += [jax.ShapeDtypeStruct((batch, W_HEADS, seq), F32)] * 2
        out_shape += [jax.ShapeDtypeStruct((batch, W_HEADS, seq), BF16)] * 2
        out_specs += [kv_spec] * 4
    else:
        out_shape += [act32, act32]
        out_specs += [row_spec(W_HEADS)] * 2
    out_shape += [jax.ShapeDtypeStruct((rows_total, LANES), F32),
                  jax.ShapeDtypeStruct((N_GATE_ROWS, rows_total), F32)]
    out_specs += [row_spec(LANES), pl.BlockSpec((N_GATE_ROWS, tm), lambda i: (0, i))]
    return pl.pallas_call(
        functools.partial(_proj_kernel, kv_transposed=kv_transposed),
        grid=(steps,),
        in_specs=in_specs,
        out_specs=out_specs,
        out_shape=out_shape,
        compiler_params=pltpu.CompilerParams(dimension_semantics=("arbitrary",), vmem_limit_bytes=48 * MIB),
        name="proj",
    )(x2d, g, wt_main, wt_kv, wt_g, bcol, brow)


def _mlstm_kernel(q_ref, k_ref, v_ref, om_ref, gc_ref, gr_ref, hg_ref, c0_ref, n0_ref, m0_ref,
                  hm_ref, c_ref, n_ref, m_ref, *, bb, chunk):
    @pl.when(pl.program_id(1) == 0)
    def _():
        c_ref[...] = c0_ref[...]
        n_ref[...] = n0_ref[...]
        m_ref[...] = m0_ref[...]

    t_idx = lax.broadcasted_iota(jnp.int32, (chunk, chunk), 0)
    s_idx = lax.broadcasted_iota(jnp.int32, (chunk, chunk), 1)
    causal = s_idx <= t_idx
    tril = causal.astype(F32)
    triu = (t_idx <= s_idx).astype(F32)

    def per_batch(b, carry):
        gc = gc_ref[b]
        gr = gr_ref[b]
        bc = jnp.dot(tril, gc, precision=HI, preferred_element_type=F32)
        br = jnp.dot(gr, triu, precision=HI, preferred_element_type=F32)
        q = q_ref[b]
        k = k_ref[b]
        v = v_ref[b]
        om = om_ref[b]
        for h in range(H_M):
            sl = slice(h * DH_M, (h + 1) * DH_M)
            li_c = gc[:, h:h + 1]
            li_r = gr[h:h + 1, :]
            b_c = bc[:, H_M + h:H_M + h + 1]
            b_r = br[H_M + h:H_M + h + 1, :]
            m_prev = m_ref[pl.ds(b, 1), h:h + 1]
            dmat = jnp.where(causal, b_c - b_r + li_r, -jnp.inf)
            inter = m_prev + b_c
            m_t = jnp.maximum(inter, jnp.max(dmat, axis=-1, keepdims=True))
            qh = q[:, sl].astype(BF16)
            kh = k[:, sl].astype(BF16)
            vh = v[:, sl].astype(BF16)
            smat = lax.dot_general(qh, kh, NT, preferred_element_type=F32) * jnp.exp(dmat - m_t)
            w_inter = jnp.exp(inter - m_t)
            c_prev = c_ref[b, h]
            cq = lax.dot_general(qh, c_prev.astype(BF16), NT, preferred_element_type=F32)
            num = w_inter * cq + jnp.dot(smat.astype(BF16), vh, preferred_element_type=F32)
            n_prev = n_ref[b, h:h + 1, :]
            nq = jnp.sum(qh.astype(F32) * n_prev, axis=-1, keepdims=True)
            den = w_inter * nq + jnp.sum(smat, axis=-1, keepdims=True)
            hh = num / jnp.maximum(jnp.abs(den), jnp.exp(-m_t))
            hn = hh * lax.rsqrt(jnp.mean(hh * hh, axis=-1, keepdims=True) + EPS) * hg_ref[:, sl]
            hm_ref[b, :, sl] = (hn * _sigmoid(om[:, sl].astype(F32))).astype(hm_ref.dtype)
            b_last = b_c[chunk - 1:chunk, :]
            m_new = m_t[chunk - 1:chunk, :]
            decay = jnp.exp(m_prev + b_last - m_new)
            w_s = jnp.exp(li_c + b_last - b_c - m_new)
            vw = (vh.astype(F32) * w_s).astype(BF16)
            c_ref[b, h] = decay * c_prev + lax.dot_general(vw, kh, TN, preferred_element_type=F32)
            n_ref[b, h:h + 1, :] = decay * n_prev + jnp.sum(kh.astype(F32) * w_s, axis=0, keepdims=True)
            m_ref[pl.ds(b, 1), h:h + 1] = m_new
        return carry

    lax.fori_loop(0, bb, per_batch, 0)


def _mlstm(qm, km, vm, om, gcol, grow3, head_g, c0, n0, m0, *, batch, seq, chunk, out_dtype):
    bb = MLSTM_BATCH
    n_chunks = seq // chunk
    as3 = lambda a: a.reshape(batch, seq, a.shape[-1])
    tok = lambda w: pl.BlockSpec((bb, chunk, w), lambda g, c: (g, c, 0))
    state = lambda shape: pl.BlockSpec((bb,) + shape, lambda g, c: (g,) + (0,) * len(shape))
    in_specs = [tok(W_HEADS), tok(W_HEADS), tok(W_HEADS), tok(W_HEADS), tok(LANES),
                pl.BlockSpec((bb, N_GATE_ROWS, chunk), lambda g, c: (g, 0, c)),
                _const_spec(head_g.shape),
                state((H_M, DH_M, DH_M)), state((H_M, DH_M)), state((H_M,))]
    out_specs = [tok(W_HEADS), state((H_M, DH_M, DH_M)), state((H_M, DH_M)), state((H_M,))]
    out_shape = [jax.ShapeDtypeStruct((batch, seq, W_HEADS), out_dtype),
                 jax.ShapeDtypeStruct(c0.shape, F32), jax.ShapeDtypeStruct(n0.shape, F32),
                 jax.ShapeDtypeStruct(m0.shape, F32)]
    hm, c_new, n_new, m_new = pl.pallas_call(
        functools.partial(_mlstm_kernel, bb=bb, chunk=chunk),
        grid=(batch // bb, n_chunks),
        in_specs=in_specs,
        out_specs=out_specs,
        out_shape=out_shape,
        compiler_params=pltpu.CompilerParams(dimension_semantics=("arbitrary", "arbitrary"),
                                             vmem_limit_bytes=48 * MIB),
        name="mlstm",
    )(as3(qm), as3(km), as3(vm), as3(om), as3(gcol), grow3, head_g, c0, n0, m0)
    return hm.reshape(batch * seq, W_HEADS), c_new, n_new, m_new


def _mlstm_t_kernel(q_ref, k_ref, vt_ref, omt_ref, gc_ref, gr_ref, hg_ref, c0_ref, n0_ref, m0_ref,
                    hm_ref, c_ref, n_ref, m_ref, *, bb, chunk):
    @pl.when(pl.program_id(1) == 0)
    def _():
        c_ref[...] = c0_ref[...]
        n_ref[...] = n0_ref[...]
        m_ref[...] = m0_ref[...]

    s_idx = lax.broadcasted_iota(jnp.int32, (chunk, chunk), 0)
    t_idx = lax.broadcasted_iota(jnp.int32, (chunk, chunk), 1)
    causal = s_idx <= t_idx
    tril = (t_idx <= s_idx).astype(F32)
    triu = causal.astype(F32)

    def per_group(i, carry):
        bs = [i * MLSTM_INTERLEAVE + j for j in range(MLSTM_INTERLEAVE)]
        gc = [gc_ref[b] for b in bs]
        gr = [gr_ref[b] for b in bs]
        bc = [jnp.dot(tril, g, precision=HI, preferred_element_type=F32) for g in gc]
        br = [jnp.dot(g, triu, precision=HI, preferred_element_type=F32) for g in gr]
        n_all = [n_ref[b] for b in bs]
        m_all = [m_ref[pl.ds(b, 1), :] for b in bs]
        chains = [(j, h) for j in range(MLSTM_INTERLEAVE) for h in range(H_M)]
        ids = range(len(chains))
        sls = [slice(h * DH_M, (h + 1) * DH_M) for _, h in chains]
        c_prev = [c_ref[bs[j], h] for j, h in chains]
        n_prev = [n_all[j][h:h + 1, :] for j, h in chains]
        m_prev = [m_all[j][:, h:h + 1] for j, h in chains]
        qh = [q_ref[bs[j], :, sls[c]].astype(BF16) for c, (j, _) in enumerate(chains)]
        kh = [k_ref[bs[j], :, sls[c]].astype(BF16) for c, (j, _) in enumerate(chains)]
        vth = [vt_ref[bs[j], sls[c], :].astype(BF16) for c, (j, _) in enumerate(chains)]
        qk = [lax.dot_general(kh[c], qh[c], NT, preferred_element_type=F32) for c in ids]
        cq = [lax.dot_general(c_prev[c].astype(BF16), qh[c], NT, preferred_element_type=F32) for c in ids]
        nq = [lax.dot_general(jnp.broadcast_to(n_prev[c], (8, DH_M)).astype(BF16), qh[c], NT,
                              preferred_element_type=F32)[0:1, :] for c in ids]
        li_r = [gr[j][h:h + 1, :] for j, h in chains]
        b_r = [br[j][H_M + h:H_M + h + 1, :] for j, h in chains]
        col = [gc[j][:, h:h + 1] - bc[j][:, H_M + h:H_M + h + 1] for j, h in chains]
        dmat = [jnp.where(causal, b_r[c] + col[c], -jnp.inf) for c in ids]
        inter = [m_prev[c] + b_r[c] for c in ids]
        m_t = [jnp.maximum(inter[c], jnp.max(dmat[c], axis=0, keepdims=True)) for c in ids]
        smat = [qk[c] * jnp.exp(dmat[c] - m_t[c]) for c in ids]
        w_inter = [jnp.exp(inter[c] - m_t[c]) for c in ids]
        sv = [jnp.dot(vth[c], smat[c].astype(BF16), preferred_element_type=F32) for c in ids]
        outs, vw, w_s_c, decay, m_new = [], [], [], [], []
        for c, (j, h) in enumerate(chains):
            num = w_inter[c] * cq[c] + sv[c]
            den = w_inter[c] * nq[c] + jnp.sum(smat[c], axis=0, keepdims=True)
            hh = num * (1.0 / jnp.maximum(jnp.abs(den), jnp.exp(-m_t[c])))
            hn = hh * lax.rsqrt(jnp.mean(hh * hh, axis=0, keepdims=True) + EPS) * hg_ref[sls[c], :]
            gate = _sigmoid(omt_ref[bs[j], sls[c], :].astype(F32))
            outs.append((hn * gate).T.astype(hm_ref.dtype))
            b_last = b_r[c][:, chunk - 1:chunk]
            m_new.append(m_t[c][:, chunk - 1:chunk])
            decay.append(jnp.exp(m_prev[c] + b_last - m_new[c]))
            w_s_r = jnp.exp(li_r[c] + b_last - b_r[c] - m_new[c])
            w_s_c.append(jnp.exp(col[c] + b_last - m_new[c]))
            vw.append((vth[c].astype(F32) * w_s_r).astype(BF16))
        for c, (j, h) in enumerate(chains):
            c_new = decay[c] * c_prev[c] + jnp.dot(vw[c], kh[c], preferred_element_type=F32)
            n_new = decay[c] * n_prev[c] + jnp.sum(kh[c].astype(F32) * w_s_c[c], axis=0, keepdims=True)
            hm_ref[bs[j], :, sls[c]] = outs[c]
            c_ref[bs[j], h] = c_new
            n_ref[bs[j], h:h + 1, :] = n_new
            m_ref[pl.ds(bs[j], 1), h:h + 1] = m_new[c]
        return carry

    lax.fori_loop(0, bb // MLSTM_INTERLEAVE, per_group, 0)


def _mlstm_prompt(qm, km, vmt, omt, gcol, grow3, head_g, c0, n0, m0, *, batch, seq, chunk, out_dtype):
    bb = MLSTM_BATCH
    n_chunks = seq // chunk
    as3 = lambda a: a.reshape(batch, seq, a.shape[-1])
    tok = lambda w: pl.BlockSpec((bb, chunk, w), lambda g, c: (g, c, 0))
    tok_t = pl.BlockSpec((bb, W_HEADS, chunk), lambda g, c: (g, 0, c))
    state = lambda shape: pl.BlockSpec((bb,) + shape, lambda g, c: (g,) + (0,) * len(shape))
    hg_col = head_g.reshape(W_HEADS, 1)
    in_specs = [tok(W_HEADS), tok(W_HEADS), tok_t, tok_t, tok(LANES),
                pl.BlockSpec((bb, N_GATE_ROWS, chunk), lambda g, c: (g, 0, c)),
                _const_spec(hg_col.shape),
                state((H_M, DH_M, DH_M)), state((H_M, DH_M)), state((H_M,))]
    out_specs = [tok(W_HEADS), state((H_M, DH_M, DH_M)), state((H_M, DH_M)), state((H_M,))]
    out_shape = [jax.ShapeDtypeStruct((batch, seq, W_HEADS), out_dtype),
                 jax.ShapeDtypeStruct(c0.shape, F32), jax.ShapeDtypeStruct(n0.shape, F32),
                 jax.ShapeDtypeStruct(m0.shape, F32)]
    hm, c_new, n_new, m_new = pl.pallas_call(
        functools.partial(_mlstm_t_kernel, bb=bb, chunk=chunk),
        grid=(batch // bb, n_chunks),
        in_specs=in_specs,
        out_specs=out_specs,
        out_shape=out_shape,
        compiler_params=pltpu.CompilerParams(dimension_semantics=("arbitrary", "arbitrary"),
                                             vmem_limit_bytes=48 * MIB),
        name="mlstm_prompt",
    )(as3(qm), as3(km), vmt, omt, as3(gcol), grow3, hg_col, c0, n0, m0)
    return hm.reshape(batch * seq, W_HEADS), c_new, n_new, m_new


def _fox_cumsum_kernel(gc_ref, gr_ref, cc_ref, cr_ref):
    seq = gc_ref.shape[1]
    t_idx = lax.broadcasted_iota(jnp.int32, (LANES, LANES), 0)
    s_idx = lax.broadcasted_iota(jnp.int32, (LANES, LANES), 1)
    tril = (s_idx <= t_idx).astype(F32)
    triu = (t_idx <= s_idx).astype(F32)
    carry_c = jnp.zeros((1, LANES), F32)
    carry_r = jnp.zeros((N_GATE_ROWS, 1), F32)
    for j in range(seq // LANES):
        blk = slice(j * LANES, (j + 1) * LANES)
        cb = jnp.dot(tril, gc_ref[0, blk, :], precision=HI, preferred_element_type=F32) + carry_c
        cc_ref[0, blk, :] = cb * LOG2E
        carry_c = cb[LANES - 1:LANES, :]
        rb = jnp.dot(gr_ref[:, blk], triu, precision=HI, preferred_element_type=F32) + carry_r
        cr_ref[:, blk] = rb * LOG2E
        carry_r = rb[:, LANES - 1:LANES]


def _fox_cumsum(gcol, grow, *, batch, seq):
    return pl.pallas_call(
        _fox_cumsum_kernel,
        grid=(batch,),
        in_specs=[pl.BlockSpec((1, seq, LANES), lambda b: (b, 0, 0)),
                  pl.BlockSpec((N_GATE_ROWS, seq), lambda b: (0, b))],
        out_specs=[pl.BlockSpec((1, seq, LANES), lambda b: (b, 0, 0)),
                   pl.BlockSpec((N_GATE_ROWS, seq), lambda b: (0, b))],
        out_shape=[jax.ShapeDtypeStruct((batch, seq, LANES), F32),
                   jax.ShapeDtypeStruct((N_GATE_ROWS, batch * seq), F32)],
        compiler_params=pltpu.CompilerParams(dimension_semantics=("arbitrary",)),
        name="fox_cumsum",
    )(gcol.reshape(batch, seq, LANES), grow)


def _fox_prompt_kernel(q_ref, kt_ref, vt_ref, cc_ref, cr_ref, o_ref, qs_ref, cq_ref, m_ref, l_ref, acc_ref, *, blk):
    i = pl.program_id(1)
    n_pairs = H_F // 2
    lane_tiles = blk // LANES
    lane = lax.broadcasted_iota(jnp.int32, (blk, LANES), 1)
    low_half = lane < DH_F
    r_idx = lax.broadcasted_iota(jnp.int32, (blk, blk), 0)
    c_idx = lax.broadcasted_iota(jnp.int32, (blk, blk), 1)
    diag_mask = jnp.concatenate([c_idx <= r_idx] * 2, axis=0)
    gate0 = N_GATE_ROWS - H_F

    for pair in range(n_pairs):
        rows = slice(pair * LANES, (pair + 1) * LANES)
        q_pair = q_ref[:, rows]
        zero = jnp.zeros_like(q_pair)
        qs_ref[pair, 0:blk, :] = jnp.where(low_half, q_pair, zero)
        qs_ref[pair, blk:2 * blk, :] = jnp.where(low_half, zero, q_pair)
        for e in range(2):
            col = gate0 + 2 * pair + e
            cq_ref[pair, e * blk:(e + 1) * blk, :] = jnp.broadcast_to(cc_ref[:, col:col + 1], (blk, LANES))
    m_ref[...] = jnp.full(m_ref.shape, -jnp.inf, F32)
    l_ref[...] = jnp.zeros(l_ref.shape, F32)
    acc_ref[...] = jnp.zeros(acc_ref.shape, F32)

    def step(j, masked):
        cols = pl.ds(pl.multiple_of(j * blk, blk), blk)
        for pair in range(n_pairs):
            rows = slice(pair * LANES, (pair + 1) * LANES)
            kj = kt_ref[0, rows, cols]
            vj = vt_ref[0, rows, cols]
            s = jnp.dot(qs_ref[pair], kj, preferred_element_type=F32)
            ck = jnp.concatenate(
                [jnp.broadcast_to(cr_ref[gate0 + 2 * pair + e:gate0 + 2 * pair + e + 1, cols], (blk, blk))
                 for e in range(2)], axis=0)
            s = s + (jnp.concatenate([cq_ref[pair]] * lane_tiles, axis=1) - ck)
            if masked:
                s = jnp.where(diag_mask, s, -jnp.inf)
            m_old = m_ref[pair]
            m_new = jnp.maximum(m_old, jnp.max(s, axis=-1, keepdims=True))
            p = jnp.exp2(s - jnp.concatenate([m_new] * lane_tiles, axis=1))
            alpha = jnp.exp2(m_old - m_new)
            p_lanes = p[:, 0:LANES]
            for t in range(1, lane_tiles):
                p_lanes = p_lanes + p[:, t * LANES:(t + 1) * LANES]
            l_ref[pair] = alpha * l_ref[pair] + p_lanes
            pv = lax.dot_general(p.astype(BF16), vj, NT, preferred_element_type=F32)
            acc_ref[pair] = alpha * acc_ref[pair] + pv
            m_ref[pair] = m_new

    def loop_body(j, carry):
        step(j, False)
        return carry

    lax.fori_loop(0, i, loop_body, 0)
    step(i, True)
    for pair in range(n_pairs):
        rows = slice(pair * LANES, (pair + 1) * LANES)
        out = acc_ref[pair] / jnp.sum(l_ref[pair], axis=-1, keepdims=True)
        o_ref[:, rows] = jnp.where(low_half, out[0:blk], out[blk:2 * blk]).astype(o_ref.dtype)


def _fox_prompt(qf, ktb, vtb, ccol, crow, *, batch, seq):
    blk = FOX_BLOCK
    nq = seq // blk
    return pl.pallas_call(
        functools.partial(_fox_prompt_kernel, blk=blk),
        grid=(batch, nq),
        in_specs=[pl.BlockSpec((blk, W_HEADS), lambda b, i: (b * nq + i, 0)),
                  pl.BlockSpec((1, W_HEADS, seq), lambda b, i: (b, 0, 0)),
                  pl.BlockSpec((1, W_HEADS, seq), lambda b, i: (b, 0, 0)),
                  pl.BlockSpec((blk, LANES), lambda b, i: (b * nq + i, 0)),
                  pl.BlockSpec((N_GATE_ROWS, seq), lambda b, i: (0, b))],
        out_specs=pl.BlockSpec((blk, W_HEADS), lambda b, i: (b * nq + i, 0)),
        out_shape=jax.ShapeDtypeStruct((batch * seq, W_HEADS), BF16),
        scratch_shapes=[pltpu.VMEM((H_F // 2, 2 * blk, LANES), BF16)] + [pltpu.VMEM((H_F // 2, 2 * blk, LANES), F32)] * 4,
        compiler_params=pltpu.CompilerParams(dimension_semantics=("arbitrary", "arbitrary"),
                                             vmem_limit_bytes=48 * MIB),
        name="fox_prompt",
    )(qf, ktb, vtb, ccol.reshape(batch * seq, LANES), crow)


def _page_bias_kernel(pt_ref, lf_hbm, o_ref, buf, sem):
    b = pl.program_id(0)
    nb = pl.num_programs(0)
    n_pages = buf.shape[1]
    slot = b % 2

    def page_copy(bi, p, sl):
        return pltpu.make_async_copy(lf_hbm.at[pt_ref[bi, p]], buf.at[sl, p], sem.at[sl])

    def start_all(bi, sl):
        def body(p, c):
            page_copy(bi, p, sl).start()
            return c
        lax.fori_loop(0, n_pages, body, 0)

    @pl.when(b == 0)
    def _():
        start_all(0, 0)

    @pl.when(b + 1 < nb)
    def _():
        start_all(b + 1, 1 - slot)

    def wait_body(p, c):
        page_copy(b, p, slot).wait()
        return c
    lax.fori_loop(0, n_pages, wait_body, 0)

    x = buf[slot].reshape(n_pages * H_F, LANES)
    t_idx = lax.broadcasted_iota(jnp.int32, (LANES, LANES), 0)
    s_idx = lax.broadcasted_iota(jnp.int32, (LANES, LANES), 1)
    later = (t_idx > s_idx).astype(F32)
    within = jnp.dot(x, later, precision=HI, preferred_element_type=F32)
    total = jnp.sum(x, axis=-1, keepdims=True)
    run = jnp.zeros((H_F, LANES), F32)
    for p in range(n_pages - 1, -1, -1):
        rows = slice(p * H_F, (p + 1) * H_F)
        o_ref[0, p] = (within[rows] + run) * (-LOG2E)
        run = run + total[rows]


def _page_bias(page_table, lf_pool):
    batch, n_pages = page_table.shape
    return pl.pallas_call(
        _page_bias_kernel,
        grid_spec=pltpu.PrefetchScalarGridSpec(
            num_scalar_prefetch=1,
            grid=(batch,),
            in_specs=[pl.BlockSpec(memory_space=pl.ANY)],
            out_specs=pl.BlockSpec((1, n_pages, H_F, LANES), lambda b, pt: (b, 0, 0, 0)),
            scratch_shapes=[pltpu.VMEM((2, n_pages, H_F, LANES), F32), pltpu.SemaphoreType.DMA((2,))],
        ),
        out_shape=jax.ShapeDtypeStruct((batch, n_pages, H_F, LANES), F32),
        compiler_params=pltpu.CompilerParams(dimension_semantics=("arbitrary",)),
        name="page_bias",
    )(page_table, lf_pool)


def _paged_attention(pt_ref, q_ref, gr_ref, kn_ref, vn_ref, cp_ref, k_hbm, v_hbm, o_ref,
                     kbuf, vbuf, sem, acc_ref, *, n_groups, phases):
    b = pl.program_id(0)
    nb = pl.num_programs(0)
    group = PAGES_PER_GROUP
    t_new = q_ref.shape[0]
    n_rows = t_new * H_F

    def page_copies(bi, g, sl):
        cps = []
        for j in range(group):
            page = pt_ref[bi, g * group + j]
            cps.append(pltpu.make_async_copy(k_hbm.at[page], kbuf.at[sl, j], sem.at[sl, 0]))
            cps.append(pltpu.make_async_copy(v_hbm.at[page], vbuf.at[sl, j], sem.at[sl, 1]))
        return cps

    def start_group(bi, g, sl):
        for cp in page_copies(bi, g, sl):
            cp.start()

    @pl.when(b == 0)
    def _():
        for a in range(PAGE_SLOTS - 1):
            start_group(0, a, a)

    sub = lax.broadcasted_iota(jnp.int32, (H_F, W_HEADS), 0)
    lane = lax.broadcasted_iota(jnp.int32, (H_F, W_HEADS), 1)
    own_head = (lane // DH_F) == sub
    q = q_ref[...].astype(F32)
    qbd = jnp.concatenate(
        [jnp.where(own_head, jnp.broadcast_to(q[t:t + 1, :], (H_F, W_HEADS)), 0.0) for t in range(t_new)],
        axis=0).astype(BF16)

    gate0 = N_GATE_ROWS - H_F
    lf_new = gr_ref[0, gate0:N_GATE_ROWS, :]
    a_idx = lax.broadcasted_iota(jnp.int32, (t_new, t_new), 0)
    b_idx = lax.broadcasted_iota(jnp.int32, (t_new, t_new), 1)
    c_new = jnp.dot(lf_new, (a_idx <= b_idx).astype(F32), precision=HI, preferred_element_type=F32) * LOG2E
    cq = jnp.concatenate([c_new[:, t:t + 1] for t in range(t_new)], axis=0)

    acc_ref[...] = jnp.zeros_like(acc_ref)

    def body(g, carry, side_work):
        m_i, l_i = carry
        slot = g % PAGE_SLOTS
        ahead = g + (PAGE_SLOTS - 1)

        @pl.when(ahead < n_groups)
        def _():
            start_group(b, ahead, ahead % PAGE_SLOTS)

        @pl.when(jnp.logical_and(ahead >= n_groups, b + 1 < nb))
        def _():
            start_group(b + 1, ahead - n_groups, ahead % PAGE_SLOTS)

        for cp in page_copies(b, g, slot):
            cp.wait()
        side = side_work(g)

        kcat = jnp.concatenate([kbuf[slot, j].astype(BF16) for j in range(group)], axis=1)
        s = jnp.dot(qbd, kcat, preferred_element_type=F32)
        next(side, None)
        first = pl.multiple_of(g * group, group)
        cpg = cp_ref[0, pl.ds(first, group)]
        bias = jnp.concatenate(
            [jnp.broadcast_to(cpg[j][None], (t_new, H_F, LANES)).reshape(n_rows, LANES) for j in range(group)],
            axis=1)
        s = s + (cq - bias)
        m_new = jnp.maximum(m_i, jnp.max(s, axis=-1, keepdims=True))
        p = jnp.exp2(s - m_new)
        alpha = jnp.exp2(m_i - m_new)
        l_new = alpha * l_i + jnp.sum(p, axis=-1, keepdims=True)
        next(side, None)
        vcat = jnp.concatenate([vbuf[slot, j].T.astype(BF16) for j in range(group)], axis=0)
        pv = jnp.dot(p.astype(BF16), vcat, preferred_element_type=F32)
        next(side, None)
        acc_ref[...] = alpha * acc_ref[...] + pv
        return m_new, l_new

    carry = (jnp.full((n_rows, 1), -jnp.inf, F32), jnp.zeros((n_rows, 1), F32))
    start = 0
    for stop, side_work in phases:
        carry = lax.fori_loop(start, stop, functools.partial(body, side_work=side_work), carry)
        start = stop
    assert start == n_groups
    m_i, l_i = carry

    kn = kn_ref[...].astype(BF16)
    vn = vn_ref[...].astype(BF16)
    s = lax.dot_general(qbd, kn, NT, preferred_element_type=F32)
    ck = jnp.broadcast_to(c_new[None], (t_new, H_F, t_new)).reshape(n_rows, t_new)
    s = s + (cq - ck)
    r_idx = lax.broadcasted_iota(jnp.int32, (n_rows, t_new), 0)
    k_idx = lax.broadcasted_iota(jnp.int32, (n_rows, t_new), 1)
    s = jnp.where(k_idx <= r_idx // H_F, s, -jnp.inf)
    m_new = jnp.maximum(m_i, jnp.max(s, axis=-1, keepdims=True))
    p = jnp.exp2(s - m_new)
    alpha = jnp.exp2(m_i - m_new)
    l_fin = alpha * l_i + jnp.sum(p, axis=-1, keepdims=True)
    acc = alpha * acc_ref[...] + jnp.dot(p.astype(BF16), vn, preferred_element_type=F32)
    out = acc / l_fin
    o_ref[...] = jnp.concatenate(
        [jnp.sum(jnp.where(own_head, out[t * H_F:(t + 1) * H_F, :], 0.0), axis=0, keepdims=True)
         for t in range(t_new)], axis=0).astype(o_ref.dtype)


def _merge_residual(x_ref, hm_ref, hf_ref, wo_ref, g2_ref):
    x1 = (x_ref[...]
          + jnp.dot(hm_ref[...].astype(BF16), wo_ref[0:W_HEADS, :], preferred_element_type=F32)
          + jnp.dot(hf_ref[...].astype(BF16), wo_ref[W_HEADS:2 * W_HEADS, :], preferred_element_type=F32))
    return x1, _rms(x1, g2_ref[...]).astype(BF16)


def _ffn_sample_kernel(pt_ref, x_ref, hm_ref, hf_ref, wo_ref, g2_ref, wg_ref, wu_ref, wd_ref, g3_ref,
                       q_ref, gr_ref, kn_ref, vn_ref, cp_ref, k_hbm, v_hbm, y_ref, o_ref,
                       h_ref, x2_ref, kbuf, vbuf, sem, acc_ref, *, n_groups, final_norm):
    x1, h = _merge_residual(x_ref, hm_ref, hf_ref, wo_ref, g2_ref)
    h_ref[...] = h
    x2_ref[...] = x1
    ffn_chunks = wg_ref.shape[1] // FFN_CHUNK
    assert ffn_chunks <= n_groups

    def ffn_chunk(c):
        cols = pl.ds(pl.multiple_of(c * FFN_CHUNK, FFN_CHUNK), FFN_CHUNK)
        hh = h_ref[...]
        gate = jnp.dot(hh, wg_ref[:, cols], preferred_element_type=F32)
        yield
        up = jnp.dot(hh, wu_ref[:, cols], preferred_element_type=F32)
        act = (gate * _sigmoid(gate) * up).astype(BF16)
        yield
        x2_ref[...] += jnp.dot(act, wd_ref[cols, :], preferred_element_type=F32)
        yield

    phases = [(ffn_chunks, ffn_chunk), (n_groups, lambda g: iter(()))]
    _paged_attention(pt_ref, q_ref, gr_ref, kn_ref, vn_ref, cp_ref, k_hbm, v_hbm, o_ref, kbuf, vbuf, sem, acc_ref,
                     n_groups=n_groups, phases=phases)
    x2 = x2_ref[...]
    y_ref[...] = _rms(x2, g3_ref[...]) if final_norm else x2


def _merge_ffn_and_fox_sample(x2d, hm, hf, wo, g2, wg, wu, wd, g3, page_table, qf, grow3, k_new, v_new, page_bias,
                              k_pool, v_pool, *, final_norm):
    rows_total, d_model = x2d.shape
    batch, n_pages = page_table.shape
    t_new = qf.shape[0] // batch
    n_groups = n_pages // PAGES_PER_GROUP
    tm = rows_total // batch
    d_ff = wg.shape[1]
    assert n_pages % PAGES_PER_GROUP == 0 and n_groups % PAGE_SLOTS == 0
    assert rows_total % batch == 0 and tm % 16 == 0 and d_ff % FFN_CHUNK == 0 and W_HEADS % FFN_CHUNK == 0
    page_rows, page_len = k_pool.shape[1], k_pool.shape[2]
    row = lambda w: pl.BlockSpec((tm, w), lambda b, pt: (b, 0))
    tok = lambda w: pl.BlockSpec((t_new, w), lambda b, pt: (b, 0))
    return pl.pallas_call(
        functools.partial(_ffn_sample_kernel, n_groups=n_groups, final_norm=final_norm),
        grid_spec=pltpu.PrefetchScalarGridSpec(
            num_scalar_prefetch=1,
            grid=(batch,),
            in_specs=[row(d_model), row(W_HEADS), row(W_HEADS), _const_spec(wo.shape), _const_spec(g2.shape),
                      _const_spec(wg.shape), _const_spec(wu.shape), _const_spec(wd.shape), _const_spec(g3.shape),
                      tok(W_HEADS),
                      pl.BlockSpec((1, N_GATE_ROWS, t_new), lambda b, pt: (b, 0, 0)),
                      tok(W_HEADS), tok(W_HEADS),
                      pl.BlockSpec((1, n_pages, H_F, LANES), lambda b, pt: (b, 0, 0, 0)),
                      pl.BlockSpec(memory_space=pl.ANY), pl.BlockSpec(memory_space=pl.ANY)],
            out_specs=[row(d_model), tok(W_HEADS)],
            scratch_shapes=[pltpu.VMEM((tm, d_model), BF16),
                            pltpu.VMEM((tm, d_model), F32),
                            pltpu.VMEM((PAGE_SLOTS, PAGES_PER_GROUP, page_rows, page_len), F32),
                            pltpu.VMEM((PAGE_SLOTS, PAGES_PER_GROUP, page_rows, page_len), F32),
                            pltpu.SemaphoreType.DMA((PAGE_SLOTS, 2)),
                            pltpu.VMEM((t_new * H_F, W_HEADS), F32)],
        ),
        out_shape=[jax.ShapeDtypeStruct((rows_total, d_model), F32),
                   jax.ShapeDtypeStruct((batch * t_new, W_HEADS), F32)],
        compiler_params=pltpu.CompilerParams(dimension_semantics=("arbitrary",), vmem_limit_bytes=58 * MIB),
        name="ffn_and_fox_sample",
    )(page_table, x2d, hm, hf, wo, g2, wg, wu, wd, g3, qf, grow3, k_new, v_new, page_bias, k_pool, v_pool)


def _ffn_kernel(x_ref, hm_ref, hf_ref, wo_ref, g2_ref, wg_ref, wu_ref, wd_ref, g3_ref, y_ref, *, final_norm):
    x1, h = _merge_residual(x_ref, hm_ref, hf_ref, wo_ref, g2_ref)
    gate = jnp.dot(h, wg_ref[...], preferred_element_type=F32)
    up = jnp.dot(h, wu_ref[...], preferred_element_type=F32)
    act = (gate * _sigmoid(gate) * up).astype(BF16)
    x2 = x1 + jnp.dot(act, wd_ref[...], preferred_element_type=F32)
    y_ref[...] = _rms(x2, g3_ref[...]) if final_norm else x2


def _merge_ffn(x2d, hm, hf, wo, g2, wg, wu, wd, g3, *, final_norm):
    rows_total, d_model = x2d.shape
    tm = min(FFN_ROWS, rows_total)
    row_spec = lambda w: pl.BlockSpec((tm, w), lambda i: (i, 0))
    return pl.pallas_call(
        functools.partial(_ffn_kernel, final_norm=final_norm),
        grid=(rows_total // tm,),
        in_specs=[row_spec(d_model), row_spec(W_HEADS), row_spec(W_HEADS), _const_spec(wo.shape),
                  _const_spec(g2.shape), _const_spec(wg.shape), _const_spec(wu.shape), _const_spec(wd.shape),
                  _const_spec(g3.shape)],
        out_specs=row_spec(d_model),
        out_shape=jax.ShapeDtypeStruct((rows_total, d_model), F32),
        compiler_params=pltpu.CompilerParams(dimension_semantics=("arbitrary",), vmem_limit_bytes=56 * MIB),
        name="merge_ffn",
    )(x2d, hm, hf, wo, g2, wg, wu, wd, g3)


def kernel(x_prompt, x_sample, cache_fox_k, cache_fox_v, cache_fox_logf, page_table, state_mlstm_C,
           state_mlstm_n, state_mlstm_m, norm_mix_g, w_in, b_m_igate, b_m_fgate, b_f_fgate, mlstm_head_g,
           w_out, norm_ffn_g, w_gate, w_up, w_down, norm_final_g):
    depth = w_in.shape[0]
    batch, seq, d_model = x_prompt.shape
    dec_batch, dec_seq, _ = x_sample.shape
    n_pool, page_size = cache_fox_k.shape[1], cache_fox_k.shape[2]
    xp = x_prompt.reshape(batch * seq, d_model)
    xs = x_sample.reshape(dec_batch * dec_seq, d_model)
    g_final = norm_final_g.reshape(1, d_model)
    pk, pv, plf, pc, pn, pm = [], [], [], [], [], []
    sk, sv, slf, sc, sn, sm = [], [], [], [], [], []
    gate0 = N_GATE_ROWS - H_F
    o_gm = 4 * W_HEADS
    o_qf = o_gm + 2 * H_M
    o_gf = o_qf + 3 * W_HEADS
    for l in range(depth):
        wt = jnp.swapaxes(w_in[l], 0, 1)
        wt_main = jnp.concatenate([wt[0:o_gm], wt[o_qf:o_qf + W_HEADS]], axis=0).astype(BF16)
        wt_kv = wt[o_qf + W_HEADS:o_gf].astype(BF16)
        wt_g = jnp.concatenate([wt[o_gm:o_qf], wt[o_gf:o_gf + H_F],
                                jnp.zeros((LANES - N_GATE_ROWS, d_model), F32)], axis=0).astype(BF16)
        bias = jnp.concatenate([b_m_igate[l], b_m_fgate[l], b_f_fgate[l],
                                jnp.zeros((LANES - N_GATE_ROWS,), F32)]).astype(F32)
        bcol = bias.reshape(1, LANES)
        brow = bias[:N_GATE_ROWS].reshape(N_GATE_ROWS, 1)
        g_mix = norm_mix_g[l].reshape(1, d_model)
        g_ffn = norm_ffn_g[l].reshape(1, d_model)
        head_g = mlstm_head_g[l].reshape(1, W_HEADS)
        wo = w_out[l].astype(BF16)
        wg = w_gate[l].astype(BF16)
        wu = w_up[l].astype(BF16)
        wd = w_down[l].astype(BF16)

        qm, km, vmt, omt, qf, kt, vt, ktb, vtb, gcol, grow = _project(
            xp, g_mix, wt_main, wt_kv, wt_g, bcol, brow, batch=batch, seq=seq, kv_transposed=True, act_dtype=BF16)
        chunk = min(LANES, seq)
        grow3 = grow.reshape(N_GATE_ROWS, batch, seq).transpose(1, 0, 2)
        hm, c_p, n_p, m_p = _mlstm_prompt(
            qm, km, vmt, omt, gcol, grow3, head_g,
            jnp.zeros((batch, H_M, DH_M, DH_M), F32), jnp.zeros((batch, H_M, DH_M), F32),
            jnp.zeros((batch, H_M), F32), batch=batch, seq=seq, chunk=chunk, out_dtype=BF16)
        ccol, crow = _fox_cumsum(gcol, grow, batch=batch, seq=seq)
        hf = _fox_prompt(qf, ktb, vtb, ccol, crow, batch=batch, seq=seq)
        hm_p, hf_p = hm, hf
        pk.append(kt.reshape(batch, H_F, DH_F, seq).transpose(0, 3, 1, 2))
        pv.append(vt.reshape(batch, H_F, DH_F, seq).transpose(0, 3, 1, 2))
        plf.append(grow3[:, gate0:, :].transpose(0, 2, 1))
        pc.append(c_p); pn.append(n_p); pm.append(m_p)

        qm, km, vm, om, qf, k_new, v_new, gcol, grow = _project(
            xs, g_mix, wt_main, wt_kv, wt_g, bcol, brow, batch=dec_batch, seq=dec_seq, kv_transposed=False,
            act_dtype=F32)
        grow3 = grow.reshape(N_GATE_ROWS, dec_batch, dec_seq).transpose(1, 0, 2)
        hm, c_s, n_s, m_s = _mlstm(
            qm, km, vm, om, gcol, grow3, head_g,
            state_mlstm_C[l].astype(F32), state_mlstm_n[l].astype(F32), state_mlstm_m[l].astype(F32),
            batch=dec_batch, seq=dec_seq, chunk=dec_seq, out_dtype=F32)
        k_pool = cache_fox_k[l].transpose(0, 2, 3, 1).reshape(n_pool, W_HEADS, page_size)
        v_pool = cache_fox_v[l].transpose(0, 2, 3, 1).reshape(n_pool, W_HEADS, page_size)
        lf_pool = cache_fox_logf[l].transpose(0, 2, 1)
        page_bias = _page_bias(page_table, lf_pool)
        xp, hf = _merge_ffn_and_fox_sample(
            xp, hm_p, hf_p, wo, g_ffn, wg, wu, wd, g_final, page_table, qf, grow3, k_new, v_new, page_bias,
            k_pool, v_pool, final_norm=(l == depth - 1))
        xs = _merge_ffn(xs, hm, hf, wo, g_ffn, wg, wu, wd, g_final, final_norm=(l == depth - 1))
        sk.append(k_new.reshape(dec_batch, dec_seq, H_F, DH_F))
        sv.append(v_new.reshape(dec_batch, dec_seq, H_F, DH_F))
        slf.append(grow3[:, gate0:, :].transpose(0, 2, 1))
        sc.append(c_s); sn.append(n_s); sm.append(m_s)

    st = lambda a, ref: jnp.stack(a, axis=0).astype(ref.dtype)
    return (xp.reshape(batch, seq, d_model), xs.reshape(dec_batch, dec_seq, d_model),
            st(pk, cache_fox_k), st(pv, cache_fox_v), st(plf, cache_fox_logf),
            st(pc, state_mlstm_C), st(pn, state_mlstm_n), st(pm, state_mlstm_m),
            st(sk, cache_fox_k), st(sv, cache_fox_v), st(slf, cache_fox_logf),
            st(sc, state_mlstm_C), st(sn, state_mlstm_n), st(sm, state_mlstm_m))
```

```python
import functools

import jax
import jax.numpy as jnp
from jax import lax
from jax.experimental import pallas as pl
from jax.experimental.pallas import tpu as pltpu

F32 = jnp.float32
BF16 = jnp.bfloat16
HI = lax.Precision.HIGHEST
NT = (((1,), (1,)), ((), ()))
TN = (((0,), (0,)), ((), ()))

EPS = 1e-6
LOG2E = 1.4426950408889634
H_M = 4
DH_M = 128
H_F = 8
DH_F = 64
W_HEADS = 512
N_GATE_ROWS = 16
LANES = 128
MIB = 1024 * 1024

PROJ_ROWS = 512
FFN_ROWS = 512
FFN_CHUNK = 256
MLSTM_BATCH = 8
MLSTM_INTERLEAVE = 4
FOX_BLOCK = 512
FOX_KEYS = 512
PAGES_PER_GROUP = 8
PAGE_SLOTS = 4


def _rms(x, g):
    return x * lax.rsqrt(jnp.mean(x * x, axis=-1, keepdims=True) + EPS) * g


def _log_sigmoid(x):
    return jnp.minimum(x, 0.0) - jnp.log1p(jnp.exp(-jnp.abs(x)))


def _sigmoid(x):
    return 1.0 / (1.0 + jnp.exp(-x))


def _const_spec(shape):
    return pl.BlockSpec(shape, lambda *_: (0,) * len(shape), pipeline_mode=pl.Buffered(1))


def _proj_kernel(x_ref, g_ref, wt_ref, wkv_ref, wg_ref, bcol_ref, brow_ref, *outs, kv_transposed):
    if kv_transposed:
        qm_ref, km_ref, vm_ref, om_ref, qf_ref, kf_ref, vf_ref, kfb_ref, vfb_ref, gcol_ref, grow_ref = outs
    else:
        qm_ref, km_ref, vm_ref, om_ref, qf_ref, kf_ref, vf_ref, gcol_ref, grow_ref = outs
    h = _rms(x_ref[...], g_ref[...]).astype(BF16)
    rows = h.shape[0]

    def mm(i):
        w = wt_ref[i * W_HEADS:(i + 1) * W_HEADS, :]
        return lax.dot_general(h, w, NT, preferred_element_type=F32)

    def mm_t(w):
        return lax.dot_general(w, h, NT, preferred_element_type=F32)

    qm_ref[...] = mm(0).astype(qm_ref.dtype)
    km_ref[...] = (mm(1) * (DH_M ** -0.5)).astype(km_ref.dtype)
    if kv_transposed:
        vm_ref[0] = mm_t(wt_ref[2 * W_HEADS:3 * W_HEADS, :]).astype(vm_ref.dtype)
        om_ref[0] = mm_t(wt_ref[3 * W_HEADS:4 * W_HEADS, :]).astype(om_ref.dtype)
    else:
        vm_ref[...] = mm(2).astype(vm_ref.dtype)
        om_ref[...] = mm(3).astype(om_ref.dtype)
    qf_ref[...] = (mm(4) * (DH_F ** -0.5 * LOG2E)).astype(qf_ref.dtype)
    if kv_transposed:
        kt = lax.dot_general(wkv_ref[0:W_HEADS, :], h, NT, preferred_element_type=F32)
        kf_ref[0] = kt
        kfb_ref[...] = kt.T.astype(BF16)
        vt = lax.dot_general(wkv_ref[W_HEADS:2 * W_HEADS, :], h, NT, preferred_element_type=F32)
        vf_ref[0] = vt
        vfb_ref[0] = vt.astype(BF16)
    else:
        kf_ref[...] = lax.dot_general(h, wkv_ref[0:W_HEADS, :], NT, preferred_element_type=F32)
        vf_ref[...] = lax.dot_general(h, wkv_ref[W_HEADS:2 * W_HEADS, :], NT, preferred_element_type=F32)
    pre_c = lax.dot_general(h, wg_ref[...], NT, preferred_element_type=F32) + bcol_ref[...]
    lane = lax.broadcasted_iota(jnp.int32, (rows, LANES), 1)
    gcol_ref[...] = jnp.where(lane < H_M, pre_c, _log_sigmoid(pre_c))
    pre_r = lax.dot_general(wg_ref[0:N_GATE_ROWS, :], h, NT, preferred_element_type=F32) + brow_ref[...]
    row = lax.broadcasted_iota(jnp.int32, (N_GATE_ROWS, rows), 0)
    grow_ref[...] = jnp.where(row < H_M, pre_r, _log_sigmoid(pre_r))


def _project(x2d, g, wt_main, wt_kv, wt_g, bcol, brow, *, batch, seq, kv_transposed, act_dtype):
    rows_total, d_model = x2d.shape
    tm = min(PROJ_ROWS, rows_total)
    steps = rows_total // tm
    per_seq = max(seq // tm, 1)
    row_spec = lambda w: pl.BlockSpec((tm, w), lambda i: (i, 0))
    in_specs = [row_spec(d_model), _const_spec(g.shape), _const_spec(wt_main.shape), _const_spec(wt_kv.shape),
                _const_spec(wt_g.shape), _const_spec(bcol.shape), _const_spec(brow.shape)]
    act = jax.ShapeDtypeStruct((rows_total, W_HEADS), act_dtype)
    act32 = jax.ShapeDtypeStruct((rows_total, W_HEADS), F32)
    out_shape = [act, act, act, act32, act]
    out_specs = [row_spec(W_HEADS)] * 5
    if kv_transposed:
        kv_spec = pl.BlockSpec((1, W_HEADS, tm), lambda i: (i // per_seq, 0, i % per_seq))
        out_shape[2] = jax.ShapeDtypeStruct((batch, W_HEADS, seq), act_dtype)
        out_shape[3] = jax.ShapeDtypeStruct((batch, W_HEADS, seq), F32)
        out_specs[2] = out_specs[3] = kv_spec
        out_shape += [jax.ShapeDtypeStruct((batch, W_HEADS, seq), F32)] * 2
        out_shape += [jax.ShapeDtypeStruct((rows_total, W_HEADS), BF16),
                      jax.ShapeDtypeStruct((batch, W_HEADS, seq), BF16)]
        out_specs += [kv_spec, kv_spec, row_spec(W_HEADS), kv_spec]
    else:
        out_shape += [act32, act32]
        out_specs += [row_spec(W_HEADS)] * 2
    out_shape += [jax.ShapeDtypeStruct((rows_total, LANES), F32),
                  jax.ShapeDtypeStruct((N_GATE_ROWS, rows_total), F32)]
    out_specs += [row_spec(LANES), pl.BlockSpec((N_GATE_ROWS, tm), lambda i: (0, i))]
    return pl.pallas_call(
        functools.partial(_proj_kernel, kv_transposed=kv_transposed),
        grid=(steps,),
        in_specs=in_specs,
        out_specs=out_specs,
        out_shape=out_shape,
        compiler_params=pltpu.CompilerParams(dimension_semantics=("arbitrary",), vmem_limit_bytes=48 * MIB),
        name="proj",
    )(x2d, g, wt_main, wt_kv, wt_g, bcol, brow)


def _mlstm_kernel(q_ref, k_ref, v_ref, om_ref, gc_ref, gr_ref, hg_ref, c0_ref, n0_ref, m0_ref,
                  hm_ref, c_ref, n_ref, m_ref, *, bb, chunk):
    @pl.when(pl.program_id(1) == 0)
    def _():
        c_ref[...] = c0_ref[...]
        n_ref[...] = n0_ref[...]
        m_ref[...] = m0_ref[...]

    t_idx = lax.broadcasted_iota(jnp.int32, (chunk, chunk), 0)
    s_idx = lax.broadcasted_iota(jnp.int32, (chunk, chunk), 1)
    causal = s_idx <= t_idx
    tril = causal.astype(F32)
    triu = (t_idx <= s_idx).astype(F32)

    def per_batch(b, carry):
        gc = gc_ref[b]
        gr = gr_ref[b]
        bc = jnp.dot(tril, gc, precision=HI, preferred_element_type=F32)
        br = jnp.dot(gr, triu, precision=HI, preferred_element_type=F32)
        q = q_ref[b]
        k = k_ref[b]
        v = v_ref[b]
        om = om_ref[b]
        for h in range(H_M):
            sl = slice(h * DH_M, (h + 1) * DH_M)
            li_c = gc[:, h:h + 1]
            li_r = gr[h:h + 1, :]
            b_c = bc[:, H_M + h:H_M + h + 1]
            b_r = br[H_M + h:H_M + h + 1, :]
            m_prev = m_ref[pl.ds(b, 1), h:h + 1]
            dmat = jnp.where(causal, b_c - b_r + li_r, -jnp.inf)
            inter = m_prev + b_c
            m_t = jnp.maximum(inter, jnp.max(dmat, axis=-1, keepdims=True))
            qh = q[:, sl].astype(BF16)
            kh = k[:, sl].astype(BF16)
            vh = v[:, sl].astype(BF16)
            smat = lax.dot_general(qh, kh, NT, preferred_element_type=F32) * jnp.exp(dmat - m_t)
            w_inter = jnp.exp(inter - m_t)
            c_prev = c_ref[b, h]
            cq = lax.dot_general(qh, c_prev.astype(BF16), NT, preferred_element_type=F32)
            num = w_inter * cq + jnp.dot(smat.astype(BF16), vh, preferred_element_type=F32)
            n_prev = n_ref[b, h:h + 1, :]
            nq = jnp.sum(qh.astype(F32) * n_prev, axis=-1, keepdims=True)
            den = w_inter * nq + jnp.sum(smat, axis=-1, keepdims=True)
            hh = num / jnp.maximum(jnp.abs(den), jnp.exp(-m_t))
            hn = hh * lax.rsqrt(jnp.mean(hh * hh, axis=-1, keepdims=True) + EPS) * hg_ref[:, sl]
            hm_ref[b, :, sl] = (hn * _sigmoid(om[:, sl].astype(F32))).astype(hm_ref.dtype)
            b_last = b_c[chunk - 1:chunk, :]
            m_new = m_t[chunk - 1:chunk, :]
            decay = jnp.exp(m_prev + b_last - m_new)
            w_s = jnp.exp(li_c + b_last - b_c - m_new)
            vw = (vh.astype(F32) * w_s).astype(BF16)
            c_ref[b, h] = decay * c_prev + lax.dot_general(vw, kh, TN, preferred_element_type=F32)
            n_ref[b, h:h + 1, :] = decay * n_prev + jnp.sum(kh.astype(F32) * w_s, axis=0, keepdims=True)
            m_ref[pl.ds(b, 1), h:h + 1] = m_new
        return carry

    lax.fori_loop(0, bb, per_batch, 0)


def _mlstm(qm, km, vm, om, gcol, grow3, head_g, c0, n0, m0, *, batch, seq, chunk, out_dtype):
    bb = MLSTM_BATCH
    n_chunks = seq // chunk
    as3 = lambda a: a.reshape(batch, seq, a.shape[-1])
    tok = lambda w: pl.BlockSpec((bb, chunk, w), lambda g, c: (g, c, 0))
    state = lambda shape: pl.BlockSpec((bb,) + shape, lambda g, c: (g,) + (0,) * len(shape))
    in_specs = [tok(W_HEADS), tok(W_HEADS), tok(W_HEADS), tok(W_HEADS), tok(LANES),
                pl.BlockSpec((bb, N_GATE_ROWS, chunk), lambda g, c: (g, 0, c)),
                _const_spec(head_g.shape),
                state((H_M, DH_M, DH_M)), state((H_M, DH_M)), state((H_M,))]
    out_specs = [tok(W_HEADS), state((H_M, DH_M, DH_M)), state((H_M, DH_M)), state((H_M,))]
    out_shape = [jax.ShapeDtypeStruct((batch, seq, W_HEADS), out_dtype),
                 jax.ShapeDtypeStruct(c0.shape, F32), jax.ShapeDtypeStruct(n0.shape, F32),
                 jax.ShapeDtypeStruct(m0.shape, F32)]
    hm, c_new, n_new, m_new = pl.pallas_call(
        functools.partial(_mlstm_kernel, bb=bb, chunk=chunk),
        grid=(batch // bb, n_chunks),
        in_specs=in_specs,
        out_specs=out_specs,
        out_shape=out_shape,
        compiler_params=pltpu.CompilerParams(dimension_semantics=("arbitrary", "arbitrary"),
                                             vmem_limit_bytes=48 * MIB),
        name="mlstm",
    )(as3(qm), as3(km), as3(vm), as3(om), as3(gcol), grow3, head_g, c0, n0, m0)
    return hm.reshape(batch * seq, W_HEADS), c_new, n_new, m_new


def _mlstm_t_kernel(q_ref, k_ref, vt_ref, omt_ref, gc_ref, gr_ref, hg_ref, c0_ref, n0_ref, m0_ref,
                    hm_ref, c_ref, n_ref, m_ref, *, bb, chunk):
    @pl.when(pl.program_id(1) == 0)
    def _():
        c_ref[...] = c0_ref[...]
        n_ref[...] = n0_ref[...]
        m_ref[...] = m0_ref[...]

    s_idx = lax.broadcasted_iota(jnp.int32, (chunk, chunk), 0)
    t_idx = lax.broadcasted_iota(jnp.int32, (chunk, chunk), 1)
    causal = s_idx <= t_idx
    tril = (t_idx <= s_idx).astype(F32)
    triu = causal.astype(F32)

    def per_group(i, carry):
        bs = [i * MLSTM_INTERLEAVE + j for j in range(MLSTM_INTERLEAVE)]
        gc = [gc_ref[b] for b in bs]
        gr = [gr_ref[b] for b in bs]
        bc = [jnp.dot(tril, g, precision=HI, preferred_element_type=F32) for g in gc]
        br = [jnp.dot(g, triu, precision=HI, preferred_element_type=F32) for g in gr]
        n_all = [n_ref[b] for b in bs]
        m_all = [m_ref[pl.ds(b, 1), :] for b in bs]
        chains = [(j, h) for j in range(MLSTM_INTERLEAVE) for h in range(H_M)]
        ids = range(len(chains))
        sls = [slice(h * DH_M, (h + 1) * DH_M) for _, h in chains]
        c_prev = [c_ref[bs[j], h] for j, h in chains]
        n_prev = [n_all[j][h:h + 1, :] for j, h in chains]
        m_prev = [m_all[j][:, h:h + 1] for j, h in chains]
        qh = [q_ref[bs[j], :, sls[c]].astype(BF16) for c, (j, _) in enumerate(chains)]
        kh = [k_ref[bs[j], :, sls[c]].astype(BF16) for c, (j, _) in enumerate(chains)]
        vth = [vt_ref[bs[j], sls[c], :].astype(BF16) for c, (j, _) in enumerate(chains)]
        qk = [lax.dot_general(kh[c], qh[c], NT, preferred_element_type=F32) for c in ids]
        cq = [lax.dot_general(c_prev[c].astype(BF16), qh[c], NT, preferred_element_type=F32) for c in ids]
        nq = [lax.dot_general(jnp.broadcast_to(n_prev[c], (8, DH_M)).astype(BF16), qh[c], NT,
                              preferred_element_type=F32)[0:1, :] for c in ids]
        li_r = [gr[j][h:h + 1, :] for j, h in chains]
        b_r = [br[j][H_M + h:H_M + h + 1, :] for j, h in chains]
        col = [gc[j][:, h:h + 1] - bc[j][:, H_M + h:H_M + h + 1] for j, h in chains]
        dmat = [jnp.where(causal, b_r[c] + col[c], -jnp.inf) for c in ids]
        inter = [m_prev[c] + b_r[c] for c in ids]
        m_t = [jnp.maximum(inter[c], jnp.max(dmat[c], axis=0, keepdims=True)) for c in ids]
        smat = [qk[c] * jnp.exp(dmat[c] - m_t[c]) for c in ids]
        w_inter = [jnp.exp(inter[c] - m_t[c]) for c in ids]
        sv = [jnp.dot(vth[c], smat[c].astype(BF16), preferred_element_type=F32) for c in ids]
        outs, vw, w_s_c, decay, m_new = [], [], [], [], []
        for c, (j, h) in enumerate(chains):
            num = w_inter[c] * cq[c] + sv[c]
            den = w_inter[c] * nq[c] + jnp.sum(smat[c], axis=0, keepdims=True)
            hh = num * (1.0 / jnp.maximum(jnp.abs(den), jnp.exp(-m_t[c])))
            hn = hh * lax.rsqrt(jnp.mean(hh * hh, axis=0, keepdims=True) + EPS) * hg_ref[sls[c], :]
            gate = _sigmoid(omt_ref[bs[j], sls[c], :].astype(F32))
            outs.append((hn * gate).T.astype(hm_ref.dtype))
            b_last = b_r[c][:, chunk - 1:chunk]
            m_new.append(m_t[c][:, chunk - 1:chunk])
            decay.append(jnp.exp(m_prev[c] + b_last - m_new[c]))
            w_s_r = jnp.exp(li_r[c] + b_last - b_r[c] - m_new[c])
            w_s_c.append(jnp.exp(col[c] + b_last - m_new[c]))
            vw.append((vth[c].astype(F32) * w_s_r).astype(BF16))
        for c, (j, h) in enumerate(chains):
            c_new = decay[c] * c_prev[c] + jnp.dot(vw[c], kh[c], preferred_element_type=F32)
            n_new = decay[c] * n_prev[c] + jnp.sum(kh[c].astype(F32) * w_s_c[c], axis=0, keepdims=True)
            hm_ref[bs[j], :, sls[c]] = outs[c]
            c_ref[bs[j], h] = c_new
            n_ref[bs[j], h:h + 1, :] = n_new
            m_ref[pl.ds(bs[j], 1), h:h + 1] = m_new[c]
        return carry

    lax.fori_loop(0, bb // MLSTM_INTERLEAVE, per_group, 0)


def _mlstm_prompt(qm, km, vmt, omt, gcol, grow3, head_g, c0, n0, m0, *, batch, seq, chunk, out_dtype):
    bb = MLSTM_BATCH
    n_chunks = seq // chunk
    as3 = lambda a: a.reshape(batch, seq, a.shape[-1])
    tok = lambda w: pl.BlockSpec((bb, chunk, w), lambda g, c: (g, c, 0))
    tok_t = pl.BlockSpec((bb, W_HEADS, chunk), lambda g, c: (g, 0, c))
    state = lambda shape: pl.BlockSpec((bb,) + shape, lambda g, c: (g,) + (0,) * len(shape))
    hg_col = head_g.reshape(W_HEADS, 1)
    in_specs = [tok(W_HEADS), tok(W_HEADS), tok_t, tok_t, tok(LANES),
                pl.BlockSpec((bb, N_GATE_ROWS, chunk), lambda g, c: (g, 0, c)),
                _const_spec(hg_col.shape),
                state((H_M, DH_M, DH_M)), state((H_M, DH_M)), state((H_M,))]
    out_specs = [tok(W_HEADS), state((H_M, DH_M, DH_M)), state((H_M, DH_M)), state((H_M,))]
    out_shape = [jax.ShapeDtypeStruct((batch, seq, W_HEADS), out_dtype),
                 jax.ShapeDtypeStruct(c0.shape, F32), jax.ShapeDtypeStruct(n0.shape, F32),
                 jax.ShapeDtypeStruct(m0.shape, F32)]
    hm, c_new, n_new, m_new = pl.pallas_call(
        functools.partial(_mlstm_t_kernel, bb=bb, chunk=chunk),
        grid=(batch // bb, n_chunks),
        in_specs=in_specs,
        out_specs=out_specs,
        out_shape=out_shape,
        compiler_params=pltpu.CompilerParams(dimension_semantics=("arbitrary", "arbitrary"),
                                             vmem_limit_bytes=48 * MIB),
        name="mlstm_prompt",
    )(as3(qm), as3(km), vmt, omt, as3(gcol), grow3, hg_col, c0, n0, m0)
    return hm.reshape(batch * seq, W_HEADS), c_new, n_new, m_new


def _fox_cumsum_kernel(gc_ref, gr_ref, cc_ref, cr_ref):
    seq = gc_ref.shape[1]
    t_idx = lax.broadcasted_iota(jnp.int32, (LANES, LANES), 0)
    s_idx = lax.broadcasted_iota(jnp.int32, (LANES, LANES), 1)
    tril = (s_idx <= t_idx).astype(F32)
    triu = (t_idx <= s_idx).astype(F32)
    carry_c = jnp.zeros((1, LANES), F32)
    carry_r = jnp.zeros((N_GATE_ROWS, 1), F32)
    for j in range(seq // LANES):
        blk = slice(j * LANES, (j + 1) * LANES)
        cb = jnp.dot(tril, gc_ref[0, blk, :], precision=HI, preferred_element_type=F32) + carry_c
        cc_ref[0, blk, :] = cb * LOG2E
        carry_c = cb[LANES - 1:LANES, :]
        rb = jnp.dot(gr_ref[:, blk], triu, precision=HI, preferred_element_type=F32) + carry_r
        cr_ref[:, blk] = rb * LOG2E
        carry_r = rb[:, LANES - 1:LANES]


def _fox_cumsum(gcol, grow, *, batch, seq):
    return pl.pallas_call(
        _fox_cumsum_kernel,
        grid=(batch,),
        in_specs=[pl.BlockSpec((1, seq, LANES), lambda b: (b, 0, 0)),
                  pl.BlockSpec((N_GATE_ROWS, seq), lambda b: (0, b))],
        out_specs=[pl.BlockSpec((1, seq, LANES), lambda b: (b, 0, 0)),
                   pl.BlockSpec((N_GATE_ROWS, seq), lambda b: (0, b))],
        out_shape=[jax.ShapeDtypeStruct((batch, seq, LANES), F32),
                   jax.ShapeDtypeStruct((N_GATE_ROWS, batch * seq), F32)],
        compiler_params=pltpu.CompilerParams(dimension_semantics=("arbitrary",)),
        name="fox_cumsum",
    )(gcol.reshape(batch, seq, LANES), grow)


def _fox_prompt_kernel(q_ref, k_ref, vt_ref, cc_ref, cr_ref, o_ref, qs_ref, m_ref, l_ref, acc_ref, *, blk, kb):
    i = pl.program_id(1)
    n_pairs = H_F // 2
    lane = lax.broadcasted_iota(jnp.int32, (blk, LANES), 1)
    low_half = lane < DH_F
    s_idx = lax.broadcasted_iota(jnp.int32, (kb, blk), 0)
    t_idx = lax.broadcasted_iota(jnp.int32, (kb, blk), 1)
    gate0 = N_GATE_ROWS - H_F
    q_cols = pl.ds(pl.multiple_of(i * blk, blk), blk)

    for pair in range(n_pairs):
        slab = slice(pair * LANES, (pair + 1) * LANES)
        q_pair = q_ref[:, slab]
        zero = jnp.zeros_like(q_pair)
        qs_ref[pair, 0:blk, :] = jnp.where(low_half, q_pair, zero)
        qs_ref[pair, blk:2 * blk, :] = jnp.where(low_half, zero, q_pair)
    m_ref[...] = jnp.full(m_ref.shape, -jnp.inf, F32)
    l_ref[...] = jnp.zeros(l_ref.shape, F32)
    acc_ref[...] = jnp.zeros(acc_ref.shape, F32)

    def step(j, diag_offset):
        keys = pl.ds(pl.multiple_of(j * kb, kb), kb)
        pairs = range(n_pairs)
        raw = [lax.dot_general(k_ref[keys, pair * LANES:(pair + 1) * LANES], qs_ref[pair], NT,
                               preferred_element_type=F32) for pair in pairs]
        pb, alpha = [], []
        for pair in pairs:
            ck = jnp.concatenate(
                [jnp.broadcast_to(cc_ref[keys, gate0 + 2 * pair + e:gate0 + 2 * pair + e + 1], (kb, blk))
                 for e in range(2)], axis=1)
            cq = jnp.concatenate(
                [cr_ref[gate0 + 2 * pair + e:gate0 + 2 * pair + e + 1, q_cols] for e in range(2)], axis=1)
            s = raw[pair] - ck
            if diag_offset is not None:
                causal = s_idx + diag_offset <= t_idx
                s = jnp.where(jnp.concatenate([causal] * 2, axis=1), s, -jnp.inf)
            m_old = m_ref[pair]
            m_new = jnp.maximum(m_old, jnp.max(s, axis=0, keepdims=True) + cq)
            p = jnp.exp2(s - (m_new - cq))
            alpha.append(jnp.exp2(m_old - m_new))
            l_ref[pair] = alpha[pair] * l_ref[pair] + jnp.sum(p, axis=0, keepdims=True)
            m_ref[pair] = m_new
            pb.append(p.astype(BF16))
        for pair in pairs:
            for e in range(2):
                own = slice(pair * LANES + e * DH_F, pair * LANES + (e + 1) * DH_F)
                lanes = slice(e * blk, (e + 1) * blk)
                pv = jnp.dot(vt_ref[0, own, keys], pb[pair][:, lanes], preferred_element_type=F32)
                acc_ref[own, :] = alpha[pair][:, lanes] * acc_ref[own, :] + pv

    def loop_body(j, carry):
        step(j, None)
        return carry

    per_q = blk // kb
    lax.fori_loop(0, i * per_q, loop_body, 0)
    for r in range(per_q):
        step(i * per_q + r, r * kb)
    inv_l = jnp.concatenate(
        [jnp.broadcast_to(1.0 / l_ref[h // 2][:, (h % 2) * blk:(h % 2 + 1) * blk], (DH_F, blk)) for h in range(H_F)],
        axis=0)
    o_ref[...] = (acc_ref[...] * inv_l).T.astype(o_ref.dtype)


def _fox_prompt(qf, kfr, vtb, ccol, crow, *, batch, seq):
    blk = FOX_BLOCK
    nq = seq // blk
    stat = pltpu.VMEM((H_F // 2, 1, 2 * blk), F32)
    return pl.pallas_call(
        functools.partial(_fox_prompt_kernel, blk=blk, kb=FOX_KEYS),
        grid=(batch, nq),
        in_specs=[pl.BlockSpec((blk, W_HEADS), lambda b, i: (b * nq + i, 0)),
                  pl.BlockSpec((seq, W_HEADS), lambda b, i: (b, 0)),
                  pl.BlockSpec((1, W_HEADS, seq), lambda b, i: (b, 0, 0)),
                  pl.BlockSpec((seq, LANES), lambda b, i: (b, 0)),
                  pl.BlockSpec((N_GATE_ROWS, seq), lambda b, i: (0, b))],
        out_specs=pl.BlockSpec((blk, W_HEADS), lambda b, i: (b * nq + i, 0)),
        out_shape=jax.ShapeDtypeStruct((batch * seq, W_HEADS), BF16),
        scratch_shapes=[pltpu.VMEM((H_F // 2, 2 * blk, LANES), BF16), stat, stat, pltpu.VMEM((W_HEADS, blk), F32)],
        compiler_params=pltpu.CompilerParams(dimension_semantics=("arbitrary", "arbitrary"),
                                             vmem_limit_bytes=48 * MIB),
        name="fox_prompt",
    )(qf, kfr, vtb, ccol.reshape(batch * seq, LANES), crow)


def _page_bias_kernel(pt_ref, lf_hbm, o_ref, buf, sem):
    b = pl.program_id(0)
    nb = pl.num_programs(0)
    n_pages = buf.shape[1]
    slot = b % 2

    def page_copy(bi, p, sl):
        return pltpu.make_async_copy(lf_hbm.at[pt_ref[bi, p]], buf.at[sl, p], sem.at[sl])

    def start_all(bi, sl):
        def body(p, c):
            page_copy(bi, p, sl).start()
            return c
        lax.fori_loop(0, n_pages, body, 0)

    @pl.when(b == 0)
    def _():
        start_all(0, 0)

    @pl.when(b + 1 < nb)
    def _():
        start_all(b + 1, 1 - slot)

    def wait_body(p, c):
        page_copy(b, p, slot).wait()
        return c
    lax.fori_loop(0, n_pages, wait_body, 0)

    x = buf[slot].reshape(n_pages * H_F, LANES)
    t_idx = lax.broadcasted_iota(jnp.int32, (LANES, LANES), 0)
    s_idx = lax.broadcasted_iota(jnp.int32, (LANES, LANES), 1)
    later = (t_idx > s_idx).astype(F32)
    within = jnp.dot(x, later, precision=HI, preferred_element_type=F32)
    total = jnp.sum(x, axis=-1, keepdims=True)
    run = jnp.zeros((H_F, LANES), F32)
    for p in range(n_pages - 1, -1, -1):
        rows = slice(p * H_F, (p + 1) * H_F)
        o_ref[0, p] = (within[rows] + run) * (-LOG2E)
        run = run + total[rows]


def _page_bias(page_table, lf_pool):
    batch, n_pages = page_table.shape
    return pl.pallas_call(
        _page_bias_kernel,
        grid_spec=pltpu.PrefetchScalarGridSpec(
            num_scalar_prefetch=1,
            grid=(batch,),
            in_specs=[pl.BlockSpec(memory_space=pl.ANY)],
            out_specs=pl.BlockSpec((1, n_pages, H_F, LANES), lambda b, pt: (b, 0, 0, 0)),
            scratch_shapes=[pltpu.VMEM((2, n_pages, H_F, LANES), F32), pltpu.SemaphoreType.DMA((2,))],
        ),
        out_shape=jax.ShapeDtypeStruct((batch, n_pages, H_F, LANES), F32),
        compiler_params=pltpu.CompilerParams(dimension_semantics=("arbitrary",)),
        name="page_bias",
    )(page_table, lf_pool)


def _paged_attention(pt_ref, q_ref, gr_ref, kn_ref, vn_ref, cp_ref, k_hbm, v_hbm, o_ref,
                     kbuf, vbuf, sem, acc_ref, *, n_groups, phases):
    b = pl.program_id(0)
    nb = pl.num_programs(0)
    group = PAGES_PER_GROUP
    t_new = q_ref.shape[0]
    n_rows = t_new * H_F

    def page_copies(bi, g, sl):
        cps = []
        for j in range(group):
            page = pt_ref[bi, g * group + j]
            cps.append(pltpu.make_async_copy(k_hbm.at[page], kbuf.at[sl, j], sem.at[sl, 0]))
            cps.append(pltpu.make_async_copy(v_hbm.at[page], vbuf.at[sl, j], sem.at[sl, 1]))
        return cps

    def start_group(bi, g, sl):
        for cp in page_copies(bi, g, sl):
            cp.start()

    @pl.when(b == 0)
    def _():
        for a in range(PAGE_SLOTS - 1):
            start_group(0, a, a)

    sub = lax.broadcasted_iota(jnp.int32, (H_F, W_HEADS), 0)
    lane = lax.broadcasted_iota(jnp.int32, (H_F, W_HEADS), 1)
    own_head = (lane // DH_F) == sub
    q = q_ref[...].astype(F32)
    qbd = jnp.concatenate(
        [jnp.where(own_head, jnp.broadcast_to(q[t:t + 1, :], (H_F, W_HEADS)), 0.0) for t in range(t_new)],
        axis=0).astype(BF16)

    gate0 = N_GATE_ROWS - H_F
    lf_new = gr_ref[0, gate0:N_GATE_ROWS, :]
    a_idx = lax.broadcasted_iota(jnp.int32, (t_new, t_new), 0)
    b_idx = lax.broadcasted_iota(jnp.int32, (t_new, t_new), 1)
    c_new = jnp.dot(lf_new, (a_idx <= b_idx).astype(F32), precision=HI, preferred_element_type=F32) * LOG2E
    cq = jnp.concatenate([c_new[:, t:t + 1] for t in range(t_new)], axis=0)

    acc_ref[...] = jnp.zeros_like(acc_ref)

    def body(g, carry, side_work):
        m_i, l_i = carry
        slot = g % PAGE_SLOTS
        ahead = g + (PAGE_SLOTS - 1)

        @pl.when(ahead < n_groups)
        def _():
            start_group(b, ahead, ahead % PAGE_SLOTS)

        @pl.when(jnp.logical_and(ahead >= n_groups, b + 1 < nb))
        def _():
            start_group(b + 1, ahead - n_groups, ahead % PAGE_SLOTS)

        for cp in page_copies(b, g, slot):
            cp.wait()
        side = side_work(g)

        kcat = jnp.concatenate([kbuf[slot, j].astype(BF16) for j in range(group)], axis=1)
        s = jnp.dot(qbd, kcat, preferred_element_type=F32)
        next(side, None)
        first = pl.multiple_of(g * group, group)
        cpg = cp_ref[0, pl.ds(first, group)]
        bias = jnp.concatenate(
            [jnp.broadcast_to(cpg[j][None], (t_new, H_F, LANES)).reshape(n_rows, LANES) for j in range(group)],
            axis=1)
        s = s + (cq - bias)
        m_new = jnp.maximum(m_i, jnp.max(s, axis=-1, keepdims=True))
        p = jnp.exp2(s - m_new)
        alpha = jnp.exp2(m_i - m_new)
        l_new = alpha * l_i + jnp.sum(p, axis=-1, keepdims=True)
        next(side, None)
        vcat = jnp.concatenate([vbuf[slot, j].T.astype(BF16) for j in range(group)], axis=0)
        pv = jnp.dot(p.astype(BF16), vcat, preferred_element_type=F32)
        next(side, None)
        acc_ref[...] = alpha * acc_ref[...] + pv
        return m_new, l_new

    carry = (jnp.full((n_rows, 1), -jnp.inf, F32), jnp.zeros((n_rows, 1), F32))
    start = 0
    for stop, side_work in phases:
        carry = lax.fori_loop(start, stop, functools.partial(body, side_work=side_work), carry)
        start = stop
    assert start == n_groups
    m_i, l_i = carry

    kn = kn_ref[...].astype(BF16)
    vn = vn_ref[...].astype(BF16)
    s = lax.dot_general(qbd, kn, NT, preferred_element_type=F32)
    ck = jnp.broadcast_to(c_new[None], (t_new, H_F, t_new)).reshape(n_rows, t_new)
    s = s + (cq - ck)
    r_idx = lax.broadcasted_iota(jnp.int32, (n_rows, t_new), 0)
    k_idx = lax.broadcasted_iota(jnp.int32, (n_rows, t_new), 1)
    s = jnp.where(k_idx <= r_idx // H_F, s, -jnp.inf)
    m_new = jnp.maximum(m_i, jnp.max(s, axis=-1, keepdims=True))
    p = jnp.exp2(s - m_new)
    alpha = jnp.exp2(m_i - m_new)
    l_fin = alpha * l_i + jnp.sum(p, axis=-1, keepdims=True)
    acc = alpha * acc_ref[...] + jnp.dot(p.astype(BF16), vn, preferred_element_type=F32)
    out = acc / l_fin
    o_ref[...] = jnp.concatenate(
        [jnp.sum(jnp.where(own_head, out[t * H_F:(t + 1) * H_F, :], 0.0), axis=0, keepdims=True)
         for t in range(t_new)], axis=0).astype(o_ref.dtype)


def _merge_residual(x_ref, hm_ref, hf_ref, wo_ref, g2_ref):
    x1 = (x_ref[...]
          + jnp.dot(hm_ref[...].astype(BF16), wo_ref[0:W_HEADS, :], preferred_element_type=F32)
          + jnp.dot(hf_ref[...].astype(BF16), wo_ref[W_HEADS:2 * W_HEADS, :], preferred_element_type=F32))
    return x1, _rms(x1, g2_ref[...]).astype(BF16)


def _ffn_sample_kernel(pt_ref, x_ref, hm_ref, hf_ref, wo_ref, g2_ref, wg_ref, wu_ref, wd_ref, g3_ref,
                       q_ref, gr_ref, kn_ref, vn_ref, cp_ref, k_hbm, v_hbm, y_ref, o_ref,
                       h_ref, x2_ref, kbuf, vbuf, sem, acc_ref, *, n_groups, final_norm):
    x1, h = _merge_residual(x_ref, hm_ref, hf_ref, wo_ref, g2_ref)
    h_ref[...] = h
    x2_ref[...] = x1
    ffn_chunks = wg_ref.shape[1] // FFN_CHUNK
    assert ffn_chunks <= n_groups

    def ffn_chunk(c):
        cols = pl.ds(pl.multiple_of(c * FFN_CHUNK, FFN_CHUNK), FFN_CHUNK)
        hh = h_ref[...]
        gate = jnp.dot(hh, wg_ref[:, cols], preferred_element_type=F32)
        yield
        up = jnp.dot(hh, wu_ref[:, cols], preferred_element_type=F32)
        act = (gate * _sigmoid(gate) * up).astype(BF16)
        yield
        x2_ref[...] += jnp.dot(act, wd_ref[cols, :], preferred_element_type=F32)
        yield

    phases = [(ffn_chunks, ffn_chunk), (n_groups, lambda g: iter(()))]
    _paged_attention(pt_ref, q_ref, gr_ref, kn_ref, vn_ref, cp_ref, k_hbm, v_hbm, o_ref, kbuf, vbuf, sem, acc_ref,
                     n_groups=n_groups, phases=phases)
    x2 = x2_ref[...]
    y_ref[...] = _rms(x2, g3_ref[...]) if final_norm else x2


def _merge_ffn_and_fox_sample(x2d, hm, hf, wo, g2, wg, wu, wd, g3, page_table, qf, grow3, k_new, v_new, page_bias,
                              k_pool, v_pool, *, final_norm):
    rows_total, d_model = x2d.shape
    batch, n_pages = page_table.shape
    t_new = qf.shape[0] // batch
    n_groups = n_pages // PAGES_PER_GROUP
    tm = rows_total // batch
    d_ff = wg.shape[1]
    assert n_pages % PAGES_PER_GROUP == 0 and n_groups % PAGE_SLOTS == 0
    assert rows_total % batch == 0 and tm % 16 == 0 and d_ff % FFN_CHUNK == 0 and W_HEADS % FFN_CHUNK == 0
    page_rows, page_len = k_pool.shape[1], k_pool.shape[2]
    row = lambda w: pl.BlockSpec((tm, w), lambda b, pt: (b, 0))
    tok = lambda w: pl.BlockSpec((t_new, w), lambda b, pt: (b, 0))
    return pl.pallas_call(
        functools.partial(_ffn_sample_kernel, n_groups=n_groups, final_norm=final_norm),
        grid_spec=pltpu.PrefetchScalarGridSpec(
            num_scalar_prefetch=1,
            grid=(batch,),
            in_specs=[row(d_model), row(W_HEADS), row(W_HEADS), _const_spec(wo.shape), _const_spec(g2.shape),
                      _const_spec(wg.shape), _const_spec(wu.shape), _const_spec(wd.shape), _const_spec(g3.shape),
                      tok(W_HEADS),
                      pl.BlockSpec((1, N_GATE_ROWS, t_new), lambda b, pt: (b, 0, 0)),
                      tok(W_HEADS), tok(W_HEADS),
                      pl.BlockSpec((1, n_pages, H_F, LANES), lambda b, pt: (b, 0, 0, 0)),
                      pl.BlockSpec(memory_space=pl.ANY), pl.BlockSpec(memory_space=pl.ANY)],
            out_specs=[row(d_model), tok(W_HEADS)],
            scratch_shapes=[pltpu.VMEM((tm, d_model), BF16),
                            pltpu.VMEM((tm, d_model), F32),
                            pltpu.VMEM((PAGE_SLOTS, PAGES_PER_GROUP, page_rows, page_len), F32),
                            pltpu.VMEM((PAGE_SLOTS, PAGES_PER_GROUP, page_rows, page_len), F32),
                            pltpu.SemaphoreType.DMA((PAGE_SLOTS, 2)),
                            pltpu.VMEM((t_new * H_F, W_HEADS), F32)],
        ),
        out_shape=[jax.ShapeDtypeStruct((rows_total, d_model), F32),
                   jax.ShapeDtypeStruct((batch * t_new, W_HEADS), F32)],
        compiler_params=pltpu.CompilerParams(dimension_semantics=("arbitrary",), vmem_limit_bytes=58 * MIB),
        name="ffn_and_fox_sample",
    )(page_table, x2d, hm, hf, wo, g2, wg, wu, wd, g3, qf, grow3, k_new, v_new, page_bias, k_pool, v_pool)


def _ffn_kernel(x_ref, hm_ref, hf_ref, wo_ref, g2_ref, wg_ref, wu_ref, wd_ref, g3_ref, y_ref, *, final_norm):
    x1, h = _merge_residual(x_ref, hm_ref, hf_ref, wo_ref, g2_ref)
    gate = jnp.dot(h, wg_ref[...], preferred_element_type=F32)
    up = jnp.dot(h, wu_ref[...], preferred_element_type=F32)
    act = (gate * _sigmoid(gate) * up).astype(BF16)
    x2 = x1 + jnp.dot(act, wd_ref[...], preferred_element_type=F32)
    y_ref[...] = _rms(x2, g3_ref[...]) if final_norm else x2


def _merge_ffn(x2d, hm, hf, wo, g2, wg, wu, wd, g3, *, final_norm):
    rows_total, d_model = x2d.shape
    tm = min(FFN_ROWS, rows_total)
    row_spec = lambda w: pl.BlockSpec((tm, w), lambda i: (i, 0))
    return pl.pallas_call(
        functools.partial(_ffn_kernel, final_norm=final_norm),
        grid=(rows_total // tm,),
        in_specs=[row_spec(d_model), row_spec(W_HEADS), row_spec(W_HEADS), _const_spec(wo.shape),
                  _const_spec(g2.shape), _const_spec(wg.shape), _const_spec(wu.shape), _const_spec(wd.shape),
                  _const_spec(g3.shape)],
        out_specs=row_spec(d_model),
        out_shape=jax.ShapeDtypeStruct((rows_total, d_model), F32),
        compiler_params=pltpu.CompilerParams(dimension_semantics=("arbitrary",), vmem_limit_bytes=56 * MIB),
        name="merge_ffn",
    )(x2d, hm, hf, wo, g2, wg, wu, wd, g3)


def kernel(x_prompt, x_sample, cache_fox_k, cache_fox_v, cache_fox_logf, page_table, state_mlstm_C,
           state_mlstm_n, state_mlstm_m, norm_mix_g, w_in, b_m_igate, b_m_fgate, b_f_fgate, mlstm_head_g,
           w_out, norm_ffn_g, w_gate, w_up, w_down, norm_final_g):
    depth = w_in.shape[0]
    batch, seq, d_model = x_prompt.shape
    dec_batch, dec_seq, _ = x_sample.shape
    n_pool, page_size = cache_fox_k.shape[1], cache_fox_k.shape[2]
    xp = x_prompt.reshape(batch * seq, d_model)
    xs = x_sample.reshape(dec_batch * dec_seq, d_model)
    g_final = norm_final_g.reshape(1, d_model)
    pk, pv, plf, pc, pn, pm = [], [], [], [], [], []
    sk, sv, slf, sc, sn, sm = [], [], [], [], [], []
    gate0 = N_GATE_ROWS - H_F
    o_gm = 4 * W_HEADS
    o_qf = o_gm + 2 * H_M
    o_gf = o_qf + 3 * W_HEADS
    for l in range(depth):
        wt = jnp.swapaxes(w_in[l], 0, 1)
        wt_main = jnp.concatenate([wt[0:o_gm], wt[o_qf:o_qf + W_HEADS]], axis=0).astype(BF16)
        wt_kv = wt[o_qf + W_HEADS:o_gf].astype(BF16)
        wt_g = jnp.concatenate([wt[o_gm:o_qf], wt[o_gf:o_gf + H_F],
                                jnp.zeros((LANES - N_GATE_ROWS, d_model), F32)], axis=0).astype(BF16)
        bias = jnp.concatenate([b_m_igate[l], b_m_fgate[l], b_f_fgate[l],
                                jnp.zeros((LANES - N_GATE_ROWS,), F32)]).astype(F32)
        bcol = bias.reshape(1, LANES)
        brow = bias[:N_GATE_ROWS].reshape(N_GATE_ROWS, 1)
        g_mix = norm_mix_g[l].reshape(1, d_model)
        g_ffn = norm_ffn_g[l].reshape(1, d_model)
        head_g = mlstm_head_g[l].reshape(1, W_HEADS)
        wo = w_out[l].astype(BF16)
        wg = w_gate[l].astype(BF16)
        wu = w_up[l].astype(BF16)
        wd = w_down[l].astype(BF16)

        qm, km, vmt, omt, qf, kt, vt, kfr, vtb, gcol, grow = _project(
            xp, g_mix, wt_main, wt_kv, wt_g, bcol, brow, batch=batch, seq=seq, kv_transposed=True, act_dtype=BF16)
        chunk = min(LANES, seq)
        grow3 = grow.reshape(N_GATE_ROWS, batch, seq).transpose(1, 0, 2)
        hm, c_p, n_p, m_p = _mlstm_prompt(
            qm, km, vmt, omt, gcol, grow3, head_g,
            jnp.zeros((batch, H_M, DH_M, DH_M), F32), jnp.zeros((batch, H_M, DH_M), F32),
            jnp.zeros((batch, H_M), F32), batch=batch, seq=seq, chunk=chunk, out_dtype=BF16)
        ccol, crow = _fox_cumsum(gcol, grow, batch=batch, seq=seq)
        hf = _fox_prompt(qf, kfr, vtb, ccol, crow, batch=batch, seq=seq)
        hm_p, hf_p = hm, hf
        pk.append(kt.reshape(batch, H_F, DH_F, seq).transpose(0, 3, 1, 2))
        pv.append(vt.reshape(batch, H_F, DH_F, seq).transpose(0, 3, 1, 2))
        plf.append(grow3[:, gate0:, :].transpose(0, 2, 1))
        pc.append(c_p); pn.append(n_p); pm.append(m_p)

        qm, km, vm, om, qf, k_new, v_new, gcol, grow = _project(
            xs, g_mix, wt_main, wt_kv, wt_g, bcol, brow, batch=dec_batch, seq=dec_seq, kv_transposed=False,
            act_dtype=F32)
        grow3 = grow.reshape(N_GATE_ROWS, dec_batch, dec_seq).transpose(1, 0, 2)
        hm, c_s, n_s, m_s = _mlstm(
            qm, km, vm, om, gcol, grow3, head_g,
            state_mlstm_C[l].astype(F32), state_mlstm_n[l].astype(F32), state_mlstm_m[l].astype(F32),
            batch=dec_batch, seq=dec_seq, chunk=dec_seq, out_dtype=F32)
        k_pool = cache_fox_k[l].transpose(0, 2, 3, 1).reshape(n_pool, W_HEADS, page_size)
        v_pool = cache_fox_v[l].transpose(0, 2, 3, 1).reshape(n_pool, W_HEADS, page_size)
        lf_pool = cache_fox_logf[l].transpose(0, 2, 1)
        page_bias = _page_bias(page_table, lf_pool)
        xp, hf = _merge_ffn_and_fox_sample(
            xp, hm_p, hf_p, wo, g_ffn, wg, wu, wd, g_final, page_table, qf, grow3, k_new, v_new, page_bias,
            k_pool, v_pool, final_norm=(l == depth - 1))
        xs = _merge_ffn(xs, hm, hf, wo, g_ffn, wg, wu, wd, g_final, final_norm=(l == depth - 1))
        sk.append(k_new.reshape(dec_batch, dec_seq, H_F, DH_F))
        sv.append(v_new.reshape(dec_batch, dec_seq, H_F, DH_F))
        slf.append(grow3[:, gate0:, :].transpose(0, 2, 1))
        sc.append(c_s); sn.append(n_s); sm.append(m_s)

    st = lambda a, ref: jnp.stack(a, axis=0).astype(ref.dtype)
    return (xp.reshape(batch, seq, d_model), xs.reshape(dec_batch, dec_seq, d_model),
            st(pk, cache_fox_k), st(pv, cache_fox_v), st(plf, cache_fox_logf),
            st(pc, state_mlstm_C), st(pn, state_mlstm_n), st(pm, state_mlstm_m),
            st(sk, cache_fox_k), st(sv, cache_fox_v), st(slf, cache_fox_logf),
            st(sc, state_mlstm_C), st(sn, state_mlstm_n), st(sm, state_mlstm_m))
```

```python
import functools

import jax
import jax.numpy as jnp
from jax import lax
from jax.experimental import pallas as pl
from jax.experimental.pallas import tpu as pltpu

F32 = jnp.float32
BF16 = jnp.bfloat16
HI = lax.Precision.HIGHEST
NT = (((1,), (1,)), ((), ()))
TN = (((0,), (0,)), ((), ()))

EPS = 1e-6
LOG2E = 1.4426950408889634
H_M = 4
DH_M = 128
H_F = 8
DH_F = 64
W_HEADS = 512
N_GATE_ROWS = 16
LANES = 128
MIB = 1024 * 1024

PROJ_ROWS = 512
FFN_ROWS = 512
FFN_CHUNK = 256
MLSTM_BATCH = 8
MLSTM_INTERLEAVE = 4
FOX_BLOCK = 512
FOX_KEYS = 512
PAGES_PER_GROUP = 8
PAGE_SLOTS = 4


def _rms(x, g):
    return x * lax.rsqrt(jnp.mean(x * x, axis=-1, keepdims=True) + EPS) * g


def _log_sigmoid(x):
    return jnp.minimum(x, 0.0) - jnp.log1p(jnp.exp(-jnp.abs(x)))


def _sigmoid(x):
    return 1.0 / (1.0 + jnp.exp(-x))


def _const_spec(shape):
    return pl.BlockSpec(shape, lambda *_: (0,) * len(shape), pipeline_mode=pl.Buffered(1))


def _proj_kernel(x_ref, g_ref, wt_ref, wkv_ref, wg_ref, bcol_ref, brow_ref, *outs, kv_transposed):
    if kv_transposed:
        qm_ref, km_ref, vm_ref, om_ref, qf_ref, kf_ref, vf_ref, kfb_ref, vfb_ref, gcol_ref, grow_ref = outs
    else:
        qm_ref, km_ref, vm_ref, om_ref, qf_ref, kf_ref, vf_ref, gcol_ref, grow_ref = outs
    h = _rms(x_ref[...], g_ref[...]).astype(BF16)
    rows = h.shape[0]

    def mm(i):
        w = wt_ref[i * W_HEADS:(i + 1) * W_HEADS, :]
        return lax.dot_general(h, w, NT, preferred_element_type=F32)

    def mm_t(w):
        return lax.dot_general(w, h, NT, preferred_element_type=F32)

    qm_ref[...] = mm(0).astype(qm_ref.dtype)
    km_ref[...] = (mm(1) * (DH_M ** -0.5)).astype(km_ref.dtype)
    if kv_transposed:
        vm_ref[0] = mm_t(wt_ref[2 * W_HEADS:3 * W_HEADS, :]).astype(vm_ref.dtype)
        om_ref[0] = mm_t(wt_ref[3 * W_HEADS:4 * W_HEADS, :]).astype(om_ref.dtype)
    else:
        vm_ref[...] = mm(2).astype(vm_ref.dtype)
        om_ref[...] = mm(3).astype(om_ref.dtype)
    qf_ref[...] = (mm(4) * (DH_F ** -0.5 * LOG2E)).astype(qf_ref.dtype)
    if kv_transposed:
        kt = lax.dot_general(wkv_ref[0:W_HEADS, :], h, NT, preferred_element_type=F32)
        kf_ref[0] = kt
        kfb_ref[...] = kt.T.astype(BF16)
        vt = lax.dot_general(wkv_ref[W_HEADS:2 * W_HEADS, :], h, NT, preferred_element_type=F32)
        vf_ref[0] = vt
        vfb_ref[0] = vt.astype(BF16)
    else:
        kf_ref[...] = lax.dot_general(h, wkv_ref[0:W_HEADS, :], NT, preferred_element_type=F32)
        vf_ref[...] = lax.dot_general(h, wkv_ref[W_HEADS:2 * W_HEADS, :], NT, preferred_element_type=F32)
    pre_c = lax.dot_general(h, wg_ref[...], NT, preferred_element_type=F32) + bcol_ref[...]
    lane = lax.broadcasted_iota(jnp.int32, (rows, LANES), 1)
    gcol_ref[...] = jnp.where(lane < H_M, pre_c, _log_sigmoid(pre_c))
    pre_r = lax.dot_general(wg_ref[0:N_GATE_ROWS, :], h, NT, preferred_element_type=F32) + brow_ref[...]
    row = lax.broadcasted_iota(jnp.int32, (N_GATE_ROWS, rows), 0)
    grow_ref[...] = jnp.where(row < H_M, pre_r, _log_sigmoid(pre_r))


def _project(x2d, g, wt_main, wt_kv, wt_g, bcol, brow, *, batch, seq, kv_transposed, act_dtype):
    rows_total, d_model = x2d.shape
    tm = min(PROJ_ROWS, rows_total)
    steps = rows_total // tm
    per_seq = max(seq // tm, 1)
    row_spec = lambda w: pl.BlockSpec((tm, w), lambda i: (i, 0))
    in_specs = [row_spec(d_model), _const_spec(g.shape), _const_spec(wt_main.shape), _const_spec(wt_kv.shape),
                _const_spec(wt_g.shape), _const_spec(bcol.shape), _const_spec(brow.shape)]
    act = jax.ShapeDtypeStruct((rows_total, W_HEADS), act_dtype)
    act32 = jax.ShapeDtypeStruct((rows_total, W_HEADS), F32)
    out_shape = [act, act, act, act32, act]
    out_specs = [row_spec(W_HEADS)] * 5
    if kv_transposed:
        kv_spec = pl.BlockSpec((1, W_HEADS, tm), lambda i: (i // per_seq, 0, i % per_seq))
        out_shape[2] = jax.ShapeDtypeStruct((batch, W_HEADS, seq), act_dtype)
        out_shape[3] = jax.ShapeDtypeStruct((batch, W_HEADS, seq), F32)
        out_specs[2] = out_specs[3] = kv_spec
        out_shape += [jax.ShapeDtypeStruct((batch, W_HEADS, seq), F32)] * 2
        out_shape += [jax.ShapeDtypeStruct((rows_total, W_HEADS), BF16),
                      jax.ShapeDtypeStruct((batch, W_HEADS, seq), BF16)]
        out_specs += [kv_spec, kv_spec, row_spec(W_HEADS), kv_spec]
    else:
        out_shape += [act32, act32]
        out_specs += [row_spec(W_HEADS)] * 2
    out_shape += [jax.ShapeDtypeStruct((rows_total, LANES), F32),
                  jax.ShapeDtypeStruct((N_GATE_ROWS, rows_total), F32)]
    out_specs += [row_spec(LANES), pl.BlockSpec((N_GATE_ROWS, tm), lambda i: (0, i))]
    return pl.pallas_call(
        functools.partial(_proj_kernel, kv_transposed=kv_transposed),
        grid=(steps,),
        in_specs=in_specs,
        out_specs=out_specs,
        out_shape=out_shape,
        compiler_params=pltpu.CompilerParams(dimension_semantics=("arbitrary",), vmem_limit_bytes=48 * MIB),
        name="proj",
    )(x2d, g, wt_main, wt_kv, wt_g, bcol, brow)


def _mlstm_kernel(q_ref, k_ref, v_ref, om_ref, gc_ref, gr_ref, hg_ref, c0_ref, n0_ref, m0_ref,
                  hm_ref, c_ref, n_ref, m_ref, *, bb, chunk):
    @pl.when(pl.program_id(1) == 0)
    def _():
        c_ref[...] = c0_ref[...]
        n_ref[...] = n0_ref[...]
        m_ref[...] = m0_ref[...]

    t_idx = lax.broadcasted_iota(jnp.int32, (chunk, chunk), 0)
    s_idx = lax.broadcasted_iota(jnp.int32, (chunk, chunk), 1)
    causal = s_idx <= t_idx
    tril = causal.astype(F32)
    triu = (t_idx <= s_idx).astype(F32)

    def per_group(i, carry):
        bs = [i * MLSTM_INTERLEAVE + j for j in range(MLSTM_INTERLEAVE)]
        gc = [gc_ref[b] for b in bs]
        gr = [gr_ref[b] for b in bs]
        bc = [jnp.dot(tril, g, precision=HI, preferred_element_type=F32) for g in gc]
        br = [jnp.dot(g, triu, precision=HI, preferred_element_type=F32) for g in gr]
        n_all = [n_ref[b] for b in bs]
        m_all = [m_ref[pl.ds(b, 1), :] for b in bs]
        chains = [(j, h) for j in range(MLSTM_INTERLEAVE) for h in range(H_M)]
        ids = range(len(chains))
        sls = [slice(h * DH_M, (h + 1) * DH_M) for _, h in chains]
        c_prev = [c_ref[bs[j], h] for j, h in chains]
        n_prev = [n_all[j][h:h + 1, :] for j, h in chains]
        m_prev = [m_all[j][:, h:h + 1] for j, h in chains]
        qh = [q_ref[bs[j], :, sls[c]].astype(BF16) for c, (j, _) in enumerate(chains)]
        kh = [k_ref[bs[j], :, sls[c]].astype(BF16) for c, (j, _) in enumerate(chains)]
        vh = [v_ref[bs[j], :, sls[c]].astype(BF16) for c, (j, _) in enumerate(chains)]
        qk = [lax.dot_general(qh[c], kh[c], NT, preferred_element_type=F32) for c in ids]
        cq = [lax.dot_general(qh[c], c_prev[c].astype(BF16), NT, preferred_element_type=F32) for c in ids]
        li_c = [gc[j][:, h:h + 1] for j, h in chains]
        li_r = [gr[j][h:h + 1, :] for j, h in chains]
        b_c = [bc[j][:, H_M + h:H_M + h + 1] for j, h in chains]
        b_r = [br[j][H_M + h:H_M + h + 1, :] for j, h in chains]
        dmat = [jnp.where(causal, b_c[c] - b_r[c] + li_r[c], -jnp.inf) for c in ids]
        inter = [m_prev[c] + b_c[c] for c in ids]
        m_t = [jnp.maximum(inter[c], jnp.max(dmat[c], axis=-1, keepdims=True)) for c in ids]
        smat = [qk[c] * jnp.exp(dmat[c] - m_t[c]) for c in ids]
        w_inter = [jnp.exp(inter[c] - m_t[c]) for c in ids]
        sv = [jnp.dot(smat[c].astype(BF16), vh[c], preferred_element_type=F32) for c in ids]
        outs, vw, w_s, decay, m_new = [], [], [], [], []
        for c, (j, h) in enumerate(chains):
            num = w_inter[c] * cq[c] + sv[c]
            nq = jnp.sum(qh[c].astype(F32) * n_prev[c], axis=-1, keepdims=True)
            den = w_inter[c] * nq + jnp.sum(smat[c], axis=-1, keepdims=True)
            hh = num / jnp.maximum(jnp.abs(den), jnp.exp(-m_t[c]))
            hn = hh * lax.rsqrt(jnp.mean(hh * hh, axis=-1, keepdims=True) + EPS) * hg_ref[:, sls[c]]
            outs.append((hn * _sigmoid(om_ref[bs[j], :, sls[c]].astype(F32))).astype(hm_ref.dtype))
            b_last = b_c[c][chunk - 1:chunk, :]
            m_new.append(m_t[c][chunk - 1:chunk, :])
            decay.append(jnp.exp(m_prev[c] + b_last - m_new[c]))
            w_s.append(jnp.exp(li_c[c] + b_last - b_c[c] - m_new[c]))
            vw.append((vh[c].astype(F32) * w_s[c]).astype(BF16))
        for c, (j, h) in enumerate(chains):
            c_new = decay[c] * c_prev[c] + lax.dot_general(vw[c], kh[c], TN, preferred_element_type=F32)
            n_new = decay[c] * n_prev[c] + jnp.sum(kh[c].astype(F32) * w_s[c], axis=0, keepdims=True)
            hm_ref[bs[j], :, sls[c]] = outs[c]
            c_ref[bs[j], h] = c_new
            n_ref[bs[j], h:h + 1, :] = n_new
            m_ref[pl.ds(bs[j], 1), h:h + 1] = m_new[c]
        return carry

    lax.fori_loop(0, bb // MLSTM_INTERLEAVE, per_group, 0)


def _mlstm(qm, km, vm, om, gcol, grow3, head_g, c0, n0, m0, *, batch, seq, chunk, out_dtype):
    bb = MLSTM_BATCH
    n_chunks = seq // chunk
    as3 = lambda a: a.reshape(batch, seq, a.shape[-1])
    tok = lambda w: pl.BlockSpec((bb, chunk, w), lambda g, c: (g, c, 0))
    state = lambda shape: pl.BlockSpec((bb,) + shape, lambda g, c: (g,) + (0,) * len(shape))
    in_specs = [tok(W_HEADS), tok(W_HEADS), tok(W_HEADS), tok(W_HEADS), tok(LANES),
                pl.BlockSpec((bb, N_GATE_ROWS, chunk), lambda g, c: (g, 0, c)),
                _const_spec(head_g.shape),
                state((H_M, DH_M, DH_M)), state((H_M, DH_M)), state((H_M,))]
    out_specs = [tok(W_HEADS), state((H_M, DH_M, DH_M)), state((H_M, DH_M)), state((H_M,))]
    out_shape = [jax.ShapeDtypeStruct((batch, seq, W_HEADS), out_dtype),
                 jax.ShapeDtypeStruct(c0.shape, F32), jax.ShapeDtypeStruct(n0.shape, F32),
                 jax.ShapeDtypeStruct(m0.shape, F32)]
    hm, c_new, n_new, m_new = pl.pallas_call(
        functools.partial(_mlstm_kernel, bb=bb, chunk=chunk),
        grid=(batch // bb, n_chunks),
        in_specs=in_specs,
        out_specs=out_specs,
        out_shape=out_shape,
        compiler_params=pltpu.CompilerParams(dimension_semantics=("arbitrary", "arbitrary"),
                                             vmem_limit_bytes=48 * MIB),
        name="mlstm",
    )(as3(qm), as3(km), as3(vm), as3(om), as3(gcol), grow3, head_g, c0, n0, m0)
    return hm.reshape(batch * seq, W_HEADS), c_new, n_new, m_new


def _mlstm_t_kernel(q_ref, k_ref, vt_ref, omt_ref, gc_ref, gr_ref, hg_ref, c0_ref, n0_ref, m0_ref,
                    hm_ref, c_ref, n_ref, m_ref, *, bb, chunk):
    @pl.when(pl.program_id(1) == 0)
    def _():
        c_ref[...] = c0_ref[...]
        n_ref[...] = n0_ref[...]
        m_ref[...] = m0_ref[...]

    s_idx = lax.broadcasted_iota(jnp.int32, (chunk, chunk), 0)
    t_idx = lax.broadcasted_iota(jnp.int32, (chunk, chunk), 1)
    causal = s_idx <= t_idx
    tril = (t_idx <= s_idx).astype(F32)
    triu = causal.astype(F32)

    def per_group(i, carry):
        bs = [i * MLSTM_INTERLEAVE + j for j in range(MLSTM_INTERLEAVE)]
        gc = [gc_ref[b] for b in bs]
        gr = [gr_ref[b] for b in bs]
        bc = [jnp.dot(tril, g, precision=HI, preferred_element_type=F32) for g in gc]
        br = [jnp.dot(g, triu, precision=HI, preferred_element_type=F32) for g in gr]
        n_all = [n_ref[b] for b in bs]
        m_all = [m_ref[pl.ds(b, 1), :] for b in bs]
        chains = [(j, h) for j in range(MLSTM_INTERLEAVE) for h in range(H_M)]
        ids = range(len(chains))
        sls = [slice(h * DH_M, (h + 1) * DH_M) for _, h in chains]
        c_prev = [c_ref[bs[j], h] for j, h in chains]
        n_prev = [n_all[j][h:h + 1, :] for j, h in chains]
        m_prev = [m_all[j][:, h:h + 1] for j, h in chains]
        qh = [q_ref[bs[j], :, sls[c]].astype(BF16) for c, (j, _) in enumerate(chains)]
        kh = [k_ref[bs[j], :, sls[c]].astype(BF16) for c, (j, _) in enumerate(chains)]
        vth = [vt_ref[bs[j], sls[c], :].astype(BF16) for c, (j, _) in enumerate(chains)]
        qk = [lax.dot_general(kh[c], qh[c], NT, preferred_element_type=F32) for c in ids]
        cq = [lax.dot_general(c_prev[c].astype(BF16), qh[c], NT, preferred_element_type=F32) for c in ids]
        nq = [lax.dot_general(jnp.broadcast_to(n_prev[c], (8, DH_M)).astype(BF16), qh[c], NT,
                              preferred_element_type=F32)[0:1, :] for c in ids]
        li_r = [gr[j][h:h + 1, :] for j, h in chains]
        b_r = [br[j][H_M + h:H_M + h + 1, :] for j, h in chains]
        col = [gc[j][:, h:h + 1] - bc[j][:, H_M + h:H_M + h + 1] for j, h in chains]
        dmat = [jnp.where(causal, b_r[c] + col[c], -jnp.inf) for c in ids]
        inter = [m_prev[c] + b_r[c] for c in ids]
        m_t = [jnp.maximum(inter[c], jnp.max(dmat[c], axis=0, keepdims=True)) for c in ids]
        smat = [qk[c] * jnp.exp(dmat[c] - m_t[c]) for c in ids]
        w_inter = [jnp.exp(inter[c] - m_t[c]) for c in ids]
        sv = [jnp.dot(vth[c], smat[c].astype(BF16), preferred_element_type=F32) for c in ids]
        outs, vw, w_s_c, decay, m_new = [], [], [], [], []
        for c, (j, h) in enumerate(chains):
            num = w_inter[c] * cq[c] + sv[c]
            den = w_inter[c] * nq[c] + jnp.sum(smat[c], axis=0, keepdims=True)
            hh = num * (1.0 / jnp.maximum(jnp.abs(den), jnp.exp(-m_t[c])))
            hn = hh * lax.rsqrt(jnp.mean(hh * hh, axis=0, keepdims=True) + EPS) * hg_ref[sls[c], :]
            gate = _sigmoid(omt_ref[bs[j], sls[c], :].astype(F32))
            outs.append((hn * gate).T.astype(hm_ref.dtype))
            b_last = b_r[c][:, chunk - 1:chunk]
            m_new.append(m_t[c][:, chunk - 1:chunk])
            decay.append(jnp.exp(m_prev[c] + b_last - m_new[c]))
            w_s_r = jnp.exp(li_r[c] + b_last - b_r[c] - m_new[c])
            w_s_c.append(jnp.exp(col[c] + b_last - m_new[c]))
            vw.append((vth[c].astype(F32) * w_s_r).astype(BF16))
        for c, (j, h) in enumerate(chains):
            c_new = decay[c] * c_prev[c] + jnp.dot(vw[c], kh[c], preferred_element_type=F32)
            n_new = decay[c] * n_prev[c] + jnp.sum(kh[c].astype(F32) * w_s_c[c], axis=0, keepdims=True)
            hm_ref[bs[j], :, sls[c]] = outs[c]
            c_ref[bs[j], h] = c_new
            n_ref[bs[j], h:h + 1, :] = n_new
            m_ref[pl.ds(bs[j], 1), h:h + 1] = m_new[c]
        return carry

    lax.fori_loop(0, bb // MLSTM_INTERLEAVE, per_group, 0)


def _mlstm_prompt(qm, km, vmt, omt, gcol, grow3, head_g, c0, n0, m0, *, batch, seq, chunk, out_dtype):
    bb = MLSTM_BATCH
    n_chunks = seq // chunk
    as3 = lambda a: a.reshape(batch, seq, a.shape[-1])
    tok = lambda w: pl.BlockSpec((bb, chunk, w), lambda g, c: (g, c, 0))
    tok_t = pl.BlockSpec((bb, W_HEADS, chunk), lambda g, c: (g, 0, c))
    state = lambda shape: pl.BlockSpec((bb,) + shape, lambda g, c: (g,) + (0,) * len(shape))
    hg_col = head_g.reshape(W_HEADS, 1)
    in_specs = [tok(W_HEADS), tok(W_HEADS), tok_t, tok_t, tok(LANES),
                pl.BlockSpec((bb, N_GATE_ROWS, chunk), lambda g, c: (g, 0, c)),
                _const_spec(hg_col.shape),
                state((H_M, DH_M, DH_M)), state((H_M, DH_M)), state((H_M,))]
    out_specs = [tok(W_HEADS), state((H_M, DH_M, DH_M)), state((H_M, DH_M)), state((H_M,))]
    out_shape = [jax.ShapeDtypeStruct((batch, seq, W_HEADS), out_dtype),
                 jax.ShapeDtypeStruct(c0.shape, F32), jax.ShapeDtypeStruct(n0.shape, F32),
                 jax.ShapeDtypeStruct(m0.shape, F32)]
    hm, c_new, n_new, m_new = pl.pallas_call(
        functools.partial(_mlstm_t_kernel, bb=bb, chunk=chunk),
        grid=(batch // bb, n_chunks),
        in_specs=in_specs,
        out_specs=out_specs,
        out_shape=out_shape,
        compiler_params=pltpu.CompilerParams(dimension_semantics=("arbitrary", "arbitrary"),
                                             vmem_limit_bytes=48 * MIB),
        name="mlstm_prompt",
    )(as3(qm), as3(km), vmt, omt, as3(gcol), grow3, hg_col, c0, n0, m0)
    return hm.reshape(batch * seq, W_HEADS), c_new, n_new, m_new


def _fox_cumsum_kernel(gc_ref, gr_ref, cc_ref, cr_ref):
    seq = gc_ref.shape[1]
    t_idx = lax.broadcasted_iota(jnp.int32, (LANES, LANES), 0)
    s_idx = lax.broadcasted_iota(jnp.int32, (LANES, LANES), 1)
    tril = (s_idx <= t_idx).astype(F32)
    triu = (t_idx <= s_idx).astype(F32)
    blks = [slice(j * LANES, (j + 1) * LANES) for j in range(seq // LANES)]
    cbs = [jnp.dot(tril, gc_ref[0, blk, :], precision=HI, preferred_element_type=F32) for blk in blks]
    rbs = [jnp.dot(gr_ref[:, blk], triu, precision=HI, preferred_element_type=F32) for blk in blks]
    carry_c = jnp.zeros((1, LANES), F32)
    carry_r = jnp.zeros((N_GATE_ROWS, 1), F32)
    for blk, cb, rb in zip(blks, cbs, rbs):
        cc_ref[0, blk, :] = (cb + carry_c) * LOG2E
        cr_ref[:, blk] = (rb + carry_r) * LOG2E
        carry_c = carry_c + cb[LANES - 1:LANES, :]
        carry_r = carry_r + rb[:, LANES - 1:LANES]


def _fox_cumsum(gcol, grow, *, batch, seq):
    return pl.pallas_call(
        _fox_cumsum_kernel,
        grid=(batch,),
        in_specs=[pl.BlockSpec((1, seq, LANES), lambda b: (b, 0, 0)),
                  pl.BlockSpec((N_GATE_ROWS, seq), lambda b: (0, b))],
        out_specs=[pl.BlockSpec((1, seq, LANES), lambda b: (b, 0, 0)),
                   pl.BlockSpec((N_GATE_ROWS, seq), lambda b: (0, b))],
        out_shape=[jax.ShapeDtypeStruct((batch, seq, LANES), F32),
                   jax.ShapeDtypeStruct((N_GATE_ROWS, batch * seq), F32)],
        compiler_params=pltpu.CompilerParams(dimension_semantics=("arbitrary",)),
        name="fox_cumsum",
    )(gcol.reshape(batch, seq, LANES), grow)


def _fox_prompt_kernel(q_ref, k_ref, vt_ref, cc_ref, cr_ref, o_ref, qs_ref, m_ref, l_ref, acc_ref, *, blk, kb):
    i = pl.program_id(1)
    n_pairs = H_F // 2
    lane = lax.broadcasted_iota(jnp.int32, (blk, LANES), 1)
    low_half = lane < DH_F
    s_idx = lax.broadcasted_iota(jnp.int32, (kb, blk), 0)
    t_idx = lax.broadcasted_iota(jnp.int32, (kb, blk), 1)
    gate0 = N_GATE_ROWS - H_F
    q_cols = pl.ds(pl.multiple_of(i * blk, blk), blk)

    for pair in range(n_pairs):
        slab = slice(pair * LANES, (pair + 1) * LANES)
        q_pair = q_ref[:, slab]
        zero = jnp.zeros_like(q_pair)
        qs_ref[pair, 0:blk, :] = jnp.where(low_half, q_pair, zero)
        qs_ref[pair, blk:2 * blk, :] = jnp.where(low_half, zero, q_pair)
    m_ref[...] = jnp.full(m_ref.shape, -jnp.inf, F32)
    l_ref[...] = jnp.zeros(l_ref.shape, F32)
    acc_ref[...] = jnp.zeros(acc_ref.shape, F32)

    def step(j, diag_offset):
        keys = pl.ds(pl.multiple_of(j * kb, kb), kb)
        pairs = range(n_pairs)
        raw = [lax.dot_general(k_ref[keys, pair * LANES:(pair + 1) * LANES], qs_ref[pair], NT,
                               preferred_element_type=F32) for pair in pairs]
        pb, alpha = [], []
        for pair in pairs:
            ck = jnp.concatenate(
                [jnp.broadcast_to(cc_ref[keys, gate0 + 2 * pair + e:gate0 + 2 * pair + e + 1], (kb, blk))
                 for e in range(2)], axis=1)
            cq = jnp.concatenate(
                [cr_ref[gate0 + 2 * pair + e:gate0 + 2 * pair + e + 1, q_cols] for e in range(2)], axis=1)
            s = raw[pair] - ck
            if diag_offset is not None:
                causal = s_idx + diag_offset <= t_idx
                s = jnp.where(jnp.concatenate([causal] * 2, axis=1), s, -jnp.inf)
            m_old = m_ref[pair]
            m_new = jnp.maximum(m_old, jnp.max(s, axis=0, keepdims=True) + cq)
            p = jnp.exp2(s - (m_new - cq))
            alpha.append(jnp.exp2(m_old - m_new))
            l_ref[pair] = alpha[pair] * l_ref[pair] + jnp.sum(p, axis=0, keepdims=True)
            m_ref[pair] = m_new
            pb.append(p.astype(BF16))
        for pair in pairs:
            for e in range(2):
                own = slice(pair * LANES + e * DH_F, pair * LANES + (e + 1) * DH_F)
                lanes = slice(e * blk, (e + 1) * blk)
                pv = jnp.dot(vt_ref[0, own, keys], pb[pair][:, lanes], preferred_element_type=F32)
                acc_ref[own, :] = alpha[pair][:, lanes] * acc_ref[own, :] + pv

    def loop_body(j, carry):
        step(j, None)
        return carry

    per_q = blk // kb
    lax.fori_loop(0, i * per_q, loop_body, 0)
    for r in range(per_q):
        step(i * per_q + r, r * kb)
    inv_l = jnp.concatenate(
        [jnp.broadcast_to(1.0 / l_ref[h // 2][:, (h % 2) * blk:(h % 2 + 1) * blk], (DH_F, blk)) for h in range(H_F)],
        axis=0)
    o_ref[...] = (acc_ref[...] * inv_l).T.astype(o_ref.dtype)


def _fox_prompt(qf, kfr, vtb, ccol, crow, *, batch, seq):
    blk = FOX_BLOCK
    nq = seq // blk
    stat = pltpu.VMEM((H_F // 2, 1, 2 * blk), F32)
    return pl.pallas_call(
        functools.partial(_fox_prompt_kernel, blk=blk, kb=FOX_KEYS),
        grid=(batch, nq),
        in_specs=[pl.BlockSpec((blk, W_HEADS), lambda b, i: (b * nq + i, 0)),
                  pl.BlockSpec((seq, W_HEADS), lambda b, i: (b, 0)),
                  pl.BlockSpec((1, W_HEADS, seq), lambda b, i: (b, 0, 0)),
                  pl.BlockSpec((seq, LANES), lambda b, i: (b, 0)),
                  pl.BlockSpec((N_GATE_ROWS, seq), lambda b, i: (0, b))],
        out_specs=pl.BlockSpec((blk, W_HEADS), lambda b, i: (b * nq + i, 0)),
        out_shape=jax.ShapeDtypeStruct((batch * seq, W_HEADS), BF16),
        scratch_shapes=[pltpu.VMEM((H_F // 2, 2 * blk, LANES), BF16), stat, stat, pltpu.VMEM((W_HEADS, blk), F32)],
        compiler_params=pltpu.CompilerParams(dimension_semantics=("arbitrary", "arbitrary"),
                                             vmem_limit_bytes=48 * MIB),
        name="fox_prompt",
    )(qf, kfr, vtb, ccol.reshape(batch * seq, LANES), crow)


def _page_bias_kernel(pt_ref, lf_ref, o_ref):
    b = pl.program_id(0)
    n_pages = o_ref.shape[1]
    x = jnp.concatenate([lf_ref[pt_ref[b, p]] for p in range(n_pages)], axis=0)
    t_idx = lax.broadcasted_iota(jnp.int32, (LANES, LANES), 0)
    s_idx = lax.broadcasted_iota(jnp.int32, (LANES, LANES), 1)
    later = (t_idx > s_idx).astype(F32)
    within = jnp.dot(x, later, precision=HI, preferred_element_type=F32)
    total = jnp.sum(x, axis=-1, keepdims=True)
    run = jnp.zeros((H_F, LANES), F32)
    for p in range(n_pages - 1, -1, -1):
        rows = slice(p * H_F, (p + 1) * H_F)
        o_ref[0, p] = (within[rows] + run) * (-LOG2E)
        run = run + total[rows]


def _page_bias(page_table, lf_pool):
    batch, n_pages = page_table.shape
    return pl.pallas_call(
        _page_bias_kernel,
        grid_spec=pltpu.PrefetchScalarGridSpec(
            num_scalar_prefetch=1,
            grid=(batch,),
            in_specs=[_const_spec(lf_pool.shape)],
            out_specs=pl.BlockSpec((1, n_pages, H_F, LANES), lambda b, pt: (b, 0, 0, 0)),
        ),
        out_shape=jax.ShapeDtypeStruct((batch, n_pages, H_F, LANES), F32),
        compiler_params=pltpu.CompilerParams(dimension_semantics=("arbitrary",), vmem_limit_bytes=40 * MIB),
        name="page_bias",
    )(page_table, lf_pool)


def _paged_attention(pt_ref, q_ref, gr_ref, kn_ref, vn_ref, cp_ref, k_hbm, v_hbm, o_ref,
                     kbuf, vbuf, sem, acc_ref, *, n_groups, phases):
    b = pl.program_id(0)
    nb = pl.num_programs(0)
    group = PAGES_PER_GROUP
    t_new = q_ref.shape[0]
    n_rows = t_new * H_F

    def page_copies(bi, g, sl):
        cps = []
        for j in range(group):
            page = pt_ref[bi, g * group + j]
            cps.append(pltpu.make_async_copy(k_hbm.at[page], kbuf.at[sl, j], sem.at[sl, 0]))
            cps.append(pltpu.make_async_copy(v_hbm.at[page], vbuf.at[sl, j], sem.at[sl, 1]))
        return cps

    def start_group(bi, g, sl):
        for cp in page_copies(bi, g, sl):
            cp.start()

    @pl.when(b == 0)
    def _():
        for a in range(PAGE_SLOTS - 1):
            start_group(0, a, a)

    sub = lax.broadcasted_iota(jnp.int32, (H_F, W_HEADS), 0)
    lane = lax.broadcasted_iota(jnp.int32, (H_F, W_HEADS), 1)
    own_head = (lane // DH_F) == sub
    q = q_ref[...].astype(F32)
    qbd = jnp.concatenate(
        [jnp.where(own_head, jnp.broadcast_to(q[t:t + 1, :], (H_F, W_HEADS)), 0.0) for t in range(t_new)],
        axis=0).astype(BF16)

    gate0 = N_GATE_ROWS - H_F
    lf_new = gr_ref[0, gate0:N_GATE_ROWS, :]
    a_idx = lax.broadcasted_iota(jnp.int32, (t_new, t_new), 0)
    b_idx = lax.broadcasted_iota(jnp.int32, (t_new, t_new), 1)
    c_new = jnp.dot(lf_new, (a_idx <= b_idx).astype(F32), precision=HI, preferred_element_type=F32) * LOG2E
    cq = jnp.concatenate([c_new[:, t:t + 1] for t in range(t_new)], axis=0)

    acc_ref[...] = jnp.zeros_like(acc_ref)

    def body(g, carry, side_work):
        m_i, l_i = carry
        slot = g % PAGE_SLOTS
        ahead = g + (PAGE_SLOTS - 1)

        @pl.when(ahead < n_groups)
        def _():
            start_group(b, ahead, ahead % PAGE_SLOTS)

        @pl.when(jnp.logical_and(ahead >= n_groups, b + 1 < nb))
        def _():
            start_group(b + 1, ahead - n_groups, ahead % PAGE_SLOTS)

        for cp in page_copies(b, g, slot):
            cp.wait()
        side = side_work(g)

        kcat = jnp.concatenate([kbuf[slot, j].astype(BF16) for j in range(group)], axis=1)
        s = jnp.dot(qbd, kcat, preferred_element_type=F32)
        next(side, None)
        first = pl.multiple_of(g * group, group)
        cpg = cp_ref[0, pl.ds(first, group)]
        bias = jnp.concatenate(
            [jnp.broadcast_to(cpg[j][None], (t_new, H_F, LANES)).reshape(n_rows, LANES) for j in range(group)],
            axis=1)
        s = s + (cq - bias)
        m_new = jnp.maximum(m_i, jnp.max(s, axis=-1, keepdims=True))
        p = jnp.exp2(s - m_new)
        alpha = jnp.exp2(m_i - m_new)
        l_new = alpha * l_i + jnp.sum(p, axis=-1, keepdims=True)
        next(side, None)
        vcat = jnp.concatenate([vbuf[slot, j].T.astype(BF16) for j in range(group)], axis=0)
        pv = jnp.dot(p.astype(BF16), vcat, preferred_element_type=F32)
        next(side, None)
        acc_ref[...] = alpha * acc_ref[...] + pv
        return m_new, l_new

    carry = (jnp.full((n_rows, 1), -jnp.inf, F32), jnp.zeros((n_rows, 1), F32))
    start = 0
    for stop, side_work in phases:
        carry = lax.fori_loop(start, stop, functools.partial(body, side_work=side_work), carry)
        start = stop
    assert start == n_groups
    m_i, l_i = carry

    kn = kn_ref[...].astype(BF16)
    vn = vn_ref[...].astype(BF16)
    s = lax.dot_general(qbd, kn, NT, preferred_element_type=F32)
    ck = jnp.broadcast_to(c_new[None], (t_new, H_F, t_new)).reshape(n_rows, t_new)
    s = s + (cq - ck)
    r_idx = lax.broadcasted_iota(jnp.int32, (n_rows, t_new), 0)
    k_idx = lax.broadcasted_iota(jnp.int32, (n_rows, t_new), 1)
    s = jnp.where(k_idx <= r_idx // H_F, s, -jnp.inf)
    m_new = jnp.maximum(m_i, jnp.max(s, axis=-1, keepdims=True))
    p = jnp.exp2(s - m_new)
    alpha = jnp.exp2(m_i - m_new)
    l_fin = alpha * l_i + jnp.sum(p, axis=-1, keepdims=True)
    acc = alpha * acc_ref[...] + jnp.dot(p.astype(BF16), vn, preferred_element_type=F32)
    out = acc / l_fin
    o_ref[...] = jnp.concatenate(
        [jnp.sum(jnp.where(own_head, out[t * H_F:(t + 1) * H_F, :], 0.0), axis=0, keepdims=True)
         for t in range(t_new)], axis=0).astype(o_ref.dtype)


def _merge_residual(x_ref, hm_ref, hf_ref, wo_ref, g2_ref):
    x1 = (x_ref[...]
          + jnp.dot(hm_ref[...].astype(BF16), wo_ref[0:W_HEADS, :], preferred_element_type=F32)
          + jnp.dot(hf_ref[...].astype(BF16), wo_ref[W_HEADS:2 * W_HEADS, :], preferred_element_type=F32))
    return x1, _rms(x1, g2_ref[...]).astype(BF16)


def _ffn_sample_kernel(pt_ref, x_ref, hm_ref, hf_ref, wo_ref, g2_ref, wg_ref, wu_ref, wd_ref, g3_ref,
                       q_ref, gr_ref, kn_ref, vn_ref, cp_ref, k_hbm, v_hbm, y_ref, o_ref,
                       h_ref, x2_ref, kbuf, vbuf, sem, acc_ref, *, n_groups, final_norm):
    x1, h = _merge_residual(x_ref, hm_ref, hf_ref, wo_ref, g2_ref)
    h_ref[...] = h
    x2_ref[...] = x1
    ffn_chunks = wg_ref.shape[1] // FFN_CHUNK
    assert ffn_chunks <= n_groups

    def ffn_chunk(c):
        cols = pl.ds(pl.multiple_of(c * FFN_CHUNK, FFN_CHUNK), FFN_CHUNK)
        hh = h_ref[...]
        gate = jnp.dot(hh, wg_ref[:, cols], preferred_element_type=F32)
        yield
        up = jnp.dot(hh, wu_ref[:, cols], preferred_element_type=F32)
        act = (gate * _sigmoid(gate) * up).astype(BF16)
        yield
        x2_ref[...] += jnp.dot(act, wd_ref[cols, :], preferred_element_type=F32)
        yield

    phases = [(ffn_chunks, ffn_chunk), (n_groups, lambda g: iter(()))]
    _paged_attention(pt_ref, q_ref, gr_ref, kn_ref, vn_ref, cp_ref, k_hbm, v_hbm, o_ref, kbuf, vbuf, sem, acc_ref,
                     n_groups=n_groups, phases=phases)
    x2 = x2_ref[...]
    y_ref[...] = _rms(x2, g3_ref[...]) if final_norm else x2


def _merge_ffn_and_fox_sample(x2d, hm, hf, wo, g2, wg, wu, wd, g3, page_table, qf, grow3, k_new, v_new, page_bias,
                              k_pool, v_pool, *, final_norm):
    rows_total, d_model = x2d.shape
    batch, n_pages = page_table.shape
    t_new = qf.shape[0] // batch
    n_groups = n_pages // PAGES_PER_GROUP
    tm = rows_total // batch
    d_ff = wg.shape[1]
    assert n_pages % PAGES_PER_GROUP == 0 and n_groups % PAGE_SLOTS == 0
    assert rows_total % batch == 0 and tm % 16 == 0 and d_ff % FFN_CHUNK == 0 and W_HEADS % FFN_CHUNK == 0
    page_rows, page_len = k_pool.shape[1], k_pool.shape[2]
    row = lambda w: pl.BlockSpec((tm, w), lambda b, pt: (b, 0))
    tok = lambda w: pl.BlockSpec((t_new, w), lambda b, pt: (b, 0))
    return pl.pallas_call(
        functools.partial(_ffn_sample_kernel, n_groups=n_groups, final_norm=final_norm),
        grid_spec=pltpu.PrefetchScalarGridSpec(
            num_scalar_prefetch=1,
            grid=(batch,),
            in_specs=[row(d_model), row(W_HEADS), row(W_HEADS), _const_spec(wo.shape), _const_spec(g2.shape),
                      _const_spec(wg.shape), _const_spec(wu.shape), _const_spec(wd.shape), _const_spec(g3.shape),
                      tok(W_HEADS),
                      pl.BlockSpec((1, N_GATE_ROWS, t_new), lambda b, pt: (b, 0, 0)),
                      tok(W_HEADS), tok(W_HEADS),
                      pl.BlockSpec((1, n_pages, H_F, LANES), lambda b, pt: (b, 0, 0, 0)),
                      pl.BlockSpec(memory_space=pl.ANY), pl.BlockSpec(memory_space=pl.ANY)],
            out_specs=[row(d_model), tok(W_HEADS)],
            scratch_shapes=[pltpu.VMEM((tm, d_model), BF16),
                            pltpu.VMEM((tm, d_model), F32),
                            pltpu.VMEM((PAGE_SLOTS, PAGES_PER_GROUP, page_rows, page_len), F32),
                            pltpu.VMEM((PAGE_SLOTS, PAGES_PER_GROUP, page_rows, page_len), F32),
                            pltpu.SemaphoreType.DMA((PAGE_SLOTS, 2)),
                            pltpu.VMEM((t_new * H_F, W_HEADS), F32)],
        ),
        out_shape=[jax.ShapeDtypeStruct((rows_total, d_model), F32),
                   jax.ShapeDtypeStruct((batch * t_new, W_HEADS), F32)],
        compiler_params=pltpu.CompilerParams(dimension_semantics=("arbitrary",), vmem_limit_bytes=58 * MIB),
        name="ffn_and_fox_sample",
    )(page_table, x2d, hm, hf, wo, g2, wg, wu, wd, g3, qf, grow3, k_new, v_new, page_bias, k_pool, v_pool)


def _ffn_kernel(x_ref, hm_ref, hf_ref, wo_ref, g2_ref, wg_ref, wu_ref, wd_ref, g3_ref, y_ref, *, final_norm):
    x1, h = _merge_residual(x_ref, hm_ref, hf_ref, wo_ref, g2_ref)
    gate = jnp.dot(h, wg_ref[...], preferred_element_type=F32)
    up = jnp.dot(h, wu_ref[...], preferred_element_type=F32)
    act = (gate * _sigmoid(gate) * up).astype(BF16)
    x2 = x1 + jnp.dot(act, wd_ref[...], preferred_element_type=F32)
    y_ref[...] = _rms(x2, g3_ref[...]) if final_norm else x2


def _merge_ffn(x2d, hm, hf, wo, g2, wg, wu, wd, g3, *, final_norm):
    rows_total, d_model = x2d.shape
    tm = min(FFN_ROWS, rows_total)
    row_spec = lambda w: pl.BlockSpec((tm, w), lambda i: (i, 0))
    return pl.pallas_call(
        functools.partial(_ffn_kernel, final_norm=final_norm),
        grid=(rows_total // tm,),
        in_specs=[row_spec(d_model), row_spec(W_HEADS), row_spec(W_HEADS), _const_spec(wo.shape),
                  _const_spec(g2.shape), _const_spec(wg.shape), _const_spec(wu.shape), _const_spec(wd.shape),
                  _const_spec(g3.shape)],
        out_specs=row_spec(d_model),
        out_shape=jax.ShapeDtypeStruct((rows_total, d_model), F32),
        compiler_params=pltpu.CompilerParams(dimension_semantics=("arbitrary",), vmem_limit_bytes=56 * MIB),
        name="merge_ffn",
    )(x2d, hm, hf, wo, g2, wg, wu, wd, g3)


def kernel(x_prompt, x_sample, cache_fox_k, cache_fox_v, cache_fox_logf, page_table, state_mlstm_C,
           state_mlstm_n, state_mlstm_m, norm_mix_g, w_in, b_m_igate, b_m_fgate, b_f_fgate, mlstm_head_g,
           w_out, norm_ffn_g, w_gate, w_up, w_down, norm_final_g):
    depth = w_in.shape[0]
    batch, seq, d_model = x_prompt.shape
    dec_batch, dec_seq, _ = x_sample.shape
    n_pool, page_size = cache_fox_k.shape[1], cache_fox_k.shape[2]
    xp = x_prompt.reshape(batch * seq, d_model)
    xs = x_sample.reshape(dec_batch * dec_seq, d_model)
    g_final = norm_final_g.reshape(1, d_model)
    pk, pv, plf, pc, pn, pm = [], [], [], [], [], []
    sk, sv, slf, sc, sn, sm = [], [], [], [], [], []
    gate0 = N_GATE_ROWS - H_F
    o_gm = 4 * W_HEADS
    o_qf = o_gm + 2 * H_M
    o_gf = o_qf + 3 * W_HEADS
    for l in range(depth):
        wt = jnp.swapaxes(w_in[l], 0, 1)
        wt_main = jnp.concatenate([wt[0:o_gm], wt[o_qf:o_qf + W_HEADS]], axis=0).astype(BF16)
        wt_kv = wt[o_qf + W_HEADS:o_gf].astype(BF16)
        wt_g = jnp.concatenate([wt[o_gm:o_qf], wt[o_gf:o_gf + H_F],
                                jnp.zeros((LANES - N_GATE_ROWS, d_model), F32)], axis=0).astype(BF16)
        bias = jnp.concatenate([b_m_igate[l], b_m_fgate[l], b_f_fgate[l],
                                jnp.zeros((LANES - N_GATE_ROWS,), F32)]).astype(F32)
        bcol = bias.reshape(1, LANES)
        brow = bias[:N_GATE_ROWS].reshape(N_GATE_ROWS, 1)
        g_mix = norm_mix_g[l].reshape(1, d_model)
        g_ffn = norm_ffn_g[l].reshape(1, d_model)
        head_g = mlstm_head_g[l].reshape(1, W_HEADS)
        wo = w_out[l].astype(BF16)
        wg = w_gate[l].astype(BF16)
        wu = w_up[l].astype(BF16)
        wd = w_down[l].astype(BF16)

        qm, km, vmt, omt, qf, kt, vt, kfr, vtb, gcol, grow = _project(
            xp, g_mix, wt_main, wt_kv, wt_g, bcol, brow, batch=batch, seq=seq, kv_transposed=True, act_dtype=BF16)
        chunk = min(LANES, seq)
        grow3 = grow.reshape(N_GATE_ROWS, batch, seq).transpose(1, 0, 2)
        hm, c_p, n_p, m_p = _mlstm_prompt(
            qm, km, vmt, omt, gcol, grow3, head_g,
            jnp.zeros((batch, H_M, DH_M, DH_M), F32), jnp.zeros((batch, H_M, DH_M), F32),
            jnp.zeros((batch, H_M), F32), batch=batch, seq=seq, chunk=chunk, out_dtype=BF16)
        ccol, crow = _fox_cumsum(gcol, grow, batch=batch, seq=seq)
        hf = _fox_prompt(qf, kfr, vtb, ccol, crow, batch=batch, seq=seq)
        hm_p, hf_p = hm, hf
        pk.append(kt.reshape(batch, H_F, DH_F, seq).transpose(0, 3, 1, 2))
        pv.append(vt.reshape(batch, H_F, DH_F, seq).transpose(0, 3, 1, 2))
        plf.append(grow3[:, gate0:, :].transpose(0, 2, 1))
        pc.append(c_p); pn.append(n_p); pm.append(m_p)

        qm, km, vm, om, qf, k_new, v_new, gcol, grow = _project(
            xs, g_mix, wt_main, wt_kv, wt_g, bcol, brow, batch=dec_batch, seq=dec_seq, kv_transposed=False,
            act_dtype=F32)
        grow3 = grow.reshape(N_GATE_ROWS, dec_batch, dec_seq).transpose(1, 0, 2)
        hm, c_s, n_s, m_s = _mlstm(
            qm, km, vm, om, gcol, grow3, head_g,
            state_mlstm_C[l].astype(F32), state_mlstm_n[l].astype(F32), state_mlstm_m[l].astype(F32),
            batch=dec_batch, seq=dec_seq, chunk=dec_seq, out_dtype=F32)
        k_pool = cache_fox_k[l].transpose(0, 2, 3, 1).reshape(n_pool, W_HEADS, page_size)
        v_pool = cache_fox_v[l].transpose(0, 2, 3, 1).reshape(n_pool, W_HEADS, page_size)
        lf_pool = cache_fox_logf[l].transpose(0, 2, 1)
        page_bias = _page_bias(page_table, lf_pool)
        xp, hf = _merge_ffn_and_fox_sample(
            xp, hm_p, hf_p, wo, g_ffn, wg, wu, wd, g_final, page_table, qf, grow3, k_new, v_new, page_bias,
            k_pool, v_pool, final_norm=(l == depth - 1))
        xs = _merge_ffn(xs, hm, hf, wo, g_ffn, wg, wu, wd, g_final, final_norm=(l == depth - 1))
        sk.append(k_new.reshape(dec_batch, dec_seq, H_F, DH_F))
        sv.append(v_new.reshape(dec_batch, dec_seq, H_F, DH_F))
        slf.append(grow3[:, gate0:, :].transpose(0, 2, 1))
        sc.append(c_s); sn.append(n_s); sm.append(m_s)

    st = lambda a, ref: jnp.stack(a, axis=0).astype(ref.dtype)
    return (xp.reshape(batch, seq, d_model), xs.reshape(dec_batch, dec_seq, d_model),
            st(pk, cache_fox_k), st(pv, cache_fox_v), st(plf, cache_fox_logf),
            st(pc, state_mlstm_C), st(pn, state_mlstm_n), st(pm, state_mlstm_m),
            st(sk, cache_fox_k), st(sv, cache_fox_v), st(slf, cache_fox_logf),
            st(sc, state_mlstm_C), st(sn, state_mlstm_n), st(sm, state_mlstm_m))
```

```python
import functools

import jax
import jax.numpy as jnp
from jax import lax
from jax.experimental import pallas as pl
from jax.experimental.pallas import tpu as pltpu

F32 = jnp.float32
BF16 = jnp.bfloat16
HI = lax.Precision.HIGHEST
NT = (((1,), (1,)), ((), ()))
TN = (((0,), (0,)), ((), ()))

EPS = 1e-6
LOG2E = 1.4426950408889634
H_M = 4
DH_M = 128
H_F = 8
DH_F = 64
W_HEADS = 512
N_GATE_ROWS = 16
LANES = 128
MIB = 1024 * 1024
V7X_VMEM_BYTES = 64 * MIB
VMEM_LIMIT = V7X_VMEM_BYTES * 3 // 4
VMEM_LIMIT_WEIGHTS = V7X_VMEM_BYTES - 6 * MIB

PROJ_ROWS = 1024
FFN_ROWS = 512
FFN_CHUNK = 256
MLSTM_BATCH = 8
MLSTM_INTERLEAVE = 8
FOX_BLOCK = 512
PAGES_PER_GROUP = 8
PAGE_SLOTS = 4


def _rms(x, g):
    return x * lax.rsqrt(jnp.mean(x * x, axis=-1, keepdims=True) + EPS) * g


def _log_sigmoid(x):
    return jnp.minimum(x, 0.0) - jnp.log1p(jnp.exp(-jnp.abs(x)))


def _sigmoid(x):
    return 1.0 / (1.0 + jnp.exp(-x))


def _const_spec(shape):
    return pl.BlockSpec(shape, lambda *_: (0,) * len(shape), pipeline_mode=pl.Buffered(1))


def _proj_kernel(x_ref, g_ref, wt_ref, wkv_ref, wg_ref, bcol_ref, brow_ref, *outs, kv_transposed):
    if kv_transposed:
        qm_ref, km_ref, vm_ref, om_ref, qf_ref, kf_ref, vf_ref, kfb_ref, vfb_ref, gcol_ref, grow_ref = outs
    else:
        qm_ref, km_ref, vm_ref, om_ref, qf_ref, kf_ref, vf_ref, gcol_ref, grow_ref = outs
    h = _rms(x_ref[...], g_ref[...]).astype(BF16)
    rows = h.shape[0]

    def mm(i):
        w = wt_ref[i * W_HEADS:(i + 1) * W_HEADS, :]
        return lax.dot_general(h, w, NT, preferred_element_type=F32)

    def mm_t(w):
        return lax.dot_general(w, h, NT, preferred_element_type=F32)

    qm_ref[...] = mm(0).astype(qm_ref.dtype)
    km_ref[...] = (mm(1) * (DH_M ** -0.5)).astype(km_ref.dtype)
    if kv_transposed:
        vm_ref[0] = mm_t(wt_ref[2 * W_HEADS:3 * W_HEADS, :]).astype(vm_ref.dtype)
        om_ref[0] = mm_t(wt_ref[3 * W_HEADS:4 * W_HEADS, :]).astype(om_ref.dtype)
    else:
        vm_ref[...] = mm(2).astype(vm_ref.dtype)
        om_ref[...] = mm(3).astype(om_ref.dtype)
    qf_ref[...] = (mm(4) * (DH_F ** -0.5 * LOG2E)).astype(qf_ref.dtype)
    if kv_transposed:
        kt = lax.dot_general(wkv_ref[0:W_HEADS, :], h, NT, preferred_element_type=F32)
        kf_ref[0] = kt
        kfb_ref[...] = kt.T.astype(BF16)
        vt = lax.dot_general(wkv_ref[W_HEADS:2 * W_HEADS, :], h, NT, preferred_element_type=F32)
        vf_ref[0] = vt
        vfb_ref[0] = vt.astype(BF16)
    else:
        kf_ref[...] = lax.dot_general(h, wkv_ref[0:W_HEADS, :], NT, preferred_element_type=F32)
        vf_ref[...] = lax.dot_general(h, wkv_ref[W_HEADS:2 * W_HEADS, :], NT, preferred_element_type=F32)
    pre_c = lax.dot_general(h, wg_ref[...], NT, preferred_element_type=F32) + bcol_ref[...]
    lane = lax.broadcasted_iota(jnp.int32, (rows, LANES), 1)
    gcol_ref[...] = jnp.where(lane < H_M, pre_c, _log_sigmoid(pre_c))
    pre_r = lax.dot_general(wg_ref[0:N_GATE_ROWS, :], h, NT, preferred_element_type=F32) + brow_ref[...]
    row = lax.broadcasted_iota(jnp.int32, (N_GATE_ROWS, rows), 0)
    grow_ref[...] = jnp.where(row < H_M, pre_r, _log_sigmoid(pre_r))


def _project(x2d, g, wt_main, wt_kv, wt_g, bcol, brow, *, batch, seq, kv_transposed, act_dtype):
    rows_total, d_model = x2d.shape
    tm = min(PROJ_ROWS, rows_total)
    steps = rows_total // tm
    per_seq = max(seq // tm, 1)
    row_spec = lambda w: pl.BlockSpec((tm, w), lambda i: (i, 0))
    in_specs = [row_spec(d_model), _const_spec(g.shape), _const_spec(wt_main.shape), _const_spec(wt_kv.shape),
                _const_spec(wt_g.shape), _const_spec(bcol.shape), _const_spec(brow.shape)]
    act = jax.ShapeDtypeStruct((rows_total, W_HEADS), act_dtype)
    act32 = jax.ShapeDtypeStruct((rows_total, W_HEADS), F32)
    out_shape = [act, act, act, act32, act]
    out_specs = [row_spec(W_HEADS)] * 5
    if kv_transposed:
        kv_spec = pl.BlockSpec((1, W_HEADS, tm), lambda i: (i // per_seq, 0, i % per_seq))
        out_shape[2] = jax.ShapeDtypeStruct((batch, W_HEADS, seq), act_dtype)
        out_shape[3] = jax.ShapeDtypeStruct((batch, W_HEADS, seq), F32)
        out_specs[2] = out_specs[3] = kv_spec
        out_shape += [jax.ShapeDtypeStruct((batch, W_HEADS, seq), F32)] * 2
        out_shape += [jax.ShapeDtypeStruct((rows_total, W_HEADS), BF16),
                      jax.ShapeDtypeStruct((batch, W_HEADS, seq), BF16)]
        out_specs += [kv_spec, kv_spec, row_spec(W_HEADS), kv_spec]
    else:
        out_shape += [act32, act32]
        out_specs += [row_spec(W_HEADS)] * 2
    out_shape += [jax.ShapeDtypeStruct((rows_total, LANES), F32),
                  jax.ShapeDtypeStruct((N_GATE_ROWS, rows_total), F32)]
    out_specs += [row_spec(LANES), pl.BlockSpec((N_GATE_ROWS, tm), lambda i: (0, i))]
    return pl.pallas_call(
        functools.partial(_proj_kernel, kv_transposed=kv_transposed),
        grid=(steps,),
        in_specs=in_specs,
        out_specs=out_specs,
        out_shape=out_shape,
        compiler_params=pltpu.CompilerParams(dimension_semantics=("arbitrary",), vmem_limit_bytes=VMEM_LIMIT),
        name="proj",
    )(x2d, g, wt_main, wt_kv, wt_g, bcol, brow)


def _mlstm_kernel(q_ref, k_ref, v_ref, om_ref, gc_ref, gr_ref, hg_ref, c0_ref, n0_ref, m0_ref,
                  hm_ref, c_ref, n_ref, m_ref, *, bb, chunk):
    @pl.when(pl.program_id(1) == 0)
    def _():
        c_ref[...] = c0_ref[...]
        n_ref[...] = n0_ref[...]
        m_ref[...] = m0_ref[...]

    t_idx = lax.broadcasted_iota(jnp.int32, (chunk, chunk), 0)
    s_idx = lax.broadcasted_iota(jnp.int32, (chunk, chunk), 1)
    causal = s_idx <= t_idx
    tril = causal.astype(F32)
    triu = (t_idx <= s_idx).astype(F32)

    def per_group(i, carry):
        bs = [i * MLSTM_INTERLEAVE + j for j in range(MLSTM_INTERLEAVE)]
        gc = [gc_ref[b] for b in bs]
        gr = [gr_ref[b] for b in bs]
        bc = [jnp.dot(tril, g, precision=HI, preferred_element_type=F32) for g in gc]
        br = [jnp.dot(g, triu, precision=HI, preferred_element_type=F32) for g in gr]
        n_all = [n_ref[b] for b in bs]
        m_all = [m_ref[pl.ds(b, 1), :] for b in bs]
        chains = [(j, h) for j in range(MLSTM_INTERLEAVE) for h in range(H_M)]
        ids = range(len(chains))
        sls = [slice(h * DH_M, (h + 1) * DH_M) for _, h in chains]
        c_prev = [c_ref[bs[j], h] for j, h in chains]
        n_prev = [n_all[j][h:h + 1, :] for j, h in chains]
        m_prev = [m_all[j][:, h:h + 1] for j, h in chains]
        qh = [q_ref[bs[j], :, sls[c]].astype(BF16) for c, (j, _) in enumerate(chains)]
        kh = [k_ref[bs[j], :, sls[c]].astype(BF16) for c, (j, _) in enumerate(chains)]
        vh = [v_ref[bs[j], :, sls[c]].astype(BF16) for c, (j, _) in enumerate(chains)]
        qk = [lax.dot_general(qh[c], kh[c], NT, preferred_element_type=F32) for c in ids]
        cq = [lax.dot_general(qh[c], c_prev[c].astype(BF16), NT, preferred_element_type=F32) for c in ids]
        li_c = [gc[j][:, h:h + 1] for j, h in chains]
        li_r = [gr[j][h:h + 1, :] for j, h in chains]
        b_c = [bc[j][:, H_M + h:H_M + h + 1] for j, h in chains]
        b_r = [br[j][H_M + h:H_M + h + 1, :] for j, h in chains]
        dmat = [jnp.where(causal, b_c[c] - b_r[c] + li_r[c], -jnp.inf) for c in ids]
        inter = [m_prev[c] + b_c[c] for c in ids]
        m_t = [jnp.maximum(inter[c], jnp.max(dmat[c], axis=-1, keepdims=True)) for c in ids]
        smat = [qk[c] * jnp.exp(dmat[c] - m_t[c]) for c in ids]
        w_inter = [jnp.exp(inter[c] - m_t[c]) for c in ids]
        sv = [jnp.dot(smat[c].astype(BF16), vh[c], preferred_element_type=F32) for c in ids]
        outs, vw, w_s, decay, m_new = [], [], [], [], []
        for c, (j, h) in enumerate(chains):
            num = w_inter[c] * cq[c] + sv[c]
            nq = jnp.sum(qh[c].astype(F32) * n_prev[c], axis=-1, keepdims=True)
            den = w_inter[c] * nq + jnp.sum(smat[c], axis=-1, keepdims=True)
            hh = num / jnp.maximum(jnp.abs(den), jnp.exp(-m_t[c]))
            hn = hh * lax.rsqrt(jnp.mean(hh * hh, axis=-1, keepdims=True) + EPS) * hg_ref[:, sls[c]]
            outs.append((hn * _sigmoid(om_ref[bs[j], :, sls[c]].astype(F32))).astype(hm_ref.dtype))
            b_last = b_c[c][chunk - 1:chunk, :]
            m_new.append(m_t[c][chunk - 1:chunk, :])
            decay.append(jnp.exp(m_prev[c] + b_last - m_new[c]))
            w_s.append(jnp.exp(li_c[c] + b_last - b_c[c] - m_new[c]))
            vw.append((vh[c].astype(F32) * w_s[c]).astype(BF16))
        for c, (j, h) in enumerate(chains):
            c_new = decay[c] * c_prev[c] + lax.dot_general(vw[c], kh[c], TN, preferred_element_type=F32)
            n_new = decay[c] * n_prev[c] + jnp.sum(kh[c].astype(F32) * w_s[c], axis=0, keepdims=True)
            hm_ref[bs[j], :, sls[c]] = outs[c]
            c_ref[bs[j], h] = c_new
            n_ref[bs[j], h:h + 1, :] = n_new
            m_ref[pl.ds(bs[j], 1), h:h + 1] = m_new[c]
        return carry

    lax.fori_loop(0, bb // MLSTM_INTERLEAVE, per_group, 0)


def _mlstm(qm, km, vm, om, gcol, grow3, head_g, c0, n0, m0, *, batch, seq, chunk, out_dtype):
    bb = MLSTM_BATCH
    n_chunks = seq // chunk
    as3 = lambda a: a.reshape(batch, seq, a.shape[-1])
    tok = lambda w: pl.BlockSpec((bb, chunk, w), lambda g, c: (g, c, 0))
    state = lambda shape: pl.BlockSpec((bb,) + shape, lambda g, c: (g,) + (0,) * len(shape))
    in_specs = [tok(W_HEADS), tok(W_HEADS), tok(W_HEADS), tok(W_HEADS), tok(LANES),
                pl.BlockSpec((bb, N_GATE_ROWS, chunk), lambda g, c: (g, 0, c)),
                _const_spec(head_g.shape),
                state((H_M, DH_M, DH_M)), state((H_M, DH_M)), state((H_M,))]
    out_specs = [tok(W_HEADS), state((H_M, DH_M, DH_M)), state((H_M, DH_M)), state((H_M,))]
    out_shape = [jax.ShapeDtypeStruct((batch, seq, W_HEADS), out_dtype),
                 jax.ShapeDtypeStruct(c0.shape, F32), jax.ShapeDtypeStruct(n0.shape, F32),
                 jax.ShapeDtypeStruct(m0.shape, F32)]
    hm, c_new, n_new, m_new = pl.pallas_call(
        functools.partial(_mlstm_kernel, bb=bb, chunk=chunk),
        grid=(batch // bb, n_chunks),
        in_specs=in_specs,
        out_specs=out_specs,
        out_shape=out_shape,
        compiler_params=pltpu.CompilerParams(dimension_semantics=("arbitrary", "arbitrary"),
                                             vmem_limit_bytes=VMEM_LIMIT),
        name="mlstm",
    )(as3(qm), as3(km), as3(vm), as3(om), as3(gcol), grow3, head_g, c0, n0, m0)
    return hm.reshape(batch * seq, W_HEADS), c_new, n_new, m_new


def _mlstm_t_kernel(q_ref, k_ref, vt_ref, omt_ref, gc_ref, gr_ref, hg_ref, c0_ref, n0_ref, m0_ref,
                    hm_ref, c_ref, n_ref, m_ref, *, bb, chunk):
    @pl.when(pl.program_id(1) == 0)
    def _():
        c_ref[...] = c0_ref[...]
        n_ref[...] = n0_ref[...]
        m_ref[...] = m0_ref[...]

    s_idx = lax.broadcasted_iota(jnp.int32, (chunk, chunk), 0)
    t_idx = lax.broadcasted_iota(jnp.int32, (chunk, chunk), 1)
    causal = s_idx <= t_idx
    tril = (t_idx <= s_idx).astype(F32)
    triu = causal.astype(F32)

    def per_group(i, carry):
        bs = [i * MLSTM_INTERLEAVE + j for j in range(MLSTM_INTERLEAVE)]
        gc = [gc_ref[b] for b in bs]
        gr = [gr_ref[b] for b in bs]
        bc = [jnp.dot(tril, g, precision=HI, preferred_element_type=F32) for g in gc]
        br = [jnp.dot(g, triu, precision=HI, preferred_element_type=F32) for g in gr]
        n_all = [n_ref[b] for b in bs]
        m_all = [m_ref[pl.ds(b, 1), :] for b in bs]
        chains = [(j, h) for j in range(MLSTM_INTERLEAVE) for h in range(H_M)]
        ids = range(len(chains))
        sls = [slice(h * DH_M, (h + 1) * DH_M) for _, h in chains]
        c_prev = [c_ref[bs[j], h] for j, h in chains]
        n_prev = [n_all[j][h:h + 1, :] for j, h in chains]
        m_prev = [m_all[j][:, h:h + 1] for j, h in chains]
        qh = [q_ref[bs[j], :, sls[c]].astype(BF16) for c, (j, _) in enumerate(chains)]
        kh = [k_ref[bs[j], :, sls[c]].astype(BF16) for c, (j, _) in enumerate(chains)]
        vth = [vt_ref[bs[j], sls[c], :].astype(BF16) for c, (j, _) in enumerate(chains)]
        qk = [lax.dot_general(kh[c], qh[c], NT, preferred_element_type=F32) for c in ids]
        cq = [lax.dot_general(c_prev[c].astype(BF16), qh[c], NT, preferred_element_type=F32) for c in ids]
        nq = [lax.dot_general(jnp.broadcast_to(n_prev[c], (8, DH_M)).astype(BF16), qh[c], NT,
                              preferred_element_type=F32)[0:1, :] for c in ids]
        li_r = [gr[j][h:h + 1, :] for j, h in chains]
        b_r = [br[j][H_M + h:H_M + h + 1, :] for j, h in chains]
        col = [gc[j][:, h:h + 1] - bc[j][:, H_M + h:H_M + h + 1] for j, h in chains]
        dmat = [jnp.where(causal, b_r[c] + col[c], -jnp.inf) for c in ids]
        inter = [m_prev[c] + b_r[c] for c in ids]
        m_t = [jnp.maximum(inter[c], jnp.max(dmat[c], axis=0, keepdims=True)) for c in ids]
        smat = [qk[c] * jnp.exp(dmat[c] - m_t[c]) for c in ids]
        w_inter = [jnp.exp(inter[c] - m_t[c]) for c in ids]
        sv = [jnp.dot(vth[c], smat[c].astype(BF16), preferred_element_type=F32) for c in ids]
        outs, vw, w_s_c, decay, m_new = [], [], [], [], []
        for c, (j, h) in enumerate(chains):
            num = w_inter[c] * cq[c] + sv[c]
            den = w_inter[c] * nq[c] + jnp.sum(smat[c], axis=0, keepdims=True)
            hh = num * (1.0 / jnp.maximum(jnp.abs(den), jnp.exp(-m_t[c])))
            hn = hh * lax.rsqrt(jnp.mean(hh * hh, axis=0, keepdims=True) + EPS) * hg_ref[sls[c], :]
            gate = _sigmoid(omt_ref[bs[j], sls[c], :].astype(F32))
            outs.append((hn * gate).T.astype(hm_ref.dtype))
            b_last = b_r[c][:, chunk - 1:chunk]
            m_new.append(m_t[c][:, chunk - 1:chunk])
            decay.append(jnp.exp(m_prev[c] + b_last - m_new[c]))
            w_s_r = jnp.exp(li_r[c] + b_last - b_r[c] - m_new[c])
            w_s_c.append(jnp.exp(col[c] + b_last - m_new[c]))
            vw.append((vth[c].astype(F32) * w_s_r).astype(BF16))
        for c, (j, h) in enumerate(chains):
            c_new = decay[c] * c_prev[c] + jnp.dot(vw[c], kh[c], preferred_element_type=F32)
            n_new = decay[c] * n_prev[c] + jnp.sum(kh[c].astype(F32) * w_s_c[c], axis=0, keepdims=True)
            hm_ref[bs[j], :, sls[c]] = outs[c]
            c_ref[bs[j], h] = c_new
            n_ref[bs[j], h:h + 1, :] = n_new
            m_ref[pl.ds(bs[j], 1), h:h + 1] = m_new[c]
        return carry

    lax.fori_loop(0, bb // MLSTM_INTERLEAVE, per_group, 0)


def _mlstm_prompt(qm, km, vmt, omt, gcol, grow3, head_g, c0, n0, m0, *, batch, seq, chunk, out_dtype):
    bb = MLSTM_BATCH
    n_chunks = seq // chunk
    as3 = lambda a: a.reshape(batch, seq, a.shape[-1])
    tok = lambda w: pl.BlockSpec((bb, chunk, w), lambda g, c: (g, c, 0))
    tok_t = pl.BlockSpec((bb, W_HEADS, chunk), lambda g, c: (g, 0, c))
    state = lambda shape: pl.BlockSpec((bb,) + shape, lambda g, c: (g,) + (0,) * len(shape))
    hg_col = head_g.reshape(W_HEADS, 1)
    in_specs = [tok(W_HEADS), tok(W_HEADS), tok_t, tok_t, tok(LANES),
                pl.BlockSpec((bb, N_GATE_ROWS, chunk), lambda g, c: (g, 0, c)),
                _const_spec(hg_col.shape),
                state((H_M, DH_M, DH_M)), state((H_M, DH_M)), state((H_M,))]
    out_specs = [tok(W_HEADS), state((H_M, DH_M, DH_M)), state((H_M, DH_M)), state((H_M,))]
    out_shape = [jax.ShapeDtypeStruct((batch, seq, W_HEADS), out_dtype),
                 jax.ShapeDtypeStruct(c0.shape, F32), jax.ShapeDtypeStruct(n0.shape, F32),
                 jax.ShapeDtypeStruct(m0.shape, F32)]
    hm, c_new, n_new, m_new = pl.pallas_call(
        functools.partial(_mlstm_t_kernel, bb=bb, chunk=chunk),
        grid=(batch // bb, n_chunks),
        in_specs=in_specs,
        out_specs=out_specs,
        out_shape=out_shape,
        compiler_params=pltpu.CompilerParams(dimension_semantics=("arbitrary", "arbitrary"),
                                             vmem_limit_bytes=VMEM_LIMIT),
        name="mlstm_prompt",
    )(as3(qm), as3(km), vmt, omt, as3(gcol), grow3, hg_col, c0, n0, m0)
    return hm.reshape(batch * seq, W_HEADS), c_new, n_new, m_new


def _fox_cumsum_kernel(gc_ref, gr_ref, cc_ref, cr_ref):
    seq = gc_ref.shape[1]
    t_idx = lax.broadcasted_iota(jnp.int32, (LANES, LANES), 0)
    s_idx = lax.broadcasted_iota(jnp.int32, (LANES, LANES), 1)
    tril = (s_idx <= t_idx).astype(F32)
    triu = (t_idx <= s_idx).astype(F32)
    blks = [slice(j * LANES, (j + 1) * LANES) for j in range(seq // LANES)]
    cbs = [jnp.dot(tril, gc_ref[0, blk, :], precision=HI, preferred_element_type=F32) for blk in blks]
    rbs = [jnp.dot(gr_ref[:, blk], triu, precision=HI, preferred_element_type=F32) for blk in blks]
    carry_c = jnp.zeros((1, LANES), F32)
    carry_r = jnp.zeros((N_GATE_ROWS, 1), F32)
    for blk, cb, rb in zip(blks, cbs, rbs):
        cc_ref[0, blk, :] = (cb + carry_c) * LOG2E
        cr_ref[:, blk] = (rb + carry_r) * LOG2E
        carry_c = carry_c + cb[LANES - 1:LANES, :]
        carry_r = carry_r + rb[:, LANES - 1:LANES]


def _fox_cumsum(gcol, grow, *, batch, seq):
    return pl.pallas_call(
        _fox_cumsum_kernel,
        grid=(batch,),
        in_specs=[pl.BlockSpec((1, seq, LANES), lambda b: (b, 0, 0)),
                  pl.BlockSpec((N_GATE_ROWS, seq), lambda b: (0, b))],
        out_specs=[pl.BlockSpec((1, seq, LANES), lambda b: (b, 0, 0)),
                   pl.BlockSpec((N_GATE_ROWS, seq), lambda b: (0, b))],
        out_shape=[jax.ShapeDtypeStruct((batch, seq, LANES), F32),
                   jax.ShapeDtypeStruct((N_GATE_ROWS, batch * seq), F32)],
        compiler_params=pltpu.CompilerParams(dimension_semantics=("arbitrary",)),
        name="fox_cumsum",
    )(gcol.reshape(batch, seq, LANES), grow)


def _fox_prompt_kernel(q_ref, k_ref, vt_ref, cc_ref, cr_ref, o_ref, qs_ref, m_ref, l_ref, acc_ref, *, blk):
    i = pl.program_id(1)
    n_pairs = H_F // 2
    lane = lax.broadcasted_iota(jnp.int32, (blk, LANES), 1)
    low_half = lane < DH_F
    s_idx = lax.broadcasted_iota(jnp.int32, (blk, blk), 0)
    t_idx = lax.broadcasted_iota(jnp.int32, (blk, blk), 1)
    gate0 = N_GATE_ROWS - H_F
    q_cols = pl.ds(pl.multiple_of(i * blk, blk), blk)

    for pair in range(n_pairs):
        slab = slice(pair * LANES, (pair + 1) * LANES)
        q_pair = q_ref[:, slab]
        zero = jnp.zeros_like(q_pair)
        qs_ref[pair, 0:blk, :] = jnp.where(low_half, q_pair, zero)
        qs_ref[pair, blk:2 * blk, :] = jnp.where(low_half, zero, q_pair)
    m_ref[...] = jnp.full(m_ref.shape, -jnp.inf, F32)
    l_ref[...] = jnp.zeros(l_ref.shape, F32)
    acc_ref[...] = jnp.zeros(acc_ref.shape, F32)

    pairs = range(n_pairs)

    def key_rows(j):
        return pl.ds(pl.multiple_of(j * blk, blk), blk)

    def scores(j):
        return tuple(lax.dot_general(k_ref[key_rows(j), pair * LANES:(pair + 1) * LANES], qs_ref[pair], NT,
                                     preferred_element_type=F32) for pair in pairs)

    def softmax(j, raw, on_diagonal):
        pb, alpha = [], []
        for pair in pairs:
            ck = jnp.concatenate(
                [jnp.broadcast_to(cc_ref[key_rows(j), gate0 + 2 * pair + e:gate0 + 2 * pair + e + 1], (blk, blk))
                 for e in range(2)], axis=1)
            cq = jnp.concatenate(
                [cr_ref[gate0 + 2 * pair + e:gate0 + 2 * pair + e + 1, q_cols] for e in range(2)], axis=1)
            s = raw[pair] - ck
            if on_diagonal:
                s = jnp.where(jnp.concatenate([s_idx <= t_idx] * 2, axis=1), s, -jnp.inf)
            m_old = m_ref[pair]
            m_new = jnp.maximum(m_old, jnp.max(s, axis=0, keepdims=True) + cq)
            p = jnp.exp2(s - (m_new - cq))
            alpha.append(jnp.exp2(m_old - m_new))
            l_ref[pair] = alpha[pair] * l_ref[pair] + jnp.sum(p, axis=0, keepdims=True)
            m_ref[pair] = m_new
            pb.append(p.astype(BF16))
        return pb, alpha

    def values(j, pb, alpha):
        for pair in pairs:
            for e in range(2):
                own = slice(pair * LANES + e * DH_F, pair * LANES + (e + 1) * DH_F)
                lanes = slice(e * blk, (e + 1) * blk)
                pv = jnp.dot(vt_ref[0, own, key_rows(j)], pb[pair][:, lanes], preferred_element_type=F32)
                acc_ref[own, :] = alpha[pair][:, lanes] * acc_ref[own, :] + pv

    def key_block(j, on_diagonal):
        pb, alpha = softmax(j, scores(j), on_diagonal)
        values(j, pb, alpha)

    def loop_body(j, carry):
        key_block(j, False)
        return carry

    lax.fori_loop(0, i, loop_body, 0)
    key_block(i, True)
    inv_l = jnp.concatenate(
        [jnp.broadcast_to(1.0 / l_ref[h // 2][:, (h % 2) * blk:(h % 2 + 1) * blk], (DH_F, blk)) for h in range(H_F)],
        axis=0)
    o_ref[...] = (acc_ref[...] * inv_l).T.astype(o_ref.dtype)


def _fox_prompt(qf, kfr, vtb, ccol, crow, *, batch, seq):
    blk = FOX_BLOCK
    nq = seq // blk
    stat = pltpu.VMEM((H_F // 2, 1, 2 * blk), F32)
    return pl.pallas_call(
        functools.partial(_fox_prompt_kernel, blk=blk),
        grid=(batch, nq),
        in_specs=[pl.BlockSpec((blk, W_HEADS), lambda b, i: (b * nq + i, 0)),
                  pl.BlockSpec((seq, W_HEADS), lambda b, i: (b, 0)),
                  pl.BlockSpec((1, W_HEADS, seq), lambda b, i: (b, 0, 0)),
                  pl.BlockSpec((seq, LANES), lambda b, i: (b, 0)),
                  pl.BlockSpec((N_GATE_ROWS, seq), lambda b, i: (0, b))],
        out_specs=pl.BlockSpec((blk, W_HEADS), lambda b, i: (b * nq + i, 0)),
        out_shape=jax.ShapeDtypeStruct((batch * seq, W_HEADS), BF16),
        scratch_shapes=[pltpu.VMEM((H_F // 2, 2 * blk, LANES), BF16), stat, stat, pltpu.VMEM((W_HEADS, blk), F32)],
        compiler_params=pltpu.CompilerParams(dimension_semantics=("arbitrary", "arbitrary"),
                                             vmem_limit_bytes=VMEM_LIMIT),
        name="fox_prompt",
    )(qf, kfr, vtb, ccol.reshape(batch * seq, LANES), crow)


def _page_bias_kernel(pt_ref, lf_ref, o_ref):
    b = pl.program_id(0)
    n_pages = o_ref.shape[1]
    x = jnp.concatenate([lf_ref[pt_ref[b, p]] for p in range(n_pages)], axis=0)
    t_idx = lax.broadcasted_iota(jnp.int32, (LANES, LANES), 0)
    s_idx = lax.broadcasted_iota(jnp.int32, (LANES, LANES), 1)
    later = (t_idx > s_idx).astype(F32)
    within = jnp.dot(x, later, precision=HI, preferred_element_type=F32)
    total = jnp.sum(x, axis=-1, keepdims=True)
    run = jnp.zeros((H_F, LANES), F32)
    for p in range(n_pages - 1, -1, -1):
        rows = slice(p * H_F, (p + 1) * H_F)
        o_ref[0, p] = (within[rows] + run) * (-LOG2E)
        run = run + total[rows]


def _page_bias(page_table, lf_pool):
    batch, n_pages = page_table.shape
    return pl.pallas_call(
        _page_bias_kernel,
        grid_spec=pltpu.PrefetchScalarGridSpec(
            num_scalar_prefetch=1,
            grid=(batch,),
            in_specs=[_const_spec(lf_pool.shape)],
            out_specs=pl.BlockSpec((1, n_pages, H_F, LANES), lambda b, pt: (b, 0, 0, 0)),
        ),
        out_shape=jax.ShapeDtypeStruct((batch, n_pages, H_F, LANES), F32),
        compiler_params=pltpu.CompilerParams(dimension_semantics=("arbitrary",), vmem_limit_bytes=VMEM_LIMIT),
        name="page_bias",
    )(page_table, lf_pool)


def _paged_attention(pt_ref, q_ref, gr_ref, kn_ref, vn_ref, cp_ref, k_hbm, v_hbm, o_ref,
                     kbuf, vbuf, sem, acc_ref, *, n_groups, phases):
    b = pl.program_id(0)
    nb = pl.num_programs(0)
    group = PAGES_PER_GROUP
    t_new = q_ref.shape[0]
    n_rows = t_new * H_F

    def page_copies(bi, g, sl):
        cps = []
        for j in range(group):
            page = pt_ref[bi, g * group + j]
            cps.append(pltpu.make_async_copy(k_hbm.at[page], kbuf.at[sl, j], sem.at[sl, 0]))
            cps.append(pltpu.make_async_copy(v_hbm.at[page], vbuf.at[sl, j], sem.at[sl, 1]))
        return cps

    def start_group(bi, g, sl):
        for cp in page_copies(bi, g, sl):
            cp.start()

    @pl.when(b == 0)
    def _():
        for a in range(PAGE_SLOTS - 1):
            start_group(0, a, a)

    sub = lax.broadcasted_iota(jnp.int32, (H_F, W_HEADS), 0)
    lane = lax.broadcasted_iota(jnp.int32, (H_F, W_HEADS), 1)
    own_head = (lane // DH_F) == sub
    q = q_ref[...].astype(F32)
    qbd = jnp.concatenate(
        [jnp.where(own_head, jnp.broadcast_to(q[t:t + 1, :], (H_F, W_HEADS)), 0.0) for t in range(t_new)],
        axis=0).astype(BF16)

    gate0 = N_GATE_ROWS - H_F
    lf_new = gr_ref[0, gate0:N_GATE_ROWS, :]
    a_idx = lax.broadcasted_iota(jnp.int32, (t_new, t_new), 0)
    b_idx = lax.broadcasted_iota(jnp.int32, (t_new, t_new), 1)
    c_new = jnp.dot(lf_new, (a_idx <= b_idx).astype(F32), precision=HI, preferred_element_type=F32) * LOG2E
    cq = jnp.concatenate([c_new[:, t:t + 1] for t in range(t_new)], axis=0)

    acc_ref[...] = jnp.zeros_like(acc_ref)

    def body(g, carry, side_work):
        m_i, l_i = carry
        slot = g % PAGE_SLOTS
        ahead = g + (PAGE_SLOTS - 1)

        @pl.when(ahead < n_groups)
        def _():
            start_group(b, ahead, ahead % PAGE_SLOTS)

        @pl.when(jnp.logical_and(ahead >= n_groups, b + 1 < nb))
        def _():
            start_group(b + 1, ahead - n_groups, ahead % PAGE_SLOTS)

        for cp in page_copies(b, g, slot):
            cp.wait()
        side = side_work(g)

        kcat = jnp.concatenate([kbuf[slot, j].astype(BF16) for j in range(group)], axis=1)
        s = jnp.dot(qbd, kcat, preferred_element_type=F32)
        next(side, None)
        first = pl.multiple_of(g * group, group)
        cpg = cp_ref[0, pl.ds(first, group)]
        bias = jnp.concatenate(
            [jnp.broadcast_to(cpg[j][None], (t_new, H_F, LANES)).reshape(n_rows, LANES) for j in range(group)],
            axis=1)
        s = s + (cq - bias)
        m_new = jnp.maximum(m_i, jnp.max(s, axis=-1, keepdims=True))
        p = jnp.exp2(s - m_new)
        alpha = jnp.exp2(m_i - m_new)
        l_new = alpha * l_i + jnp.sum(p, axis=-1, keepdims=True)
        next(side, None)
        vcat = jnp.concatenate([vbuf[slot, j].T.astype(BF16) for j in range(group)], axis=0)
        pv = jnp.dot(p.astype(BF16), vcat, preferred_element_type=F32)
        next(side, None)
        acc_ref[...] = alpha * acc_ref[...] + pv
        return m_new, l_new

    carry = (jnp.full((n_rows, 1), -jnp.inf, F32), jnp.zeros((n_rows, 1), F32))
    start = 0
    for stop, side_work in phases:
        carry = lax.fori_loop(start, stop, functools.partial(body, side_work=side_work), carry)
        start = stop
    assert start == n_groups
    m_i, l_i = carry

    kn = kn_ref[...].astype(BF16)
    vn = vn_ref[...].astype(BF16)
    s = lax.dot_general(qbd, kn, NT, preferred_element_type=F32)
    ck = jnp.broadcast_to(c_new[None], (t_new, H_F, t_new)).reshape(n_rows, t_new)
    s = s + (cq - ck)
    r_idx = lax.broadcasted_iota(jnp.int32, (n_rows, t_new), 0)
    k_idx = lax.broadcasted_iota(jnp.int32, (n_rows, t_new), 1)
    s = jnp.where(k_idx <= r_idx // H_F, s, -jnp.inf)
    m_new = jnp.maximum(m_i, jnp.max(s, axis=-1, keepdims=True))
    p = jnp.exp2(s - m_new)
    alpha = jnp.exp2(m_i - m_new)
    l_fin = alpha * l_i + jnp.sum(p, axis=-1, keepdims=True)
    acc = alpha * acc_ref[...] + jnp.dot(p.astype(BF16), vn, preferred_element_type=F32)
    out = acc / l_fin
    o_ref[...] = jnp.concatenate(
        [jnp.sum(jnp.where(own_head, out[t * H_F:(t + 1) * H_F, :], 0.0), axis=0, keepdims=True)
         for t in range(t_new)], axis=0).astype(o_ref.dtype)


def _merge_residual(x_ref, hm_ref, hf_ref, wo_ref, g2_ref):
    x1 = (x_ref[...]
          + jnp.dot(hm_ref[...].astype(BF16), wo_ref[0:W_HEADS, :], preferred_element_type=F32)
          + jnp.dot(hf_ref[...].astype(BF16), wo_ref[W_HEADS:2 * W_HEADS, :], preferred_element_type=F32))
    return x1, _rms(x1, g2_ref[...]).astype(BF16)


def _ffn_sample_kernel(pt_ref, x_ref, hm_ref, hf_ref, wo_ref, g2_ref, wg_ref, wu_ref, wd_ref, g3_ref,
                       q_ref, gr_ref, kn_ref, vn_ref, cp_ref, k_hbm, v_hbm, y_ref, o_ref,
                       h_ref, x2_ref, kbuf, vbuf, sem, acc_ref, *, n_groups, final_norm):
    x1, h = _merge_residual(x_ref, hm_ref, hf_ref, wo_ref, g2_ref)
    h_ref[...] = h
    x2_ref[...] = x1
    ffn_chunks = wg_ref.shape[1] // FFN_CHUNK
    assert ffn_chunks <= n_groups

    def ffn_chunk(c):
        cols = pl.ds(pl.multiple_of(c * FFN_CHUNK, FFN_CHUNK), FFN_CHUNK)
        hh = h_ref[...]
        gate = jnp.dot(hh, wg_ref[:, cols], preferred_element_type=F32)
        yield
        up = jnp.dot(hh, wu_ref[:, cols], preferred_element_type=F32)
        act = (gate * _sigmoid(gate) * up).astype(BF16)
        yield
        x2_ref[...] += jnp.dot(act, wd_ref[cols, :], preferred_element_type=F32)
        yield

    phases = [(ffn_chunks, ffn_chunk), (n_groups, lambda g: iter(()))]
    _paged_attention(pt_ref, q_ref, gr_ref, kn_ref, vn_ref, cp_ref, k_hbm, v_hbm, o_ref, kbuf, vbuf, sem, acc_ref,
                     n_groups=n_groups, phases=phases)
    x2 = x2_ref[...]
    y_ref[...] = _rms(x2, g3_ref[...]) if final_norm else x2


def _merge_ffn_and_fox_sample(x2d, hm, hf, wo, g2, wg, wu, wd, g3, page_table, qf, grow3, k_new, v_new, page_bias,
                              k_pool, v_pool, *, final_norm):
    rows_total, d_model = x2d.shape
    batch, n_pages = page_table.shape
    t_new = qf.shape[0] // batch
    n_groups = n_pages // PAGES_PER_GROUP
    tm = rows_total // batch
    d_ff = wg.shape[1]
    assert n_pages % PAGES_PER_GROUP == 0 and n_groups % PAGE_SLOTS == 0
    assert rows_total % batch == 0 and tm % 16 == 0 and d_ff % FFN_CHUNK == 0 and W_HEADS % FFN_CHUNK == 0
    page_rows, page_len = k_pool.shape[1], k_pool.shape[2]
    row = lambda w: pl.BlockSpec((tm, w), lambda b, pt: (b, 0))
    tok = lambda w: pl.BlockSpec((t_new, w), lambda b, pt: (b, 0))
    return pl.pallas_call(
        functools.partial(_ffn_sample_kernel, n_groups=n_groups, final_norm=final_norm),
        grid_spec=pltpu.PrefetchScalarGridSpec(
            num_scalar_prefetch=1,
            grid=(batch,),
            in_specs=[row(d_model), row(W_HEADS), row(W_HEADS), _const_spec(wo.shape), _const_spec(g2.shape),
                      _const_spec(wg.shape), _const_spec(wu.shape), _const_spec(wd.shape), _const_spec(g3.shape),
                      tok(W_HEADS),
                      pl.BlockSpec((1, N_GATE_ROWS, t_new), lambda b, pt: (b, 0, 0)),
                      tok(W_HEADS), tok(W_HEADS),
                      pl.BlockSpec((1, n_pages, H_F, LANES), lambda b, pt: (b, 0, 0, 0)),
                      pl.BlockSpec(memory_space=pl.ANY), pl.BlockSpec(memory_space=pl.ANY)],
            out_specs=[row(d_model), tok(W_HEADS)],
            scratch_shapes=[pltpu.VMEM((tm, d_model), BF16),
                            pltpu.VMEM((tm, d_model), F32),
                            pltpu.VMEM((PAGE_SLOTS, PAGES_PER_GROUP, page_rows, page_len), F32),
                            pltpu.VMEM((PAGE_SLOTS, PAGES_PER_GROUP, page_rows, page_len), F32),
                            pltpu.SemaphoreType.DMA((PAGE_SLOTS, 2)),
                            pltpu.VMEM((t_new * H_F, W_HEADS), F32)],
        ),
        out_shape=[jax.ShapeDtypeStruct((rows_total, d_model), F32),
                   jax.ShapeDtypeStruct((batch * t_new, W_HEADS), F32)],
        compiler_params=pltpu.CompilerParams(dimension_semantics=("arbitrary",), vmem_limit_bytes=VMEM_LIMIT_WEIGHTS),
        name="ffn_and_fox_sample",
    )(page_table, x2d, hm, hf, wo, g2, wg, wu, wd, g3, qf, grow3, k_new, v_new, page_bias, k_pool, v_pool)


def _ffn_kernel(x_ref, hm_ref, hf_ref, wo_ref, g2_ref, wg_ref, wu_ref, wd_ref, g3_ref, y_ref, *, final_norm):
    x1, h = _merge_residual(x_ref, hm_ref, hf_ref, wo_ref, g2_ref)
    gate = jnp.dot(h, wg_ref[...], preferred_element_type=F32)
    up = jnp.dot(h, wu_ref[...], preferred_element_type=F32)
    act = (gate * _sigmoid(gate) * up).astype(BF16)
    x2 = x1 + jnp.dot(act, wd_ref[...], preferred_element_type=F32)
    y_ref[...] = _rms(x2, g3_ref[...]) if final_norm else x2


def _merge_ffn(x2d, hm, hf, wo, g2, wg, wu, wd, g3, *, final_norm):
    rows_total, d_model = x2d.shape
    tm = min(FFN_ROWS, rows_total)
    row_spec = lambda w: pl.BlockSpec((tm, w), lambda i: (i, 0))
    return pl.pallas_call(
        functools.partial(_ffn_kernel, final_norm=final_norm),
        grid=(rows_total // tm,),
        in_specs=[row_spec(d_model), row_spec(W_HEADS), row_spec(W_HEADS), _const_spec(wo.shape),
                  _const_spec(g2.shape), _const_spec(wg.shape), _const_spec(wu.shape), _const_spec(wd.shape),
                  _const_spec(g3.shape)],
        out_specs=row_spec(d_model),
        out_shape=jax.ShapeDtypeStruct((rows_total, d_model), F32),
        compiler_params=pltpu.CompilerParams(dimension_semantics=("arbitrary",), vmem_limit_bytes=VMEM_LIMIT_WEIGHTS),
        name="merge_ffn",
    )(x2d, hm, hf, wo, g2, wg, wu, wd, g3)


def kernel(x_prompt, x_sample, cache_fox_k, cache_fox_v, cache_fox_logf, page_table, state_mlstm_C,
           state_mlstm_n, state_mlstm_m, norm_mix_g, w_in, b_m_igate, b_m_fgate, b_f_fgate, mlstm_head_g,
           w_out, norm_ffn_g, w_gate, w_up, w_down, norm_final_g):
    depth = w_in.shape[0]
    batch, seq, d_model = x_prompt.shape
    dec_batch, dec_seq, _ = x_sample.shape
    n_pool, page_size = cache_fox_k.shape[1], cache_fox_k.shape[2]
    xp = x_prompt.reshape(batch * seq, d_model)
    xs = x_sample.reshape(dec_batch * dec_seq, d_model)
    g_final = norm_final_g.reshape(1, d_model)
    pk, pv, plf, pc, pn, pm = [], [], [], [], [], []
    sk, sv, slf, sc, sn, sm = [], [], [], [], [], []
    gate0 = N_GATE_ROWS - H_F
    o_gm = 4 * W_HEADS
    o_qf = o_gm + 2 * H_M
    o_gf = o_qf + 3 * W_HEADS
    for l in range(depth):
        wt = jnp.swapaxes(w_in[l], 0, 1)
        wt_main = jnp.concatenate([wt[0:o_gm], wt[o_qf:o_qf + W_HEADS]], axis=0).astype(BF16)
        wt_kv = wt[o_qf + W_HEADS:o_gf].astype(BF16)
        wt_g = jnp.concatenate([wt[o_gm:o_qf], wt[o_gf:o_gf + H_F],
                                jnp.zeros((LANES - N_GATE_ROWS, d_model), F32)], axis=0).astype(BF16)
        bias = jnp.concatenate([b_m_igate[l], b_m_fgate[l], b_f_fgate[l],
                                jnp.zeros((LANES - N_GATE_ROWS,), F32)]).astype(F32)
        bcol = bias.reshape(1, LANES)
        brow = bias[:N_GATE_ROWS].reshape(N_GATE_ROWS, 1)
        g_mix = norm_mix_g[l].reshape(1, d_model)
        g_ffn = norm_ffn_g[l].reshape(1, d_model)
        head_g = mlstm_head_g[l].reshape(1, W_HEADS)
        wo = w_out[l].astype(BF16)
        wg = w_gate[l].astype(BF16)
        wu = w_up[l].astype(BF16)
        wd = w_down[l].astype(BF16)

        qm, km, vmt, omt, qf, kt, vt, kfr, vtb, gcol, grow = _project(
            xp, g_mix, wt_main, wt_kv, wt_g, bcol, brow, batch=batch, seq=seq, kv_transposed=True, act_dtype=BF16)
        chunk = min(LANES, seq)
        grow3 = grow.reshape(N_GATE_ROWS, batch, seq).transpose(1, 0, 2)
        hm, c_p, n_p, m_p = _mlstm_prompt(
            qm, km, vmt, omt, gcol, grow3, head_g,
            jnp.zeros((batch, H_M, DH_M, DH_M), F32), jnp.zeros((batch, H_M, DH_M), F32),
            jnp.zeros((batch, H_M), F32), batch=batch, seq=seq, chunk=chunk, out_dtype=BF16)
        ccol, crow = _fox_cumsum(gcol, grow, batch=batch, seq=seq)
        hf = _fox_prompt(qf, kfr, vtb, ccol, crow, batch=batch, seq=seq)
        hm_p, hf_p = hm, hf
        pk.append(kt.reshape(batch, H_F, DH_F, seq).transpose(0, 3, 1, 2))
        pv.append(vt.reshape(batch, H_F, DH_F, seq).transpose(0, 3, 1, 2))
        plf.append(grow3[:, gate0:, :].transpose(0, 2, 1))
        pc.append(c_p); pn.append(n_p); pm.append(m_p)

        qm, km, vm, om, qf, k_new, v_new, gcol, grow = _project(
            xs, g_mix, wt_main, wt_kv, wt_g, bcol, brow, batch=dec_batch, seq=dec_seq, kv_transposed=False,
            act_dtype=F32)
        grow3 = grow.reshape(N_GATE_ROWS, dec_batch, dec_seq).transpose(1, 0, 2)
        hm, c_s, n_s, m_s = _mlstm(
            qm, km, vm, om, gcol, grow3, head_g,
            state_mlstm_C[l].astype(F32), state_mlstm_n[l].astype(F32), state_mlstm_m[l].astype(F32),
            batch=dec_batch, seq=dec_seq, chunk=dec_seq, out_dtype=F32)
        k_pool = cache_fox_k[l].transpose(0, 2, 3, 1).reshape(n_pool, W_HEADS, page_size)
        v_pool = cache_fox_v[l].transpose(0, 2, 3, 1).reshape(n_pool, W_HEADS, page_size)
        lf_pool = cache_fox_logf[l].transpose(0, 2, 1)
        page_bias = _page_bias(page_table, lf_pool)
        xp, hf = _merge_ffn_and_fox_sample(
            xp, hm_p, hf_p, wo, g_ffn, wg, wu, wd, g_final, page_table, qf, grow3, k_new, v_new, page_bias,
            k_pool, v_pool, final_norm=(l == depth - 1))
        xs = _merge_ffn(xs, hm, hf, wo, g_ffn, wg, wu, wd, g_final, final_norm=(l == depth - 1))
        sk.append(k_new.reshape(dec_batch, dec_seq, H_F, DH_F))
        sv.append(v_new.reshape(dec_batch, dec_seq, H_F, DH_F))
        slf.append(grow3[:, gate0:, :].transpose(0, 2, 1))
        sc.append(c_s); sn.append(n_s); sm.append(m_s)

    st = lambda a, ref: jnp.stack(a, axis=0).astype(ref.dtype)
    return (xp.reshape(batch, seq, d_model), xs.reshape(dec_batch, dec_seq, d_model),
            st(pk, cache_fox_k), st(pv, cache_fox_v), st(plf, cache_fox_logf),
            st(pc, state_mlstm_C), st(pn, state_mlstm_n), st(pm, state_mlstm_m),
            st(sk, cache_fox_k), st(sv, cache_fox_v), st(slf, cache_fox_logf),
            st(sc, state_mlstm_C), st(sn, state_mlstm_n), st(sm, state_mlstm_m))
```

```python
import functools

import jax
import jax.numpy as jnp
from jax import lax
from jax.experimental import pallas as pl
from jax.experimental.pallas import tpu as pltpu

F32 = jnp.float32
BF16 = jnp.bfloat16
HI = lax.Precision.HIGHEST
NT = (((1,), (1,)), ((), ()))
TN = (((0,), (0,)), ((), ()))

EPS = 1e-6
LOG2E = 1.4426950408889634
H_M = 4
DH_M = 128
H_F = 8
DH_F = 64
W_HEADS = 512
N_GATE_ROWS = 16
LANES = 128
MIB = 1024 * 1024
V7X_VMEM_BYTES = 64 * MIB
VMEM_LIMIT = V7X_VMEM_BYTES * 3 // 4
VMEM_LIMIT_WEIGHTS = V7X_VMEM_BYTES - 6 * MIB

PROJ_ROWS = 1024
FFN_ROWS = 512
FFN_CHUNK = 256
DOWN_CHUNK = 512
MLSTM_BATCH = 8
MLSTM_INTERLEAVE = 8
FOX_BLOCK = 512
PAGES_PER_GROUP = 8
PAGE_SLOTS = 4


def _rms(x, g):
    return x * lax.rsqrt(jnp.mean(x * x, axis=-1, keepdims=True) + EPS) * g


def _log_sigmoid(x):
    return jnp.minimum(x, 0.0) - jnp.log1p(jnp.exp(-jnp.abs(x)))


def _sigmoid(x):
    return 1.0 / (1.0 + jnp.exp(-x))


def _const_spec(shape):
    return pl.BlockSpec(shape, lambda *_: (0,) * len(shape), pipeline_mode=pl.Buffered(1))


def _proj_kernel(x_ref, g_ref, wt_ref, wkv_ref, wg_ref, bcol_ref, brow_ref, *outs, kv_transposed):
    if kv_transposed:
        qm_ref, km_ref, vm_ref, om_ref, qf_ref, kf_ref, vf_ref, kfb_ref, vfb_ref, gcol_ref, grow_ref = outs
    else:
        qm_ref, km_ref, vm_ref, om_ref, qf_ref, kf_ref, vf_ref, gcol_ref, grow_ref = outs
    h = _rms(x_ref[...], g_ref[...]).astype(BF16)
    rows = h.shape[0]

    def mm(i):
        w = wt_ref[i * W_HEADS:(i + 1) * W_HEADS, :]
        return lax.dot_general(h, w, NT, preferred_element_type=F32)

    def mm_t(w):
        return lax.dot_general(w, h, NT, preferred_element_type=F32)

    qm_ref[...] = mm(0).astype(qm_ref.dtype)
    km_ref[...] = (mm(1) * (DH_M ** -0.5)).astype(km_ref.dtype)
    if kv_transposed:
        vm_ref[0] = mm_t(wt_ref[2 * W_HEADS:3 * W_HEADS, :]).astype(vm_ref.dtype)
        om_ref[0] = mm_t(wt_ref[3 * W_HEADS:4 * W_HEADS, :]).astype(om_ref.dtype)
    else:
        vm_ref[...] = mm(2).astype(vm_ref.dtype)
        om_ref[...] = mm(3).astype(om_ref.dtype)
    qf_ref[...] = (mm(4) * (DH_F ** -0.5 * LOG2E)).astype(qf_ref.dtype)
    if kv_transposed:
        kt = lax.dot_general(wkv_ref[0:W_HEADS, :], h, NT, preferred_element_type=F32)
        kf_ref[0] = kt
        kfb_ref[...] = kt.T.astype(BF16)
        vt = lax.dot_general(wkv_ref[W_HEADS:2 * W_HEADS, :], h, NT, preferred_element_type=F32)
        vf_ref[0] = vt
        vfb_ref[0] = vt.astype(BF16)
    else:
        kf_ref[...] = lax.dot_general(h, wkv_ref[0:W_HEADS, :], NT, preferred_element_type=F32)
        vf_ref[...] = lax.dot_general(h, wkv_ref[W_HEADS:2 * W_HEADS, :], NT, preferred_element_type=F32)
    pre_c = lax.dot_general(h, wg_ref[...], NT, preferred_element_type=F32) + bcol_ref[...]
    lane = lax.broadcasted_iota(jnp.int32, (rows, LANES), 1)
    gcol_ref[...] = jnp.where(lane < H_M, pre_c, _log_sigmoid(pre_c))
    pre_r = lax.dot_general(wg_ref[0:N_GATE_ROWS, :], h, NT, preferred_element_type=F32) + brow_ref[...]
    row = lax.broadcasted_iota(jnp.int32, (N_GATE_ROWS, rows), 0)
    grow_ref[...] = jnp.where(row < H_M, pre_r, _log_sigmoid(pre_r))


def _project(x2d, g, wt_main, wt_kv, wt_g, bcol, brow, *, batch, seq, kv_transposed, act_dtype):
    rows_total, d_model = x2d.shape
    tm = min(PROJ_ROWS, rows_total)
    steps = rows_total // tm
    per_seq = max(seq // tm, 1)
    row_spec = lambda w: pl.BlockSpec((tm, w), lambda i: (i, 0))
    in_specs = [row_spec(d_model), _const_spec(g.shape), _const_spec(wt_main.shape), _const_spec(wt_kv.shape),
                _const_spec(wt_g.shape), _const_spec(bcol.shape), _const_spec(brow.shape)]
    act = jax.ShapeDtypeStruct((rows_total, W_HEADS), act_dtype)
    act32 = jax.ShapeDtypeStruct((rows_total, W_HEADS), F32)
    out_shape = [act, act, act, act32, act]
    out_specs = [row_spec(W_HEADS)] * 5
    if kv_transposed:
        kv_spec = pl.BlockSpec((1, W_HEADS, tm), lambda i: (i // per_seq, 0, i % per_seq))
        out_shape[2] = jax.ShapeDtypeStruct((batch, W_HEADS, seq), act_dtype)
        out_shape[3] = jax.ShapeDtypeStruct((batch, W_HEADS, seq), F32)
        out_specs[2] = out_specs[3] = kv_spec
        out_shape += [jax.ShapeDtypeStruct((batch, W_HEADS, seq), F32)] * 2
        out_shape += [jax.ShapeDtypeStruct((rows_total, W_HEADS), BF16),
                      jax.ShapeDtypeStruct((batch, W_HEADS, seq), BF16)]
        out_specs += [kv_spec, kv_spec, row_spec(W_HEADS), kv_spec]
    else:
        out_shape += [act32, act32]
        out_specs += [row_spec(W_HEADS)] * 2
    out_shape += [jax.ShapeDtypeStruct((rows_total, LANES), F32),
                  jax.ShapeDtypeStruct((N_GATE_ROWS, rows_total), F32)]
    out_specs += [row_spec(LANES), pl.BlockSpec((N_GATE_ROWS, tm), lambda i: (0, i))]
    return pl.pallas_call(
        functools.partial(_proj_kernel, kv_transposed=kv_transposed),
        grid=(steps,),
        in_specs=in_specs,
        out_specs=out_specs,
        out_shape=out_shape,
        compiler_params=pltpu.CompilerParams(dimension_semantics=("arbitrary",), vmem_limit_bytes=VMEM_LIMIT),
        name="proj",
    )(x2d, g, wt_main, wt_kv, wt_g, bcol, brow)


def _mlstm_kernel(q_ref, k_ref, v_ref, om_ref, gc_ref, gr_ref, hg_ref, c0_ref, n0_ref, m0_ref,
                  hm_ref, c_ref, n_ref, m_ref, *, bb, chunk):
    @pl.when(pl.program_id(1) == 0)
    def _():
        c_ref[...] = c0_ref[...]
        n_ref[...] = n0_ref[...]
        m_ref[...] = m0_ref[...]

    t_idx = lax.broadcasted_iota(jnp.int32, (chunk, chunk), 0)
    s_idx = lax.broadcasted_iota(jnp.int32, (chunk, chunk), 1)
    causal = s_idx <= t_idx
    tril = causal.astype(F32)
    triu = (t_idx <= s_idx).astype(F32)

    def per_group(i, carry):
        bs = [i * MLSTM_INTERLEAVE + j for j in range(MLSTM_INTERLEAVE)]
        gc = [gc_ref[b] for b in bs]
        gr = [gr_ref[b] for b in bs]
        bc = [jnp.dot(tril, g, precision=HI, preferred_element_type=F32) for g in gc]
        br = [jnp.dot(g, triu, precision=HI, preferred_element_type=F32) for g in gr]
        n_all = [n_ref[b] for b in bs]
        m_all = [m_ref[pl.ds(b, 1), :] for b in bs]
        chains = [(j, h) for j in range(MLSTM_INTERLEAVE) for h in range(H_M)]
        ids = range(len(chains))
        sls = [slice(h * DH_M, (h + 1) * DH_M) for _, h in chains]
        c_prev = [c_ref[bs[j], h] for j, h in chains]
        n_prev = [n_all[j][h:h + 1, :] for j, h in chains]
        m_prev = [m_all[j][:, h:h + 1] for j, h in chains]
        qh = [q_ref[bs[j], :, sls[c]].astype(BF16) for c, (j, _) in enumerate(chains)]
        kh = [k_ref[bs[j], :, sls[c]].astype(BF16) for c, (j, _) in enumerate(chains)]
        vh = [v_ref[bs[j], :, sls[c]].astype(BF16) for c, (j, _) in enumerate(chains)]
        qk = [lax.dot_general(qh[c], kh[c], NT, preferred_element_type=F32) for c in ids]
        cq = [lax.dot_general(qh[c], c_prev[c].astype(BF16), NT, preferred_element_type=F32) for c in ids]
        li_c = [gc[j][:, h:h + 1] for j, h in chains]
        li_r = [gr[j][h:h + 1, :] for j, h in chains]
        b_c = [bc[j][:, H_M + h:H_M + h + 1] for j, h in chains]
        b_r = [br[j][H_M + h:H_M + h + 1, :] for j, h in chains]
        dmat = [jnp.where(causal, b_c[c] - b_r[c] + li_r[c], -jnp.inf) for c in ids]
        inter = [m_prev[c] + b_c[c] for c in ids]
        m_t = [jnp.maximum(inter[c], jnp.max(dmat[c], axis=-1, keepdims=True)) for c in ids]
        smat = [qk[c] * jnp.exp(dmat[c] - m_t[c]) for c in ids]
        w_inter = [jnp.exp(inter[c] - m_t[c]) for c in ids]
        sv = [jnp.dot(smat[c].astype(BF16), vh[c], preferred_element_type=F32) for c in ids]
        outs, vw, w_s, decay, m_new = [], [], [], [], []
        for c, (j, h) in enumerate(chains):
            num = w_inter[c] * cq[c] + sv[c]
            nq = jnp.sum(qh[c].astype(F32) * n_prev[c], axis=-1, keepdims=True)
            den = w_inter[c] * nq + jnp.sum(smat[c], axis=-1, keepdims=True)
            hh = num / jnp.maximum(jnp.abs(den), jnp.exp(-m_t[c]))
            hn = hh * lax.rsqrt(jnp.mean(hh * hh, axis=-1, keepdims=True) + EPS) * hg_ref[:, sls[c]]
            outs.append((hn * _sigmoid(om_ref[bs[j], :, sls[c]].astype(F32))).astype(hm_ref.dtype))
            b_last = b_c[c][chunk - 1:chunk, :]
            m_new.append(m_t[c][chunk - 1:chunk, :])
            decay.append(jnp.exp(m_prev[c] + b_last - m_new[c]))
            w_s.append(jnp.exp(li_c[c] + b_last - b_c[c] - m_new[c]))
            vw.append((vh[c].astype(F32) * w_s[c]).astype(BF16))
        for c, (j, h) in enumerate(chains):
            c_new = decay[c] * c_prev[c] + lax.dot_general(vw[c], kh[c], TN, preferred_element_type=F32)
            n_new = decay[c] * n_prev[c] + jnp.sum(kh[c].astype(F32) * w_s[c], axis=0, keepdims=True)
            hm_ref[bs[j], :, sls[c]] = outs[c]
            c_ref[bs[j], h] = c_new
            n_ref[bs[j], h:h + 1, :] = n_new
            m_ref[pl.ds(bs[j], 1), h:h + 1] = m_new[c]
        return carry

    lax.fori_loop(0, bb // MLSTM_INTERLEAVE, per_group, 0)


def _mlstm(qm, km, vm, om, gcol, grow3, head_g, c0, n0, m0, *, batch, seq, chunk, out_dtype):
    bb = MLSTM_BATCH
    n_chunks = seq // chunk
    as3 = lambda a: a.reshape(batch, seq, a.shape[-1])
    tok = lambda w: pl.BlockSpec((bb, chunk, w), lambda g, c: (g, c, 0))
    state = lambda shape: pl.BlockSpec((bb,) + shape, lambda g, c: (g,) + (0,) * len(shape))
    in_specs = [tok(W_HEADS), tok(W_HEADS), tok(W_HEADS), tok(W_HEADS), tok(LANES),
                pl.BlockSpec((bb, N_GATE_ROWS, chunk), lambda g, c: (g, 0, c)),
                _const_spec(head_g.shape),
                state((H_M, DH_M, DH_M)), state((H_M, DH_M)), state((H_M,))]
    out_specs = [tok(W_HEADS), state((H_M, DH_M, DH_M)), state((H_M, DH_M)), state((H_M,))]
    out_shape = [jax.ShapeDtypeStruct((batch, seq, W_HEADS), out_dtype),
                 jax.ShapeDtypeStruct(c0.shape, F32), jax.ShapeDtypeStruct(n0.shape, F32),
                 jax.ShapeDtypeStruct(m0.shape, F32)]
    hm, c_new, n_new, m_new = pl.pallas_call(
        functools.partial(_mlstm_kernel, bb=bb, chunk=chunk),
        grid=(batch // bb, n_chunks),
        in_specs=in_specs,
        out_specs=out_specs,
        out_shape=out_shape,
        compiler_params=pltpu.CompilerParams(dimension_semantics=("arbitrary", "arbitrary"),
                                             vmem_limit_bytes=VMEM_LIMIT),
        name="mlstm",
    )(as3(qm), as3(km), as3(vm), as3(om), as3(gcol), grow3, head_g, c0, n0, m0)
    return hm.reshape(batch * seq, W_HEADS), c_new, n_new, m_new


def _mlstm_t_kernel(q_ref, k_ref, vt_ref, omt_ref, gc_ref, gr_ref, hg_ref, c0_ref, n0_ref, m0_ref,
                    hm_ref, c_ref, n_ref, m_ref, *, bb, chunk):
    @pl.when(pl.program_id(1) == 0)
    def _():
        c_ref[...] = c0_ref[...]
        n_ref[...] = n0_ref[...]
        m_ref[...] = m0_ref[...]

    s_idx = lax.broadcasted_iota(jnp.int32, (chunk, chunk), 0)
    t_idx = lax.broadcasted_iota(jnp.int32, (chunk, chunk), 1)
    causal = s_idx <= t_idx
    tril = (t_idx <= s_idx).astype(F32)
    triu = causal.astype(F32)

    def per_group(i, carry):
        bs = [i * MLSTM_INTERLEAVE + j for j in range(MLSTM_INTERLEAVE)]
        gc = [gc_ref[b] for b in bs]
        gr = [gr_ref[b] for b in bs]
        bc = [jnp.dot(tril, g, precision=HI, preferred_element_type=F32) for g in gc]
        br = [jnp.dot(g, triu, precision=HI, preferred_element_type=F32) for g in gr]
        n_all = [n_ref[b] for b in bs]
        m_all = [m_ref[pl.ds(b, 1), :] for b in bs]
        chains = [(j, h) for j in range(MLSTM_INTERLEAVE) for h in range(H_M)]
        ids = range(len(chains))
        sls = [slice(h * DH_M, (h + 1) * DH_M) for _, h in chains]
        c_prev = [c_ref[bs[j], h] for j, h in chains]
        n_prev = [n_all[j][h:h + 1, :] for j, h in chains]
        m_prev = [m_all[j][:, h:h + 1] for j, h in chains]
        qh = [q_ref[bs[j], :, sls[c]].astype(BF16) for c, (j, _) in enumerate(chains)]
        kh = [k_ref[bs[j], :, sls[c]].astype(BF16) for c, (j, _) in enumerate(chains)]
        vth = [vt_ref[bs[j], sls[c], :].astype(BF16) for c, (j, _) in enumerate(chains)]
        qk = [lax.dot_general(kh[c], qh[c], NT, preferred_element_type=F32) for c in ids]
        cq = [lax.dot_general(c_prev[c].astype(BF16), qh[c], NT, preferred_element_type=F32) for c in ids]
        nq = [lax.dot_general(jnp.broadcast_to(n_prev[c], (8, DH_M)).astype(BF16), qh[c], NT,
                              preferred_element_type=F32)[0:1, :] for c in ids]
        li_r = [gr[j][h:h + 1, :] for j, h in chains]
        b_r = [br[j][H_M + h:H_M + h + 1, :] for j, h in chains]
        col = [gc[j][:, h:h + 1] - bc[j][:, H_M + h:H_M + h + 1] for j, h in chains]
        dmat = [jnp.where(causal, b_r[c] + col[c], -jnp.inf) for c in ids]
        inter = [m_prev[c] + b_r[c] for c in ids]
        m_t = [jnp.maximum(inter[c], jnp.max(dmat[c], axis=0, keepdims=True)) for c in ids]
        smat = [qk[c] * jnp.exp(dmat[c] - m_t[c]) for c in ids]
        w_inter = [jnp.exp(inter[c] - m_t[c]) for c in ids]
        sv = [jnp.dot(vth[c], smat[c].astype(BF16), preferred_element_type=F32) for c in ids]
        outs, vw, w_s_c, decay, m_new = [], [], [], [], []
        for c, (j, h) in enumerate(chains):
            num = w_inter[c] * cq[c] + sv[c]
            den = w_inter[c] * nq[c] + jnp.sum(smat[c], axis=0, keepdims=True)
            hh = num * (1.0 / jnp.maximum(jnp.abs(den), jnp.exp(-m_t[c])))
            hn = hh * lax.rsqrt(jnp.mean(hh * hh, axis=0, keepdims=True) + EPS) * hg_ref[sls[c], :]
            gate = _sigmoid(omt_ref[bs[j], sls[c], :].astype(F32))
            outs.append((hn * gate).T.astype(hm_ref.dtype))
            b_last = b_r[c][:, chunk - 1:chunk]
            m_new.append(m_t[c][:, chunk - 1:chunk])
            decay.append(jnp.exp(m_prev[c] + b_last - m_new[c]))
            w_s_r = jnp.exp(li_r[c] + b_last - b_r[c] - m_new[c])
            w_s_c.append(jnp.exp(col[c] + b_last - m_new[c]))
            vw.append((vth[c].astype(F32) * w_s_r).astype(BF16))
        for c, (j, h) in enumerate(chains):
            c_new = decay[c] * c_prev[c] + jnp.dot(vw[c], kh[c], preferred_element_type=F32)
            n_new = decay[c] * n_prev[c] + jnp.sum(kh[c].astype(F32) * w_s_c[c], axis=0, keepdims=True)
            hm_ref[bs[j], :, sls[c]] = outs[c]
            c_ref[bs[j], h] = c_new
            n_ref[bs[j], h:h + 1, :] = n_new
            m_ref[pl.ds(bs[j], 1), h:h + 1] = m_new[c]
        return carry

    lax.fori_loop(0, bb // MLSTM_INTERLEAVE, per_group, 0)


def _mlstm_prompt(qm, km, vmt, omt, gcol, grow3, head_g, c0, n0, m0, *, batch, seq, chunk, out_dtype):
    bb = MLSTM_BATCH
    n_chunks = seq // chunk
    as3 = lambda a: a.reshape(batch, seq, a.shape[-1])
    tok = lambda w: pl.BlockSpec((bb, chunk, w), lambda g, c: (g, c, 0))
    tok_t = pl.BlockSpec((bb, W_HEADS, chunk), lambda g, c: (g, 0, c))
    state = lambda shape: pl.BlockSpec((bb,) + shape, lambda g, c: (g,) + (0,) * len(shape))
    hg_col = head_g.reshape(W_HEADS, 1)
    in_specs = [tok(W_HEADS), tok(W_HEADS), tok_t, tok_t, tok(LANES),
                pl.BlockSpec((bb, N_GATE_ROWS, chunk), lambda g, c: (g, 0, c)),
                _const_spec(hg_col.shape),
                state((H_M, DH_M, DH_M)), state((H_M, DH_M)), state((H_M,))]
    out_specs = [tok(W_HEADS), state((H_M, DH_M, DH_M)), state((H_M, DH_M)), state((H_M,))]
    out_shape = [jax.ShapeDtypeStruct((batch, seq, W_HEADS), out_dtype),
                 jax.ShapeDtypeStruct(c0.shape, F32), jax.ShapeDtypeStruct(n0.shape, F32),
                 jax.ShapeDtypeStruct(m0.shape, F32)]
    hm, c_new, n_new, m_new = pl.pallas_call(
        functools.partial(_mlstm_t_kernel, bb=bb, chunk=chunk),
        grid=(batch // bb, n_chunks),
        in_specs=in_specs,
        out_specs=out_specs,
        out_shape=out_shape,
        compiler_params=pltpu.CompilerParams(dimension_semantics=("arbitrary", "arbitrary"),
                                             vmem_limit_bytes=VMEM_LIMIT),
        name="mlstm_prompt",
    )(as3(qm), as3(km), vmt, omt, as3(gcol), grow3, hg_col, c0, n0, m0)
    return hm.reshape(batch * seq, W_HEADS), c_new, n_new, m_new


def _fox_cumsum_kernel(gr_ref, cc_ref, cr_ref):
    seq = gr_ref.shape[1]
    t_idx = lax.broadcasted_iota(jnp.int32, (LANES, LANES), 0)
    s_idx = lax.broadcasted_iota(jnp.int32, (LANES, LANES), 1)
    triu = (t_idx <= s_idx).astype(F32)
    blks = [slice(j * LANES, (j + 1) * LANES) for j in range(seq // LANES)]
    rbs = [jnp.dot(gr_ref[:, blk], triu, precision=HI, preferred_element_type=F32) for blk in blks]
    carry = jnp.zeros((N_GATE_ROWS, 1), F32)
    pad = jnp.zeros((LANES - N_GATE_ROWS, LANES), F32)
    for blk, rb in zip(blks, rbs):
        cr = (rb + carry) * LOG2E
        cr_ref[:, blk] = cr
        cc_ref[0, blk, :] = jnp.concatenate([cr, pad], axis=0).T
        carry = carry + rb[:, LANES - 1:LANES]


def _fox_cumsum(grow, *, batch, seq):
    return pl.pallas_call(
        _fox_cumsum_kernel,
        grid=(batch,),
        in_specs=[pl.BlockSpec((N_GATE_ROWS, seq), lambda b: (0, b))],
        out_specs=[pl.BlockSpec((1, seq, LANES), lambda b: (b, 0, 0)),
                   pl.BlockSpec((N_GATE_ROWS, seq), lambda b: (0, b))],
        out_shape=[jax.ShapeDtypeStruct((batch, seq, LANES), F32),
                   jax.ShapeDtypeStruct((N_GATE_ROWS, batch * seq), F32)],
        compiler_params=pltpu.CompilerParams(dimension_semantics=("arbitrary",)),
        name="fox_cumsum",
    )(grow)


def _fox_prompt_kernel(q_ref, k_ref, vt_ref, cc_ref, cr_ref, o_ref, qs_ref, m_ref, l_ref, acc_ref, *, blk):
    i = pl.program_id(1)
    n_pairs = H_F // 2
    lane = lax.broadcasted_iota(jnp.int32, (blk, LANES), 1)
    low_half = lane < DH_F
    s_idx = lax.broadcasted_iota(jnp.int32, (blk, blk), 0)
    t_idx = lax.broadcasted_iota(jnp.int32, (blk, blk), 1)
    gate0 = N_GATE_ROWS - H_F
    q_cols = pl.ds(pl.multiple_of(i * blk, blk), blk)

    for pair in range(n_pairs):
        slab = slice(pair * LANES, (pair + 1) * LANES)
        q_pair = q_ref[:, slab]
        zero = jnp.zeros_like(q_pair)
        qs_ref[pair, 0:blk, :] = jnp.where(low_half, q_pair, zero)
        qs_ref[pair, blk:2 * blk, :] = jnp.where(low_half, zero, q_pair)
    m_ref[...] = jnp.full(m_ref.shape, -jnp.inf, F32)
    l_ref[...] = jnp.zeros(l_ref.shape, F32)
    acc_ref[...] = jnp.zeros(acc_ref.shape, F32)

    pairs = range(n_pairs)

    def key_rows(j):
        return pl.ds(pl.multiple_of(j * blk, blk), blk)

    def scores(j):
        return tuple(lax.dot_general(k_ref[key_rows(j), pair * LANES:(pair + 1) * LANES], qs_ref[pair], NT,
                                     preferred_element_type=F32) for pair in pairs)

    def softmax(j, raw, on_diagonal):
        pb, alpha = [], []
        for pair in pairs:
            ck = jnp.concatenate(
                [jnp.broadcast_to(cc_ref[key_rows(j), gate0 + 2 * pair + e:gate0 + 2 * pair + e + 1], (blk, blk))
                 for e in range(2)], axis=1)
            cq = jnp.concatenate(
                [cr_ref[gate0 + 2 * pair + e:gate0 + 2 * pair + e + 1, q_cols] for e in range(2)], axis=1)
            s = raw[pair] - ck
            if on_diagonal:
                s = jnp.where(jnp.concatenate([s_idx <= t_idx] * 2, axis=1), s, -jnp.inf)
            m_old = m_ref[pair]
            m_new = jnp.maximum(m_old, jnp.max(s, axis=0, keepdims=True) + cq)
            p = jnp.exp2(s - (m_new - cq))
            alpha.append(jnp.exp2(m_old - m_new))
            l_ref[pair] = alpha[pair] * l_ref[pair] + jnp.sum(p, axis=0, keepdims=True)
            m_ref[pair] = m_new
            pb.append(p.astype(BF16))
        return pb, alpha

    def values(j, pb, alpha):
        for pair in pairs:
            for e in range(2):
                own = slice(pair * LANES + e * DH_F, pair * LANES + (e + 1) * DH_F)
                lanes = slice(e * blk, (e + 1) * blk)
                pv = jnp.dot(vt_ref[0, own, key_rows(j)], pb[pair][:, lanes], preferred_element_type=F32)
                acc_ref[own, :] = alpha[pair][:, lanes] * acc_ref[own, :] + pv

    def key_block(j, on_diagonal):
        pb, alpha = softmax(j, scores(j), on_diagonal)
        values(j, pb, alpha)

    def loop_body(j, carry):
        key_block(j, False)
        return carry

    lax.fori_loop(0, i, loop_body, 0)
    key_block(i, True)
    inv_l = jnp.concatenate(
        [jnp.broadcast_to(1.0 / l_ref[h // 2][:, (h % 2) * blk:(h % 2 + 1) * blk], (DH_F, blk)) for h in range(H_F)],
        axis=0)
    o_ref[...] = (acc_ref[...] * inv_l).T.astype(o_ref.dtype)


def _fox_prompt(qf, kfr, vtb, ccol, crow, *, batch, seq):
    blk = FOX_BLOCK
    nq = seq // blk
    stat = pltpu.VMEM((H_F // 2, 1, 2 * blk), F32)
    return pl.pallas_call(
        functools.partial(_fox_prompt_kernel, blk=blk),
        grid=(batch, nq),
        in_specs=[pl.BlockSpec((blk, W_HEADS), lambda b, i: (b * nq + i, 0)),
                  pl.BlockSpec((seq, W_HEADS), lambda b, i: (b, 0)),
                  pl.BlockSpec((1, W_HEADS, seq), lambda b, i: (b, 0, 0)),
                  pl.BlockSpec((seq, LANES), lambda b, i: (b, 0)),
                  pl.BlockSpec((N_GATE_ROWS, seq), lambda b, i: (0, b))],
        out_specs=pl.BlockSpec((blk, W_HEADS), lambda b, i: (b * nq + i, 0)),
        out_shape=jax.ShapeDtypeStruct((batch * seq, W_HEADS), BF16),
        scratch_shapes=[pltpu.VMEM((H_F // 2, 2 * blk, LANES), BF16), stat, stat, pltpu.VMEM((W_HEADS, blk), F32)],
        compiler_params=pltpu.CompilerParams(dimension_semantics=("arbitrary", "arbitrary"),
                                             vmem_limit_bytes=VMEM_LIMIT),
        name="fox_prompt",
    )(qf, kfr, vtb, ccol.reshape(batch * seq, LANES), crow)


def _page_bias_kernel(pt_ref, lf_ref, o_ref):
    b = pl.program_id(0)
    n_pages = o_ref.shape[1]
    x = jnp.concatenate([lf_ref[pt_ref[b, p]] for p in range(n_pages)], axis=0)
    t_idx = lax.broadcasted_iota(jnp.int32, (LANES, LANES), 0)
    s_idx = lax.broadcasted_iota(jnp.int32, (LANES, LANES), 1)
    later = (t_idx > s_idx).astype(F32)
    within = jnp.dot(x, later, precision=HI, preferred_element_type=F32)
    total = jnp.sum(x, axis=-1, keepdims=True)
    run = jnp.zeros((H_F, LANES), F32)
    for p in range(n_pages - 1, -1, -1):
        rows = slice(p * H_F, (p + 1) * H_F)
        o_ref[0, p] = (within[rows] + run) * (-LOG2E)
        run = run + total[rows]


def _page_bias(page_table, lf_pool):
    batch, n_pages = page_table.shape
    return pl.pallas_call(
        _page_bias_kernel,
        grid_spec=pltpu.PrefetchScalarGridSpec(
            num_scalar_prefetch=1,
            grid=(batch,),
            in_specs=[_const_spec(lf_pool.shape)],
            out_specs=pl.BlockSpec((1, n_pages, H_F, LANES), lambda b, pt: (b, 0, 0, 0)),
        ),
        out_shape=jax.ShapeDtypeStruct((batch, n_pages, H_F, LANES), F32),
        compiler_params=pltpu.CompilerParams(dimension_semantics=("arbitrary",), vmem_limit_bytes=VMEM_LIMIT),
        name="page_bias",
    )(page_table, lf_pool)


def _paged_attention(pt_ref, q_ref, gr_ref, kn_ref, vn_ref, cp_ref, k_hbm, v_hbm, o_ref,
                     kbuf, vbuf, sem, acc_ref, *, n_groups, phases):
    b = pl.program_id(0)
    nb = pl.num_programs(0)
    group = PAGES_PER_GROUP
    t_new = q_ref.shape[0]
    n_rows = t_new * H_F

    def page_copies(bi, g, sl):
        cps = []
        for j in range(group):
            page = pt_ref[bi, g * group + j]
            cps.append(pltpu.make_async_copy(k_hbm.at[page], kbuf.at[sl, j], sem.at[sl, 0]))
            cps.append(pltpu.make_async_copy(v_hbm.at[page], vbuf.at[sl, j], sem.at[sl, 1]))
        return cps

    def start_group(bi, g, sl):
        for cp in page_copies(bi, g, sl):
            cp.start()

    @pl.when(b == 0)
    def _():
        for a in range(PAGE_SLOTS - 1):
            start_group(0, a, a)

    sub = lax.broadcasted_iota(jnp.int32, (H_F, W_HEADS), 0)
    lane = lax.broadcasted_iota(jnp.int32, (H_F, W_HEADS), 1)
    own_head = (lane // DH_F) == sub
    q = q_ref[...].astype(F32)
    qbd = jnp.concatenate(
        [jnp.where(own_head, jnp.broadcast_to(q[t:t + 1, :], (H_F, W_HEADS)), 0.0) for t in range(t_new)],
        axis=0).astype(BF16)

    gate0 = N_GATE_ROWS - H_F
    lf_new = gr_ref[0, gate0:N_GATE_ROWS, :]
    a_idx = lax.broadcasted_iota(jnp.int32, (t_new, t_new), 0)
    b_idx = lax.broadcasted_iota(jnp.int32, (t_new, t_new), 1)
    c_new = jnp.dot(lf_new, (a_idx <= b_idx).astype(F32), precision=HI, preferred_element_type=F32) * LOG2E
    cq = jnp.concatenate([c_new[:, t:t + 1] for t in range(t_new)], axis=0)

    acc_ref[...] = jnp.zeros_like(acc_ref)

    def body(g, carry, side_work):
        m_i, l_i = carry
        slot = g % PAGE_SLOTS
        ahead = g + (PAGE_SLOTS - 1)

        @pl.when(ahead < n_groups)
        def _():
            start_group(b, ahead, ahead % PAGE_SLOTS)

        @pl.when(jnp.logical_and(ahead >= n_groups, b + 1 < nb))
        def _():
            start_group(b + 1, ahead - n_groups, ahead % PAGE_SLOTS)

        for cp in page_copies(b, g, slot):
            cp.wait()
        side = side_work(g)

        kcat = jnp.concatenate([kbuf[slot, j].astype(BF16) for j in range(group)], axis=1)
        s = jnp.dot(qbd, kcat, preferred_element_type=F32)
        next(side, None)
        first = pl.multiple_of(g * group, group)
        cpg = cp_ref[0, pl.ds(first, group)]
        bias = jnp.concatenate(
            [jnp.broadcast_to(cpg[j][None], (t_new, H_F, LANES)).reshape(n_rows, LANES) for j in range(group)],
            axis=1)
        s = s + (cq - bias)
        m_new = jnp.maximum(m_i, jnp.max(s, axis=-1, keepdims=True))
        p = jnp.exp2(s - m_new)
        alpha = jnp.exp2(m_i - m_new)
        l_new = alpha * l_i + jnp.sum(p, axis=-1, keepdims=True)
        next(side, None)
        vcat = jnp.concatenate([vbuf[slot, j].T.astype(BF16) for j in range(group)], axis=0)
        pv = jnp.dot(p.astype(BF16), vcat, preferred_element_type=F32)
        next(side, None)
        acc_ref[...] = alpha * acc_ref[...] + pv
        return m_new, l_new

    carry = (jnp.full((n_rows, 1), -jnp.inf, F32), jnp.zeros((n_rows, 1), F32))
    start = 0
    for stop, side_work in phases:
        carry = lax.fori_loop(start, stop, functools.partial(body, side_work=side_work), carry)
        start = stop
    assert start == n_groups
    m_i, l_i = carry

    kn = kn_ref[...].astype(BF16)
    vn = vn_ref[...].astype(BF16)
    s = lax.dot_general(qbd, kn, NT, preferred_element_type=F32)
    ck = jnp.broadcast_to(c_new[None], (t_new, H_F, t_new)).reshape(n_rows, t_new)
    s = s + (cq - ck)
    r_idx = lax.broadcasted_iota(jnp.int32, (n_rows, t_new), 0)
    k_idx = lax.broadcasted_iota(jnp.int32, (n_rows, t_new), 1)
    s = jnp.where(k_idx <= r_idx // H_F, s, -jnp.inf)
    m_new = jnp.maximum(m_i, jnp.max(s, axis=-1, keepdims=True))
    p = jnp.exp2(s - m_new)
    alpha = jnp.exp2(m_i - m_new)
    l_fin = alpha * l_i + jnp.sum(p, axis=-1, keepdims=True)
    acc = alpha * acc_ref[...] + jnp.dot(p.astype(BF16), vn, preferred_element_type=F32)
    out = acc / l_fin
    o_ref[...] = jnp.concatenate(
        [jnp.sum(jnp.where(own_head, out[t * H_F:(t + 1) * H_F, :], 0.0), axis=0, keepdims=True)
         for t in range(t_new)], axis=0).astype(o_ref.dtype)


def _merge_residual(x_ref, hm_ref, hf_ref, wo_ref, g2_ref):
    x1 = (x_ref[...]
          + jnp.dot(hm_ref[...].astype(BF16), wo_ref[0:W_HEADS, :], preferred_element_type=F32)
          + jnp.dot(hf_ref[...].astype(BF16), wo_ref[W_HEADS:2 * W_HEADS, :], preferred_element_type=F32))
    return x1, _rms(x1, g2_ref[...]).astype(BF16)


def _ffn_sample_kernel(pt_ref, x_ref, hm_ref, hf_ref, wo_ref, g2_ref, wg_ref, wu_ref, wd_ref, g3_ref,
                       q_ref, gr_ref, kn_ref, vn_ref, cp_ref, k_hbm, v_hbm, y_ref, o_ref,
                       h_ref, x2_ref, act_ref, kbuf, vbuf, sem, acc_ref, *, n_groups, final_norm):
    x1, h = _merge_residual(x_ref, hm_ref, hf_ref, wo_ref, g2_ref)
    h_ref[...] = h
    x2_ref[...] = x1
    d_ff = wg_ref.shape[1]
    up_groups = d_ff // FFN_CHUNK
    down_groups = d_ff // DOWN_CHUNK
    tail = d_ff - down_groups * DOWN_CHUNK
    assert up_groups + down_groups == n_groups and tail in (0, FFN_CHUNK)

    def up_chunk(c):
        cols = pl.ds(pl.multiple_of(c * FFN_CHUNK, FFN_CHUNK), FFN_CHUNK)
        hh = h_ref[...]
        gate = jnp.dot(hh, wg_ref[:, cols], preferred_element_type=F32)
        yield
        up = jnp.dot(hh, wu_ref[:, cols], preferred_element_type=F32)
        act_ref[:, cols] = (gate * _sigmoid(gate) * up).astype(BF16)
        yield

    def last_up_chunk(c):
        yield from up_chunk(c)
        cols = pl.ds(pl.multiple_of(c * FFN_CHUNK, FFN_CHUNK), FFN_CHUNK)
        x2_ref[...] += jnp.dot(act_ref[:, cols], wd_ref[cols, :], preferred_element_type=F32)
        yield

    def down_slice(g):
        cols = pl.ds(pl.multiple_of((g - up_groups) * DOWN_CHUNK, DOWN_CHUNK), DOWN_CHUNK)
        yield
        x2_ref[...] += jnp.dot(act_ref[:, cols], wd_ref[cols, :], preferred_element_type=F32)
        yield

    if tail:
        phases = [(up_groups - 1, up_chunk), (up_groups, last_up_chunk), (n_groups, down_slice)]
    else:
        phases = [(up_groups, up_chunk), (n_groups, down_slice)]
    _paged_attention(pt_ref, q_ref, gr_ref, kn_ref, vn_ref, cp_ref, k_hbm, v_hbm, o_ref, kbuf, vbuf, sem, acc_ref,
                     n_groups=n_groups, phases=phases)
    x2 = x2_ref[...]
    y_ref[...] = _rms(x2, g3_ref[...]) if final_norm else x2


def _merge_ffn_and_fox_sample(x2d, hm, hf, wo, g2, wg, wu, wd, g3, page_table, qf, grow3, k_new, v_new, page_bias,
                              k_pool, v_pool, *, final_norm):
    rows_total, d_model = x2d.shape
    batch, n_pages = page_table.shape
    t_new = qf.shape[0] // batch
    n_groups = n_pages // PAGES_PER_GROUP
    tm = rows_total // batch
    d_ff = wg.shape[1]
    assert n_pages % PAGES_PER_GROUP == 0 and n_groups % PAGE_SLOTS == 0
    assert rows_total % batch == 0 and tm % 16 == 0 and d_ff % FFN_CHUNK == 0 and W_HEADS % FFN_CHUNK == 0
    page_rows, page_len = k_pool.shape[1], k_pool.shape[2]
    row = lambda w: pl.BlockSpec((tm, w), lambda b, pt: (b, 0))
    tok = lambda w: pl.BlockSpec((t_new, w), lambda b, pt: (b, 0))
    return pl.pallas_call(
        functools.partial(_ffn_sample_kernel, n_groups=n_groups, final_norm=final_norm),
        grid_spec=pltpu.PrefetchScalarGridSpec(
            num_scalar_prefetch=1,
            grid=(batch,),
            in_specs=[row(d_model), row(W_HEADS), row(W_HEADS), _const_spec(wo.shape), _const_spec(g2.shape),
                      _const_spec(wg.shape), _const_spec(wu.shape), _const_spec(wd.shape), _const_spec(g3.shape),
                      tok(W_HEADS),
                      pl.BlockSpec((1, N_GATE_ROWS, t_new), lambda b, pt: (b, 0, 0)),
                      tok(W_HEADS), tok(W_HEADS),
                      pl.BlockSpec((1, n_pages, H_F, LANES), lambda b, pt: (b, 0, 0, 0)),
                      pl.BlockSpec(memory_space=pl.ANY), pl.BlockSpec(memory_space=pl.ANY)],
            out_specs=[row(d_model), tok(W_HEADS)],
            scratch_shapes=[pltpu.VMEM((tm, d_model), BF16),
                            pltpu.VMEM((tm, d_model), F32),
                            pltpu.VMEM((tm, d_ff), BF16),
                            pltpu.VMEM((PAGE_SLOTS, PAGES_PER_GROUP, page_rows, page_len), F32),
                            pltpu.VMEM((PAGE_SLOTS, PAGES_PER_GROUP, page_rows, page_len), F32),
                            pltpu.SemaphoreType.DMA((PAGE_SLOTS, 2)),
                            pltpu.VMEM((t_new * H_F, W_HEADS), F32)],
        ),
        out_shape=[jax.ShapeDtypeStruct((rows_total, d_model), F32),
                   jax.ShapeDtypeStruct((batch * t_new, W_HEADS), F32)],
        compiler_params=pltpu.CompilerParams(dimension_semantics=("arbitrary",), vmem_limit_bytes=VMEM_LIMIT_WEIGHTS),
        name="ffn_and_fox_sample",
    )(page_table, x2d, hm, hf, wo, g2, wg, wu, wd, g3, qf, grow3, k_new, v_new, page_bias, k_pool, v_pool)


def _ffn_kernel(x_ref, hm_ref, hf_ref, wo_ref, g2_ref, wg_ref, wu_ref, wd_ref, g3_ref, y_ref, *, final_norm):
    x1, h = _merge_residual(x_ref, hm_ref, hf_ref, wo_ref, g2_ref)
    gate = jnp.dot(h, wg_ref[...], preferred_element_type=F32)
    up = jnp.dot(h, wu_ref[...], preferred_element_type=F32)
    act = (gate * _sigmoid(gate) * up).astype(BF16)
    x2 = x1 + jnp.dot(act, wd_ref[...], preferred_element_type=F32)
    y_ref[...] = _rms(x2, g3_ref[...]) if final_norm else x2


def _merge_ffn(x2d, hm, hf, wo, g2, wg, wu, wd, g3, *, final_norm):
    rows_total, d_model = x2d.shape
    tm = min(FFN_ROWS, rows_total)
    row_spec = lambda w: pl.BlockSpec((tm, w), lambda i: (i, 0))
    return pl.pallas_call(
        functools.partial(_ffn_kernel, final_norm=final_norm),
        grid=(rows_total // tm,),
        in_specs=[row_spec(d_model), row_spec(W_HEADS), row_spec(W_HEADS), _const_spec(wo.shape),
                  _const_spec(g2.shape), _const_spec(wg.shape), _const_spec(wu.shape), _const_spec(wd.shape),
                  _const_spec(g3.shape)],
        out_specs=row_spec(d_model),
        out_shape=jax.ShapeDtypeStruct((rows_total, d_model), F32),
        compiler_params=pltpu.CompilerParams(dimension_semantics=("arbitrary",), vmem_limit_bytes=VMEM_LIMIT_WEIGHTS),
        name="merge_ffn",
    )(x2d, hm, hf, wo, g2, wg, wu, wd, g3)


def kernel(x_prompt, x_sample, cache_fox_k, cache_fox_v, cache_fox_logf, page_table, state_mlstm_C,
           state_mlstm_n, state_mlstm_m, norm_mix_g, w_in, b_m_igate, b_m_fgate, b_f_fgate, mlstm_head_g,
           w_out, norm_ffn_g, w_gate, w_up, w_down, norm_final_g):
    depth = w_in.shape[0]
    batch, seq, d_model = x_prompt.shape
    dec_batch, dec_seq, _ = x_sample.shape
    n_pool, page_size = cache_fox_k.shape[1], cache_fox_k.shape[2]
    xp = x_prompt.reshape(batch * seq, d_model)
    xs = x_sample.reshape(dec_batch * dec_seq, d_model)
    g_final = norm_final_g.reshape(1, d_model)
    pk, pv, plf, pc, pn, pm = [], [], [], [], [], []
    sk, sv, slf, sc, sn, sm = [], [], [], [], [], []
    gate0 = N_GATE_ROWS - H_F
    o_gm = 4 * W_HEADS
    o_qf = o_gm + 2 * H_M
    o_gf = o_qf + 3 * W_HEADS
    for l in range(depth):
        wt = jnp.swapaxes(w_in[l], 0, 1)
        wt_main = jnp.concatenate([wt[0:o_gm], wt[o_qf:o_qf + W_HEADS]], axis=0).astype(BF16)
        wt_kv = wt[o_qf + W_HEADS:o_gf].astype(BF16)
        wt_g = jnp.concatenate([wt[o_gm:o_qf], wt[o_gf:o_gf + H_F],
                                jnp.zeros((LANES - N_GATE_ROWS, d_model), F32)], axis=0).astype(BF16)
        bias = jnp.concatenate([b_m_igate[l], b_m_fgate[l], b_f_fgate[l],
                                jnp.zeros((LANES - N_GATE_ROWS,), F32)]).astype(F32)
        bcol = bias.reshape(1, LANES)
        brow = bias[:N_GATE_ROWS].reshape(N_GATE_ROWS, 1)
        g_mix = norm_mix_g[l].reshape(1, d_model)
        g_ffn = norm_ffn_g[l].reshape(1, d_model)
        head_g = mlstm_head_g[l].reshape(1, W_HEADS)
        wo = w_out[l].astype(BF16)
        wg = w_gate[l].astype(BF16)
        wu = w_up[l].astype(BF16)
        wd = w_down[l].astype(BF16)

        qm, km, vmt, omt, qf, kt, vt, kfr, vtb, gcol, grow = _project(
            xp, g_mix, wt_main, wt_kv, wt_g, bcol, brow, batch=batch, seq=seq, kv_transposed=True, act_dtype=BF16)
        chunk = min(LANES, seq)
        grow3 = grow.reshape(N_GATE_ROWS, batch, seq).transpose(1, 0, 2)
        hm, c_p, n_p, m_p = _mlstm_prompt(
            qm, km, vmt, omt, gcol, grow3, head_g,
            jnp.zeros((batch, H_M, DH_M, DH_M), F32), jnp.zeros((batch, H_M, DH_M), F32),
            jnp.zeros((batch, H_M), F32), batch=batch, seq=seq, chunk=chunk, out_dtype=BF16)
        ccol, crow = _fox_cumsum(grow, batch=batch, seq=seq)
        hf = _fox_prompt(qf, kfr, vtb, ccol, crow, batch=batch, seq=seq)
        hm_p, hf_p = hm, hf
        pk.append(kt.reshape(batch, H_F, DH_F, seq).transpose(0, 3, 1, 2))
        pv.append(vt.reshape(batch, H_F, DH_F, seq).transpose(0, 3, 1, 2))
        plf.append(grow3[:, gate0:, :].transpose(0, 2, 1))
        pc.append(c_p); pn.append(n_p); pm.append(m_p)

        qm, km, vm, om, qf, k_new, v_new, gcol, grow = _project(
            xs, g_mix, wt_main, wt_kv, wt_g, bcol, brow, batch=dec_batch, seq=dec_seq, kv_transposed=False,
            act_dtype=F32)
        grow3 = grow.reshape(N_GATE_ROWS, dec_batch, dec_seq).transpose(1, 0, 2)
        hm, c_s, n_s, m_s = _mlstm(
            qm, km, vm, om, gcol, grow3, head_g,
            state_mlstm_C[l].astype(F32), state_mlstm_n[l].astype(F32), state_mlstm_m[l].astype(F32),
            batch=dec_batch, seq=dec_seq, chunk=dec_seq, out_dtype=F32)
        k_pool = cache_fox_k[l].transpose(0, 2, 3, 1).reshape(n_pool, W_HEADS, page_size)
        v_pool = cache_fox_v[l].transpose(0, 2, 3, 1).reshape(n_pool, W_HEADS, page_size)
        lf_pool = cache_fox_logf[l].transpose(0, 2, 1)
        page_bias = _page_bias(page_table, lf_pool)
        xp, hf = _merge_ffn_and_fox_sample(
            xp, hm_p, hf_p, wo, g_ffn, wg, wu, wd, g_final, page_table, qf, grow3, k_new, v_new, page_bias,
            k_pool, v_pool, final_norm=(l == depth - 1))
        xs = _merge_ffn(xs, hm, hf, wo, g_ffn, wg, wu, wd, g_final, final_norm=(l == depth - 1))
        sk.append(k_new.reshape(dec_batch, dec_seq, H_F, DH_F))
        sv.append(v_new.reshape(dec_batch, dec_seq, H_F, DH_F))
        slf.append(grow3[:, gate0:, :].transpose(0, 2, 1))
        sc.append(c_s); sn.append(n_s); sm.append(m_s)

    st = lambda a, ref: jnp.stack(a, axis=0).astype(ref.dtype)
    return (xp.reshape(batch, seq, d_model), xs.reshape(dec_batch, dec_seq, d_model),
            st(pk, cache_fox_k), st(pv, cache_fox_v), st(plf, cache_fox_logf),
            st(pc, state_mlstm_C), st(pn, state_mlstm_n), st(pm, state_mlstm_m),
            st(sk, cache_fox_k), st(sv, cache_fox_v), st(slf, cache_fox_logf),
            st(sc, state_mlstm_C), st(sn, state_mlstm_n), st(sm, state_mlstm_m))
```

```python
import functools

import jax
import jax.numpy as jnp
from jax import lax
from jax.experimental import pallas as pl
from jax.experimental.pallas import tpu as pltpu

F32 = jnp.float32
BF16 = jnp.bfloat16
HI = lax.Precision.HIGHEST
NT = (((1,), (1,)), ((), ()))
TN = (((0,), (0,)), ((), ()))

EPS = 1e-6
LOG2E = 1.4426950408889634
H_M = 4
DH_M = 128
H_F = 8
DH_F = 64
W_HEADS = 512
N_GATE_ROWS = 16
LANES = 128
ROW_GROUP = 16
MIB = 1024 * 1024
V7X_VMEM_BYTES = 64 * MIB
VMEM_LIMIT = V7X_VMEM_BYTES * 3 // 4
VMEM_LIMIT_WEIGHTS = V7X_VMEM_BYTES - 6 * MIB

PROJ_ROWS = 1024
FFN_ROWS = 512
FFN_CHUNK = 256
DOWN_CHUNK = 512
MLSTM_BATCH = 8
MLSTM_INTERLEAVE = 8
FOX_BLOCK = 512
PAGES_PER_GROUP = 8
PAGE_SLOTS = 4


def _rms(x, g):
    return x * lax.rsqrt(jnp.mean(x * x, axis=-1, keepdims=True) + EPS) * g


def _log_sigmoid(x):
    return jnp.minimum(x, 0.0) - jnp.log1p(jnp.exp(-jnp.abs(x)))


def _sigmoid(x):
    return 1.0 / (1.0 + jnp.exp(-x))


def _const_spec(shape):
    return pl.BlockSpec(shape, lambda *_: (0,) * len(shape), pipeline_mode=pl.Buffered(1))


def _proj_kernel(x_ref, g_ref, wt_ref, wkv_ref, wg_ref, bcol_ref, brow_ref, *outs, kv_transposed):
    if kv_transposed:
        qm_ref, km_ref, vm_ref, om_ref, qf_ref, kf_ref, vf_ref, kfb_ref, vfb_ref, gcol_ref, grow_ref = outs
    else:
        qm_ref, km_ref, vm_ref, om_ref, qf_ref, kf_ref, vf_ref, gcol_ref, grow_ref = outs
    h = _rms(x_ref[...], g_ref[...]).astype(BF16)
    rows = h.shape[0]

    def mm(i):
        w = wt_ref[i * W_HEADS:(i + 1) * W_HEADS, :]
        return lax.dot_general(h, w, NT, preferred_element_type=F32)

    def mm_t(w):
        return lax.dot_general(w, h, NT, preferred_element_type=F32)

    qm_ref[...] = mm(0).astype(qm_ref.dtype)
    km_ref[...] = (mm(1) * (DH_M ** -0.5)).astype(km_ref.dtype)
    if kv_transposed:
        vm_ref[0] = mm_t(wt_ref[2 * W_HEADS:3 * W_HEADS, :]).astype(vm_ref.dtype)
        om_ref[0] = mm_t(wt_ref[3 * W_HEADS:4 * W_HEADS, :]).astype(om_ref.dtype)
    else:
        vm_ref[...] = mm(2).astype(vm_ref.dtype)
        om_ref[...] = mm(3).astype(om_ref.dtype)
    qf_ref[...] = (mm(4) * (DH_F ** -0.5 * LOG2E)).astype(qf_ref.dtype)
    if kv_transposed:
        kt = lax.dot_general(wkv_ref[0:W_HEADS, :], h, NT, preferred_element_type=F32)
        kf_ref[0] = kt
        kfb_ref[...] = kt.T.astype(BF16)
        vt = lax.dot_general(wkv_ref[W_HEADS:2 * W_HEADS, :], h, NT, preferred_element_type=F32)
        vf_ref[0] = vt
        vfb_ref[0] = vt.astype(BF16)
    else:
        kf_ref[...] = lax.dot_general(h, wkv_ref[0:W_HEADS, :], NT, preferred_element_type=F32)
        vf_ref[...] = lax.dot_general(h, wkv_ref[W_HEADS:2 * W_HEADS, :], NT, preferred_element_type=F32)
    pre_c = lax.dot_general(h, wg_ref[...], NT, preferred_element_type=F32) + bcol_ref[...]
    lane = lax.broadcasted_iota(jnp.int32, (rows, LANES), 1)
    gcol_ref[...] = jnp.where(lane < H_M, pre_c, _log_sigmoid(pre_c))
    pre_r = lax.dot_general(wg_ref[0:N_GATE_ROWS, :], h, NT, preferred_element_type=F32) + brow_ref[...]
    row = lax.broadcasted_iota(jnp.int32, (N_GATE_ROWS, rows), 0)
    grow_ref[...] = jnp.where(row < H_M, pre_r, _log_sigmoid(pre_r))


def _project(x2d, g, wt_main, wt_kv, wt_g, bcol, brow, *, batch, seq, kv_transposed, act_dtype):
    rows_total, d_model = x2d.shape
    tm = min(PROJ_ROWS, rows_total)
    steps = rows_total // tm
    per_seq = max(seq // tm, 1)
    row_spec = lambda w: pl.BlockSpec((tm, w), lambda i: (i, 0))
    in_specs = [row_spec(d_model), _const_spec(g.shape), _const_spec(wt_main.shape), _const_spec(wt_kv.shape),
                _const_spec(wt_g.shape), _const_spec(bcol.shape), _const_spec(brow.shape)]
    act = jax.ShapeDtypeStruct((rows_total, W_HEADS), act_dtype)
    act32 = jax.ShapeDtypeStruct((rows_total, W_HEADS), F32)
    out_shape = [act, act, act, act32, act]
    out_specs = [row_spec(W_HEADS)] * 5
    if kv_transposed:
        kv_spec = pl.BlockSpec((1, W_HEADS, tm), lambda i: (i // per_seq, 0, i % per_seq))
        out_shape[2] = jax.ShapeDtypeStruct((batch, W_HEADS, seq), act_dtype)
        out_shape[3] = jax.ShapeDtypeStruct((batch, W_HEADS, seq), F32)
        out_specs[2] = out_specs[3] = kv_spec
        out_shape += [jax.ShapeDtypeStruct((batch, W_HEADS, seq), F32)] * 2
        out_shape += [jax.ShapeDtypeStruct((rows_total, W_HEADS), BF16),
                      jax.ShapeDtypeStruct((batch, W_HEADS, seq), BF16)]
        out_specs += [kv_spec, kv_spec, row_spec(W_HEADS), kv_spec]
    else:
        out_shape += [act32, act32]
        out_specs += [row_spec(W_HEADS)] * 2
    out_shape += [jax.ShapeDtypeStruct((rows_total, LANES), F32),
                  jax.ShapeDtypeStruct((N_GATE_ROWS, rows_total), F32)]
    out_specs += [row_spec(LANES), pl.BlockSpec((N_GATE_ROWS, tm), lambda i: (0, i))]
    return pl.pallas_call(
        functools.partial(_proj_kernel, kv_transposed=kv_transposed),
        grid=(steps,),
        in_specs=in_specs,
        out_specs=out_specs,
        out_shape=out_shape,
        compiler_params=pltpu.CompilerParams(dimension_semantics=("arbitrary",), vmem_limit_bytes=VMEM_LIMIT),
        name="proj",
    )(x2d, g, wt_main, wt_kv, wt_g, bcol, brow)


def _mlstm_kernel(q_ref, k_ref, v_ref, om_ref, gc_ref, gr_ref, hg_ref, c0_ref, n0_ref, m0_ref,
                  hm_ref, c_ref, n_ref, m_ref, *, bb, chunk):
    @pl.when(pl.program_id(1) == 0)
    def _():
        c_ref[...] = c0_ref[...]
        n_ref[...] = n0_ref[...]
        m_ref[...] = m0_ref[...]

    t_idx = lax.broadcasted_iota(jnp.int32, (chunk, chunk), 0)
    s_idx = lax.broadcasted_iota(jnp.int32, (chunk, chunk), 1)
    causal = s_idx <= t_idx
    tril = causal.astype(F32)
    triu = (t_idx <= s_idx).astype(F32)

    def per_group(i, carry):
        bs = [i * MLSTM_INTERLEAVE + j for j in range(MLSTM_INTERLEAVE)]
        gc = [gc_ref[b] for b in bs]
        gr = [gr_ref[b] for b in bs]
        bc = [jnp.dot(tril, g, precision=HI, preferred_element_type=F32) for g in gc]
        br = [jnp.dot(g, triu, precision=HI, preferred_element_type=F32) for g in gr]
        n_all = [n_ref[b] for b in bs]
        m_all = [m_ref[pl.ds(b, 1), :] for b in bs]
        chains = [(j, h) for j in range(MLSTM_INTERLEAVE) for h in range(H_M)]
        ids = range(len(chains))
        sls = [slice(h * DH_M, (h + 1) * DH_M) for _, h in chains]
        c_prev = [c_ref[bs[j], h] for j, h in chains]
        n_prev = [n_all[j][h:h + 1, :] for j, h in chains]
        m_prev = [m_all[j][:, h:h + 1] for j, h in chains]
        qh = [q_ref[bs[j], :, sls[c]].astype(BF16) for c, (j, _) in enumerate(chains)]
        kh = [k_ref[bs[j], :, sls[c]].astype(BF16) for c, (j, _) in enumerate(chains)]
        vh = [v_ref[bs[j], :, sls[c]].astype(BF16) for c, (j, _) in enumerate(chains)]
        qk = [lax.dot_general(qh[c], kh[c], NT, preferred_element_type=F32) for c in ids]
        cq = [lax.dot_general(qh[c], c_prev[c].astype(BF16), NT, preferred_element_type=F32) for c in ids]
        li_c = [gc[j][:, h:h + 1] for j, h in chains]
        li_r = [gr[j][h:h + 1, :] for j, h in chains]
        b_c = [bc[j][:, H_M + h:H_M + h + 1] for j, h in chains]
        b_r = [br[j][H_M + h:H_M + h + 1, :] for j, h in chains]
        dmat = [jnp.where(causal, b_c[c] - b_r[c] + li_r[c], -jnp.inf) for c in ids]
        inter = [m_prev[c] + b_c[c] for c in ids]
        m_t = [jnp.maximum(inter[c], jnp.max(dmat[c], axis=-1, keepdims=True)) for c in ids]
        smat = [qk[c] * jnp.exp(dmat[c] - m_t[c]) for c in ids]
        w_inter = [jnp.exp(inter[c] - m_t[c]) for c in ids]
        sv = [jnp.dot(smat[c].astype(BF16), vh[c], preferred_element_type=F32) for c in ids]
        outs, vw, w_s, decay, m_new = [], [], [], [], []
        for c, (j, h) in enumerate(chains):
            num = w_inter[c] * cq[c] + sv[c]
            nq = jnp.sum(qh[c].astype(F32) * n_prev[c], axis=-1, keepdims=True)
            den = w_inter[c] * nq + jnp.sum(smat[c], axis=-1, keepdims=True)
            hh = num / jnp.maximum(jnp.abs(den), jnp.exp(-m_t[c]))
            hn = hh * lax.rsqrt(jnp.mean(hh * hh, axis=-1, keepdims=True) + EPS) * hg_ref[:, sls[c]]
            outs.append((hn * _sigmoid(om_ref[bs[j], :, sls[c]].astype(F32))).astype(hm_ref.dtype))
            b_last = b_c[c][chunk - 1:chunk, :]
            m_new.append(m_t[c][chunk - 1:chunk, :])
            decay.append(jnp.exp(m_prev[c] + b_last - m_new[c]))
            w_s.append(jnp.exp(li_c[c] + b_last - b_c[c] - m_new[c]))
            vw.append((vh[c].astype(F32) * w_s[c]).astype(BF16))
        for c, (j, h) in enumerate(chains):
            c_new = decay[c] * c_prev[c] + lax.dot_general(vw[c], kh[c], TN, preferred_element_type=F32)
            n_new = decay[c] * n_prev[c] + jnp.sum(kh[c].astype(F32) * w_s[c], axis=0, keepdims=True)
            hm_ref[bs[j], :, sls[c]] = outs[c]
            c_ref[bs[j], h] = c_new
            n_ref[bs[j], h:h + 1, :] = n_new
            m_ref[pl.ds(bs[j], 1), h:h + 1] = m_new[c]
        return carry

    lax.fori_loop(0, bb // MLSTM_INTERLEAVE, per_group, 0)


def _mlstm(qm, km, vm, om, gcol, grow3, head_g, c0, n0, m0, *, batch, seq, chunk, out_dtype):
    bb = MLSTM_BATCH
    n_chunks = seq // chunk
    as3 = lambda a: a.reshape(batch, seq, a.shape[-1])
    tok = lambda w: pl.BlockSpec((bb, chunk, w), lambda g, c: (g, c, 0))
    state = lambda shape: pl.BlockSpec((bb,) + shape, lambda g, c: (g,) + (0,) * len(shape))
    in_specs = [tok(W_HEADS), tok(W_HEADS), tok(W_HEADS), tok(W_HEADS), tok(LANES),
                pl.BlockSpec((bb, N_GATE_ROWS, chunk), lambda g, c: (g, 0, c)),
                _const_spec(head_g.shape),
                state((H_M, DH_M, DH_M)), state((H_M, DH_M)), state((H_M,))]
    out_specs = [tok(W_HEADS), state((H_M, DH_M, DH_M)), state((H_M, DH_M)), state((H_M,))]
    out_shape = [jax.ShapeDtypeStruct((batch, seq, W_HEADS), out_dtype),
                 jax.ShapeDtypeStruct(c0.shape, F32), jax.ShapeDtypeStruct(n0.shape, F32),
                 jax.ShapeDtypeStruct(m0.shape, F32)]
    hm, c_new, n_new, m_new = pl.pallas_call(
        functools.partial(_mlstm_kernel, bb=bb, chunk=chunk),
        grid=(batch // bb, n_chunks),
        in_specs=in_specs,
        out_specs=out_specs,
        out_shape=out_shape,
        compiler_params=pltpu.CompilerParams(dimension_semantics=("arbitrary", "arbitrary"),
                                             vmem_limit_bytes=VMEM_LIMIT),
        name="mlstm",
    )(as3(qm), as3(km), as3(vm), as3(om), as3(gcol), grow3, head_g, c0, n0, m0)
    return hm.reshape(batch * seq, W_HEADS), c_new, n_new, m_new


def _mlstm_t_kernel(q_ref, k_ref, vt_ref, omt_ref, gc_ref, gr_ref, hg_ref, c0_ref, n0_ref, m0_ref,
                    hm_ref, c_ref, n_ref, m_ref, *, bb, chunk):
    @pl.when(pl.program_id(1) == 0)
    def _():
        c_ref[...] = c0_ref[...]
        n_ref[...] = n0_ref[...]
        m_ref[...] = m0_ref[...]

    s_idx = lax.broadcasted_iota(jnp.int32, (chunk, chunk), 0)
    t_idx = lax.broadcasted_iota(jnp.int32, (chunk, chunk), 1)
    causal = s_idx <= t_idx
    tril = (t_idx <= s_idx).astype(F32)
    triu = causal.astype(F32)

    def per_group(i, carry):
        bs = [i * MLSTM_INTERLEAVE + j for j in range(MLSTM_INTERLEAVE)]
        gc = [gc_ref[b] for b in bs]
        gr = [gr_ref[b] for b in bs]
        bc = [jnp.dot(tril, g, precision=HI, preferred_element_type=F32) for g in gc]
        br = [jnp.dot(g, triu, precision=HI, preferred_element_type=F32) for g in gr]
        n_all = [n_ref[b] for b in bs]
        m_all = [m_ref[pl.ds(b, 1), :] for b in bs]
        chains = [(j, h) for j in range(MLSTM_INTERLEAVE) for h in range(H_M)]
        ids = range(len(chains))
        sls = [slice(h * DH_M, (h + 1) * DH_M) for _, h in chains]
        c_prev = [c_ref[bs[j], h] for j, h in chains]
        n_prev = [n_all[j][h:h + 1, :] for j, h in chains]
        m_prev = [m_all[j][:, h:h + 1] for j, h in chains]
        qh = [q_ref[bs[j], :, sls[c]].astype(BF16) for c, (j, _) in enumerate(chains)]
        kh = [k_ref[bs[j], :, sls[c]].astype(BF16) for c, (j, _) in enumerate(chains)]
        vth = [vt_ref[bs[j], sls[c], :].astype(BF16) for c, (j, _) in enumerate(chains)]
        qk = [lax.dot_general(kh[c], qh[c], NT, preferred_element_type=F32) for c in ids]
        cq = [lax.dot_general(c_prev[c].astype(BF16), qh[c], NT, preferred_element_type=F32) for c in ids]
        nq = [lax.dot_general(jnp.broadcast_to(n_prev[c], (8, DH_M)).astype(BF16), qh[c], NT,
                              preferred_element_type=F32)[0:1, :] for c in ids]
        li_r = [gr[j][h:h + 1, :] for j, h in chains]
        b_r = [br[j][H_M + h:H_M + h + 1, :] for j, h in chains]
        col = [gc[j][:, h:h + 1] - bc[j][:, H_M + h:H_M + h + 1] for j, h in chains]
        dmat = [jnp.where(causal, b_r[c] + col[c], -jnp.inf) for c in ids]
        inter = [m_prev[c] + b_r[c] for c in ids]
        m_t = [jnp.maximum(inter[c], jnp.max(dmat[c], axis=0, keepdims=True)) for c in ids]
        smat = [qk[c] * jnp.exp(dmat[c] - m_t[c]) for c in ids]
        w_inter = [jnp.exp(inter[c] - m_t[c]) for c in ids]
        sv = [jnp.dot(vth[c], smat[c].astype(BF16), preferred_element_type=F32) for c in ids]
        outs, vw, w_s_c, decay, m_new = [], [], [], [], []
        for c, (j, h) in enumerate(chains):
            num = w_inter[c] * cq[c] + sv[c]
            den = w_inter[c] * nq[c] + jnp.sum(smat[c], axis=0, keepdims=True)
            hh = num * (1.0 / jnp.maximum(jnp.abs(den), jnp.exp(-m_t[c])))
            hn = hh * lax.rsqrt(jnp.mean(hh * hh, axis=0, keepdims=True) + EPS) * hg_ref[sls[c], :]
            gate = _sigmoid(omt_ref[bs[j], sls[c], :].astype(F32))
            outs.append((hn * gate).T.astype(hm_ref.dtype))
            b_last = b_r[c][:, chunk - 1:chunk]
            m_new.append(m_t[c][:, chunk - 1:chunk])
            decay.append(jnp.exp(m_prev[c] + b_last - m_new[c]))
            w_s_r = jnp.exp(li_r[c] + b_last - b_r[c] - m_new[c])
            w_s_c.append(jnp.exp(col[c] + b_last - m_new[c]))
            vw.append((vth[c].astype(F32) * w_s_r).astype(BF16))
        for c, (j, h) in enumerate(chains):
            c_new = decay[c] * c_prev[c] + jnp.dot(vw[c], kh[c], preferred_element_type=F32)
            n_new = decay[c] * n_prev[c] + jnp.sum(kh[c].astype(F32) * w_s_c[c], axis=0, keepdims=True)
            hm_ref[bs[j], :, sls[c]] = outs[c]
            c_ref[bs[j], h] = c_new
            n_ref[bs[j], h:h + 1, :] = n_new
            m_ref[pl.ds(bs[j], 1), h:h + 1] = m_new[c]
        return carry

    lax.fori_loop(0, bb // MLSTM_INTERLEAVE, per_group, 0)


def _mlstm_prompt(qm, km, vmt, omt, gcol, grow3, head_g, c0, n0, m0, *, batch, seq, chunk, out_dtype):
    bb = MLSTM_BATCH
    n_chunks = seq // chunk
    as3 = lambda a: a.reshape(batch, seq, a.shape[-1])
    tok = lambda w: pl.BlockSpec((bb, chunk, w), lambda g, c: (g, c, 0))
    tok_t = pl.BlockSpec((bb, W_HEADS, chunk), lambda g, c: (g, 0, c))
    state = lambda shape: pl.BlockSpec((bb,) + shape, lambda g, c: (g,) + (0,) * len(shape))
    hg_col = head_g.reshape(W_HEADS, 1)
    in_specs = [tok(W_HEADS), tok(W_HEADS), tok_t, tok_t, tok(LANES),
                pl.BlockSpec((bb, N_GATE_ROWS, chunk), lambda g, c: (g, 0, c)),
                _const_spec(hg_col.shape),
                state((H_M, DH_M, DH_M)), state((H_M, DH_M)), state((H_M,))]
    out_specs = [tok(W_HEADS), state((H_M, DH_M, DH_M)), state((H_M, DH_M)), state((H_M,))]
    out_shape = [jax.ShapeDtypeStruct((batch, seq, W_HEADS), out_dtype),
                 jax.ShapeDtypeStruct(c0.shape, F32), jax.ShapeDtypeStruct(n0.shape, F32),
                 jax.ShapeDtypeStruct(m0.shape, F32)]
    hm, c_new, n_new, m_new = pl.pallas_call(
        functools.partial(_mlstm_t_kernel, bb=bb, chunk=chunk),
        grid=(batch // bb, n_chunks),
        in_specs=in_specs,
        out_specs=out_specs,
        out_shape=out_shape,
        compiler_params=pltpu.CompilerParams(dimension_semantics=("arbitrary", "arbitrary"),
                                             vmem_limit_bytes=VMEM_LIMIT),
        name="mlstm_prompt",
    )(as3(qm), as3(km), vmt, omt, as3(gcol), grow3, hg_col, c0, n0, m0)
    return hm.reshape(batch * seq, W_HEADS), c_new, n_new, m_new


def _fox_cumsum_kernel(gr_ref, cc_ref, cr_ref):
    seq = gr_ref.shape[1]
    t_idx = lax.broadcasted_iota(jnp.int32, (LANES, LANES), 0)
    s_idx = lax.broadcasted_iota(jnp.int32, (LANES, LANES), 1)
    triu = (t_idx <= s_idx).astype(F32)
    blks = [slice(j * LANES, (j + 1) * LANES) for j in range(seq // LANES)]
    rbs = [jnp.dot(gr_ref[:, blk], triu, precision=HI, preferred_element_type=F32) for blk in blks]
    carry = jnp.zeros((N_GATE_ROWS, 1), F32)
    pad = jnp.zeros((LANES - N_GATE_ROWS, LANES), F32)
    for blk, rb in zip(blks, rbs):
        cr = (rb + carry) * LOG2E
        cr_ref[:, blk] = cr
        cc_ref[0, blk, :] = jnp.concatenate([cr, pad], axis=0).T
        carry = carry + rb[:, LANES - 1:LANES]


def _fox_cumsum(grow, *, batch, seq):
    return pl.pallas_call(
        _fox_cumsum_kernel,
        grid=(batch,),
        in_specs=[pl.BlockSpec((N_GATE_ROWS, seq), lambda b: (0, b))],
        out_specs=[pl.BlockSpec((1, seq, LANES), lambda b: (b, 0, 0)),
                   pl.BlockSpec((N_GATE_ROWS, seq), lambda b: (0, b))],
        out_shape=[jax.ShapeDtypeStruct((batch, seq, LANES), F32),
                   jax.ShapeDtypeStruct((N_GATE_ROWS, batch * seq), F32)],
        compiler_params=pltpu.CompilerParams(dimension_semantics=("arbitrary",)),
        name="fox_cumsum",
    )(grow)


def _fox_prompt_kernel(q_ref, k_ref, vt_ref, cc_ref, cr_ref, o_ref, qs_ref, m_ref, l_ref, acc_ref, *, blk):
    i = pl.program_id(1)
    n_pairs = H_F // 2
    lane = lax.broadcasted_iota(jnp.int32, (blk, LANES), 1)
    low_half = lane < DH_F
    s_idx = lax.broadcasted_iota(jnp.int32, (blk, blk), 0)
    t_idx = lax.broadcasted_iota(jnp.int32, (blk, blk), 1)
    gate0 = N_GATE_ROWS - H_F
    q_cols = pl.ds(pl.multiple_of(i * blk, blk), blk)

    for pair in range(n_pairs):
        slab = slice(pair * LANES, (pair + 1) * LANES)
        q_pair = q_ref[:, slab]
        zero = jnp.zeros_like(q_pair)
        qs_ref[pair, 0:blk, :] = jnp.where(low_half, q_pair, zero)
        qs_ref[pair, blk:2 * blk, :] = jnp.where(low_half, zero, q_pair)
    m_ref[...] = jnp.full(m_ref.shape, -jnp.inf, F32)
    l_ref[...] = jnp.zeros(l_ref.shape, F32)
    acc_ref[...] = jnp.zeros(acc_ref.shape, F32)

    pairs = range(n_pairs)

    def key_rows(j):
        return pl.ds(pl.multiple_of(j * blk, blk), blk)

    def scores(j):
        return tuple(lax.dot_general(k_ref[key_rows(j), pair * LANES:(pair + 1) * LANES], qs_ref[pair], NT,
                                     preferred_element_type=F32) for pair in pairs)

    def softmax(j, raw, on_diagonal):
        pb, alpha = [], []
        for pair in pairs:
            ck = jnp.concatenate(
                [jnp.broadcast_to(cc_ref[key_rows(j), gate0 + 2 * pair + e:gate0 + 2 * pair + e + 1], (blk, blk))
                 for e in range(2)], axis=1)
            cq = jnp.concatenate(
                [cr_ref[gate0 + 2 * pair + e:gate0 + 2 * pair + e + 1, q_cols] for e in range(2)], axis=1)
            s = raw[pair] - ck
            if on_diagonal:
                s = jnp.where(jnp.concatenate([s_idx <= t_idx] * 2, axis=1), s, -jnp.inf)
            m_old = m_ref[pair]
            m_new = jnp.maximum(m_old, jnp.max(s, axis=0, keepdims=True) + cq)
            pb.append(jnp.exp2((s - (m_new - cq)).astype(BF16)))
            alpha.append(jnp.exp2(m_old - m_new))
            m_ref[pair] = m_new
        return pb, alpha

    ones_rows = jnp.ones((ROW_GROUP, blk), BF16)

    def values(j, pb, alpha):
        for pair in pairs:
            for e in range(2):
                own = slice(pair * LANES + e * DH_F, pair * LANES + (e + 1) * DH_F)
                lanes = slice(e * blk, (e + 1) * blk)
                lhs = jnp.concatenate([vt_ref[0, own, key_rows(j)], ones_rows], axis=0)
                pv = jnp.dot(lhs, pb[pair][:, lanes], preferred_element_type=F32)
                a = alpha[pair][:, lanes]
                acc_ref[own, :] = a * acc_ref[own, :] + pv[0:DH_F, :]
                l_ref[pair, :, lanes] = a * l_ref[pair, :, lanes] + pv[DH_F:DH_F + 1, :]

    def key_block(j, on_diagonal):
        pb, alpha = softmax(j, scores(j), on_diagonal)
        values(j, pb, alpha)

    def loop_body(j, carry):
        key_block(j, False)
        return carry

    lax.fori_loop(0, i, loop_body, 0)
    key_block(i, True)
    inv_l = jnp.concatenate(
        [jnp.broadcast_to(1.0 / l_ref[h // 2][:, (h % 2) * blk:(h % 2 + 1) * blk], (DH_F, blk)) for h in range(H_F)],
        axis=0)
    o_ref[...] = (acc_ref[...] * inv_l).T.astype(o_ref.dtype)


def _fox_prompt(qf, kfr, vtb, ccol, crow, *, batch, seq):
    blk = FOX_BLOCK
    nq = seq // blk
    stat = pltpu.VMEM((H_F // 2, 1, 2 * blk), F32)
    return pl.pallas_call(
        functools.partial(_fox_prompt_kernel, blk=blk),
        grid=(batch, nq),
        in_specs=[pl.BlockSpec((blk, W_HEADS), lambda b, i: (b * nq + i, 0)),
                  pl.BlockSpec((seq, W_HEADS), lambda b, i: (b, 0)),
                  pl.BlockSpec((1, W_HEADS, seq), lambda b, i: (b, 0, 0)),
                  pl.BlockSpec((seq, LANES), lambda b, i: (b, 0)),
                  pl.BlockSpec((N_GATE_ROWS, seq), lambda b, i: (0, b))],
        out_specs=pl.BlockSpec((blk, W_HEADS), lambda b, i: (b * nq + i, 0)),
        out_shape=jax.ShapeDtypeStruct((batch * seq, W_HEADS), BF16),
        scratch_shapes=[pltpu.VMEM((H_F // 2, 2 * blk, LANES), BF16), stat, stat, pltpu.VMEM((W_HEADS, blk), F32)],
        compiler_params=pltpu.CompilerParams(dimension_semantics=("arbitrary", "arbitrary"),
                                             vmem_limit_bytes=VMEM_LIMIT),
        name="fox_prompt",
    )(qf, kfr, vtb, ccol.reshape(batch * seq, LANES), crow)


def _page_bias_kernel(pt_ref, lf_ref, o_ref):
    b = pl.program_id(0)
    n_pages = o_ref.shape[1]
    x = jnp.concatenate([lf_ref[pt_ref[b, p]] for p in range(n_pages)], axis=0)
    t_idx = lax.broadcasted_iota(jnp.int32, (LANES, LANES), 0)
    s_idx = lax.broadcasted_iota(jnp.int32, (LANES, LANES), 1)
    later = (t_idx > s_idx).astype(F32)
    within = jnp.dot(x, later, precision=HI, preferred_element_type=F32)
    total = jnp.sum(x, axis=-1, keepdims=True)
    run = jnp.zeros((H_F, LANES), F32)
    for p in range(n_pages - 1, -1, -1):
        rows = slice(p * H_F, (p + 1) * H_F)
        o_ref[0, p] = (within[rows] + run) * (-LOG2E)
        run = run + total[rows]


def _page_bias(page_table, lf_pool):
    batch, n_pages = page_table.shape
    return pl.pallas_call(
        _page_bias_kernel,
        grid_spec=pltpu.PrefetchScalarGridSpec(
            num_scalar_prefetch=1,
            grid=(batch,),
            in_specs=[_const_spec(lf_pool.shape)],
            out_specs=pl.BlockSpec((1, n_pages, H_F, LANES), lambda b, pt: (b, 0, 0, 0)),
        ),
        out_shape=jax.ShapeDtypeStruct((batch, n_pages, H_F, LANES), F32),
        compiler_params=pltpu.CompilerParams(dimension_semantics=("arbitrary",), vmem_limit_bytes=VMEM_LIMIT),
        name="page_bias",
    )(page_table, lf_pool)


def _paged_attention(pt_ref, q_ref, gr_ref, kn_ref, vn_ref, cp_ref, k_hbm, v_hbm, o_ref,
                     kbuf, vbuf, sem, acc_ref, *, n_groups, phases):
    b = pl.program_id(0)
    nb = pl.num_programs(0)
    group = PAGES_PER_GROUP
    t_new = q_ref.shape[0]
    n_rows = t_new * H_F

    def page_copies(bi, g, sl):
        cps = []
        for j in range(group):
            page = pt_ref[bi, g * group + j]
            cps.append(pltpu.make_async_copy(k_hbm.at[page], kbuf.at[sl, j], sem.at[sl, 0]))
            cps.append(pltpu.make_async_copy(v_hbm.at[page], vbuf.at[sl, j], sem.at[sl, 1]))
        return cps

    def start_group(bi, g, sl):
        for cp in page_copies(bi, g, sl):
            cp.start()

    @pl.when(b == 0)
    def _():
        for a in range(PAGE_SLOTS - 1):
            start_group(0, a, a)

    sub = lax.broadcasted_iota(jnp.int32, (H_F, W_HEADS), 0)
    lane = lax.broadcasted_iota(jnp.int32, (H_F, W_HEADS), 1)
    own_head = (lane // DH_F) == sub
    q = q_ref[...].astype(F32)
    qbd = jnp.concatenate(
        [jnp.where(own_head, jnp.broadcast_to(q[t:t + 1, :], (H_F, W_HEADS)), 0.0) for t in range(t_new)],
        axis=0).astype(BF16)

    gate0 = N_GATE_ROWS - H_F
    lf_new = gr_ref[0, gate0:N_GATE_ROWS, :]
    a_idx = lax.broadcasted_iota(jnp.int32, (t_new, t_new), 0)
    b_idx = lax.broadcasted_iota(jnp.int32, (t_new, t_new), 1)
    c_new = jnp.dot(lf_new, (a_idx <= b_idx).astype(F32), precision=HI, preferred_element_type=F32) * LOG2E
    cq = jnp.concatenate([c_new[:, t:t + 1] for t in range(t_new)], axis=0)

    acc_ref[...] = jnp.zeros_like(acc_ref)

    def body(g, carry, side_work):
        m_i, l_i = carry
        slot = g % PAGE_SLOTS
        ahead = g + (PAGE_SLOTS - 1)

        @pl.when(ahead < n_groups)
        def _():
            start_group(b, ahead, ahead % PAGE_SLOTS)

        @pl.when(jnp.logical_and(ahead >= n_groups, b + 1 < nb))
        def _():
            start_group(b + 1, ahead - n_groups, ahead % PAGE_SLOTS)

        for cp in page_copies(b, g, slot):
            cp.wait()
        side = side_work(g)

        kcat = jnp.concatenate([kbuf[slot, j].astype(BF16) for j in range(group)], axis=1)
        s = jnp.dot(qbd, kcat, preferred_element_type=F32)
        next(side, None)
        first = pl.multiple_of(g * group, group)
        cpg = cp_ref[0, pl.ds(first, group)]
        bias = jnp.concatenate(
            [jnp.broadcast_to(cpg[j][None], (t_new, H_F, LANES)).reshape(n_rows, LANES) for j in range(group)],
            axis=1)
        s = s + (cq - bias)
        m_new = jnp.maximum(m_i, jnp.max(s, axis=-1, keepdims=True))
        p = jnp.exp2(s - m_new)
        alpha = jnp.exp2(m_i - m_new)
        l_new = alpha * l_i + jnp.sum(p, axis=-1, keepdims=True)
        next(side, None)
        vcat = jnp.concatenate([vbuf[slot, j].T.astype(BF16) for j in range(group)], axis=0)
        pv = jnp.dot(p.astype(BF16), vcat, preferred_element_type=F32)
        next(side, None)
        acc_ref[...] = alpha * acc_ref[...] + pv
        return m_new, l_new

    carry = (jnp.full((n_rows, 1), -jnp.inf, F32), jnp.zeros((n_rows, 1), F32))
    start = 0
    for stop, side_work in phases:
        carry = lax.fori_loop(start, stop, functools.partial(body, side_work=side_work), carry)
        start = stop
    assert start == n_groups
    m_i, l_i = carry

    kn = kn_ref[...].astype(BF16)
    vn = vn_ref[...].astype(BF16)
    s = lax.dot_general(qbd, kn, NT, preferred_element_type=F32)
    ck = jnp.broadcast_to(c_new[None], (t_new, H_F, t_new)).reshape(n_rows, t_new)
    s = s + (cq - ck)
    r_idx = lax.broadcasted_iota(jnp.int32, (n_rows, t_new), 0)
    k_idx = lax.broadcasted_iota(jnp.int32, (n_rows, t_new), 1)
    s = jnp.where(k_idx <= r_idx // H_F, s, -jnp.inf)
    m_new = jnp.maximum(m_i, jnp.max(s, axis=-1, keepdims=True))
    p = jnp.exp2(s - m_new)
    alpha = jnp.exp2(m_i - m_new)
    l_fin = alpha * l_i + jnp.sum(p, axis=-1, keepdims=True)
    acc = alpha * acc_ref[...] + jnp.dot(p.astype(BF16), vn, preferred_element_type=F32)
    out = acc / l_fin
    o_ref[...] = jnp.concatenate(
        [jnp.sum(jnp.where(own_head, out[t * H_F:(t + 1) * H_F, :], 0.0), axis=0, keepdims=True)
         for t in range(t_new)], axis=0).astype(o_ref.dtype)


def _merge_residual(x_ref, hm_ref, hf_ref, wo_ref, g2_ref):
    x1 = (x_ref[...]
          + jnp.dot(hm_ref[...].astype(BF16), wo_ref[0:W_HEADS, :], preferred_element_type=F32)
          + jnp.dot(hf_ref[...].astype(BF16), wo_ref[W_HEADS:2 * W_HEADS, :], preferred_element_type=F32))
    return x1, _rms(x1, g2_ref[...]).astype(BF16)


def _ffn_sample_kernel(pt_ref, x_ref, hm_ref, hf_ref, wo_ref, g2_ref, wg_ref, wu_ref, wd_ref, g3_ref,
                       q_ref, gr_ref, kn_ref, vn_ref, cp_ref, k_hbm, v_hbm, y_ref, o_ref,
                       h_ref, x2_ref, act_ref, kbuf, vbuf, sem, acc_ref, *, n_groups, final_norm):
    x1, h = _merge_residual(x_ref, hm_ref, hf_ref, wo_ref, g2_ref)
    h_ref[...] = h
    x2_ref[...] = x1
    d_ff = wg_ref.shape[1]
    up_groups = d_ff // FFN_CHUNK
    down_groups = d_ff // DOWN_CHUNK
    tail = d_ff - down_groups * DOWN_CHUNK
    assert up_groups + down_groups == n_groups and tail in (0, FFN_CHUNK)

    def up_chunk(c):
        cols = pl.ds(pl.multiple_of(c * FFN_CHUNK, FFN_CHUNK), FFN_CHUNK)
        hh = h_ref[...]
        gate = jnp.dot(hh, wg_ref[:, cols], preferred_element_type=F32)
        yield
        up = jnp.dot(hh, wu_ref[:, cols], preferred_element_type=F32)
        act_ref[:, cols] = (gate * _sigmoid(gate) * up).astype(BF16)
        yield

    def last_up_chunk(c):
        yield from up_chunk(c)
        cols = pl.ds(pl.multiple_of(c * FFN_CHUNK, FFN_CHUNK), FFN_CHUNK)
        x2_ref[...] += jnp.dot(act_ref[:, cols], wd_ref[cols, :], preferred_element_type=F32)
        yield

    def down_slice(g):
        cols = pl.ds(pl.multiple_of((g - up_groups) * DOWN_CHUNK, DOWN_CHUNK), DOWN_CHUNK)
        yield
        x2_ref[...] += jnp.dot(act_ref[:, cols], wd_ref[cols, :], preferred_element_type=F32)
        yield

    if tail:
        phases = [(up_groups - 1, up_chunk), (up_groups, last_up_chunk), (n_groups, down_slice)]
    else:
        phases = [(up_groups, up_chunk), (n_groups, down_slice)]
    _paged_attention(pt_ref, q_ref, gr_ref, kn_ref, vn_ref, cp_ref, k_hbm, v_hbm, o_ref, kbuf, vbuf, sem, acc_ref,
                     n_groups=n_groups, phases=phases)
    x2 = x2_ref[...]
    y_ref[...] = _rms(x2, g3_ref[...]) if final_norm else x2


def _merge_ffn_and_fox_sample(x2d, hm, hf, wo, g2, wg, wu, wd, g3, page_table, qf, grow3, k_new, v_new, page_bias,
                              k_pool, v_pool, *, final_norm):
    rows_total, d_model = x2d.shape
    batch, n_pages = page_table.shape
    t_new = qf.shape[0] // batch
    n_groups = n_pages // PAGES_PER_GROUP
    tm = rows_total // batch
    d_ff = wg.shape[1]
    assert n_pages % PAGES_PER_GROUP == 0 and n_groups % PAGE_SLOTS == 0
    assert rows_total % batch == 0 and tm % 16 == 0 and d_ff % FFN_CHUNK == 0 and W_HEADS % FFN_CHUNK == 0
    page_rows, page_len = k_pool.shape[1], k_pool.shape[2]
    row = lambda w: pl.BlockSpec((tm, w), lambda b, pt: (b, 0))
    tok = lambda w: pl.BlockSpec((t_new, w), lambda b, pt: (b, 0))
    return pl.pallas_call(
        functools.partial(_ffn_sample_kernel, n_groups=n_groups, final_norm=final_norm),
        grid_spec=pltpu.PrefetchScalarGridSpec(
            num_scalar_prefetch=1,
            grid=(batch,),
            in_specs=[row(d_model), row(W_HEADS), row(W_HEADS), _const_spec(wo.shape), _const_spec(g2.shape),
                      _const_spec(wg.shape), _const_spec(wu.shape), _const_spec(wd.shape), _const_spec(g3.shape),
                      tok(W_HEADS),
                      pl.BlockSpec((1, N_GATE_ROWS, t_new), lambda b, pt: (b, 0, 0)),
                      tok(W_HEADS), tok(W_HEADS),
                      pl.BlockSpec((1, n_pages, H_F, LANES), lambda b, pt: (b, 0, 0, 0)),
                      pl.BlockSpec(memory_space=pl.ANY), pl.BlockSpec(memory_space=pl.ANY)],
            out_specs=[row(d_model), tok(W_HEADS)],
            scratch_shapes=[pltpu.VMEM((tm, d_model), BF16),
                            pltpu.VMEM((tm, d_model), F32),
                            pltpu.VMEM((tm, d_ff), BF16),
                            pltpu.VMEM((PAGE_SLOTS, PAGES_PER_GROUP, page_rows, page_len), F32),
                            pltpu.VMEM((PAGE_SLOTS, PAGES_PER_GROUP, page_rows, page_len), F32),
                            pltpu.SemaphoreType.DMA((PAGE_SLOTS, 2)),
                            pltpu.VMEM((t_new * H_F, W_HEADS), F32)],
        ),
        out_shape=[jax.ShapeDtypeStruct((rows_total, d_model), F32),
                   jax.ShapeDtypeStruct((batch * t_new, W_HEADS), F32)],
        compiler_params=pltpu.CompilerParams(dimension_semantics=("arbitrary",), vmem_limit_bytes=VMEM_LIMIT_WEIGHTS),
        name="ffn_and_fox_sample",
    )(page_table, x2d, hm, hf, wo, g2, wg, wu, wd, g3, qf, grow3, k_new, v_new, page_bias, k_pool, v_pool)


def _ffn_kernel(x_ref, hm_ref, hf_ref, wo_ref, g2_ref, wg_ref, wu_ref, wd_ref, g3_ref, y_ref, *, final_norm):
    x1, h = _merge_residual(x_ref, hm_ref, hf_ref, wo_ref, g2_ref)
    gate = jnp.dot(h, wg_ref[...], preferred_element_type=F32)
    up = jnp.dot(h, wu_ref[...], preferred_element_type=F32)
    act = (gate * _sigmoid(gate) * up).astype(BF16)
    x2 = x1 + jnp.dot(act, wd_ref[...], preferred_element_type=F32)
    y_ref[...] = _rms(x2, g3_ref[...]) if final_norm else x2


def _merge_ffn(x2d, hm, hf, wo, g2, wg, wu, wd, g3, *, final_norm):
    rows_total, d_model = x2d.shape
    tm = min(FFN_ROWS, rows_total)
    row_spec = lambda w: pl.BlockSpec((tm, w), lambda i: (i, 0))
    return pl.pallas_call(
        functools.partial(_ffn_kernel, final_norm=final_norm),
        grid=(rows_total // tm,),
        in_specs=[row_spec(d_model), row_spec(W_HEADS), row_spec(W_HEADS), _const_spec(wo.shape),
                  _const_spec(g2.shape), _const_spec(wg.shape), _const_spec(wu.shape), _const_spec(wd.shape),
                  _const_spec(g3.shape)],
        out_specs=row_spec(d_model),
        out_shape=jax.ShapeDtypeStruct((rows_total, d_model), F32),
        compiler_params=pltpu.CompilerParams(dimension_semantics=("arbitrary",), vmem_limit_bytes=VMEM_LIMIT_WEIGHTS),
        name="merge_ffn",
    )(x2d, hm, hf, wo, g2, wg, wu, wd, g3)


def kernel(x_prompt, x_sample, cache_fox_k, cache_fox_v, cache_fox_logf, page_table, state_mlstm_C,
           state_mlstm_n, state_mlstm_m, norm_mix_g, w_in, b_m_igate, b_m_fgate, b_f_fgate, mlstm_head_g,
           w_out, norm_ffn_g, w_gate, w_up, w_down, norm_final_g):
    depth = w_in.shape[0]
    batch, seq, d_model = x_prompt.shape
    dec_batch, dec_seq, _ = x_sample.shape
    n_pool, page_size = cache_fox_k.shape[1], cache_fox_k.shape[2]
    xp = x_prompt.reshape(batch * seq, d_model)
    xs = x_sample.reshape(dec_batch * dec_seq, d_model)
    g_final = norm_final_g.reshape(1, d_model)
    pk, pv, plf, pc, pn, pm = [], [], [], [], [], []
    sk, sv, slf, sc, sn, sm = [], [], [], [], [], []
    gate0 = N_GATE_ROWS - H_F
    o_gm = 4 * W_HEADS
    o_qf = o_gm + 2 * H_M
    o_gf = o_qf + 3 * W_HEADS
    for l in range(depth):
        wt = jnp.swapaxes(w_in[l], 0, 1)
        wt_main = jnp.concatenate([wt[0:o_gm], wt[o_qf:o_qf + W_HEADS]], axis=0).astype(BF16)
        wt_kv = wt[o_qf + W_HEADS:o_gf].astype(BF16)
        wt_g = jnp.concatenate([wt[o_gm:o_qf], wt[o_gf:o_gf + H_F],
                                jnp.zeros((LANES - N_GATE_ROWS, d_model), F32)], axis=0).astype(BF16)
        bias = jnp.concatenate([b_m_igate[l], b_m_fgate[l], b_f_fgate[l],
                                jnp.zeros((LANES - N_GATE_ROWS,), F32)]).astype(F32)
        bcol = bias.reshape(1, LANES)
        brow = bias[:N_GATE_ROWS].reshape(N_GATE_ROWS, 1)
        g_mix = norm_mix_g[l].reshape(1, d_model)
        g_ffn = norm_ffn_g[l].reshape(1, d_model)
        head_g = mlstm_head_g[l].reshape(1, W_HEADS)
        wo = w_out[l].astype(BF16)
        wg = w_gate[l].astype(BF16)
        wu = w_up[l].astype(BF16)
        wd = w_down[l].astype(BF16)

        qm, km, vmt, omt, qf, kt, vt, kfr, vtb, gcol, grow = _project(
            xp, g_mix, wt_main, wt_kv, wt_g, bcol, brow, batch=batch, seq=seq, kv_transposed=True, act_dtype=BF16)
        chunk = min(LANES, seq)
        grow3 = grow.reshape(N_GATE_ROWS, batch, seq).transpose(1, 0, 2)
        hm, c_p, n_p, m_p = _mlstm_prompt(
            qm, km, vmt, omt, gcol, grow3, head_g,
            jnp.zeros((batch, H_M, DH_M, DH_M), F32), jnp.zeros((batch, H_M, DH_M), F32),
            jnp.zeros((batch, H_M), F32), batch=batch, seq=seq, chunk=chunk, out_dtype=BF16)
        ccol, crow = _fox_cumsum(grow, batch=batch, seq=seq)
        hf = _fox_prompt(qf, kfr, vtb, ccol, crow, batch=batch, seq=seq)
        hm_p, hf_p = hm, hf
        pk.append(kt.reshape(batch, H_F, DH_F, seq).transpose(0, 3, 1, 2))
        pv.append(vt.reshape(batch, H_F, DH_F, seq).transpose(0, 3, 1, 2))
        plf.append(grow3[:, gate0:, :].transpose(0, 2, 1))
        pc.append(c_p); pn.append(n_p); pm.append(m_p)

        qm, km, vm, om, qf, k_new, v_new, gcol, grow = _project(
            xs, g_mix, wt_main, wt_kv, wt_g, bcol, brow, batch=dec_batch, seq=dec_seq, kv_transposed=False,
            act_dtype=F32)
        grow3 = grow.reshape(N_GATE_ROWS, dec_batch, dec_seq).transpose(1, 0, 2)
        hm, c_s, n_s, m_s = _mlstm(
            qm, km, vm, om, gcol, grow3, head_g,
            state_mlstm_C[l].astype(F32), state_mlstm_n[l].astype(F32), state_mlstm_m[l].astype(F32),
            batch=dec_batch, seq=dec_seq, chunk=dec_seq, out_dtype=F32)
        k_pool = cache_fox_k[l].transpose(0, 2, 3, 1).reshape(n_pool, W_HEADS, page_size)
        v_pool = cache_fox_v[l].transpose(0, 2, 3, 1).reshape(n_pool, W_HEADS, page_size)
        lf_pool = cache_fox_logf[l].transpose(0, 2, 1)
        page_bias = _page_bias(page_table, lf_pool)
        xp, hf = _merge_ffn_and_fox_sample(
            xp, hm_p, hf_p, wo, g_ffn, wg, wu, wd, g_final, page_table, qf, grow3, k_new, v_new, page_bias,
            k_pool, v_pool, final_norm=(l == depth - 1))
        xs = _merge_ffn(xs, hm, hf, wo, g_ffn, wg, wu, wd, g_final, final_norm=(l == depth - 1))
        sk.append(k_new.reshape(dec_batch, dec_seq, H_F, DH_F))
        sv.append(v_new.reshape(dec_batch, dec_seq, H_F, DH_F))
        slf.append(grow3[:, gate0:, :].transpose(0, 2, 1))
        sc.append(c_s); sn.append(n_s); sm.append(m_s)

    st = lambda a, ref: jnp.stack(a, axis=0).astype(ref.dtype)
    return (xp.reshape(batch, seq, d_model), xs.reshape(dec_batch, dec_seq, d_model),
            st(pk, cache_fox_k), st(pv, cache_fox_v), st(plf, cache_fox_logf),
            st(pc, state_mlstm_C), st(pn, state_mlstm_n), st(pm, state_mlstm_m),
            st(sk, cache_fox_k), st(sv, cache_fox_v), st(slf, cache_fox_logf),
            st(sc, state_mlstm_C), st(sn, state_mlstm_n), st(sm, state_mlstm_m))
```

```python
import functools

import jax
import jax.numpy as jnp
from jax import lax
from jax.experimental import pallas as pl
from jax.experimental.pallas import tpu as pltpu

F32 = jnp.float32
BF16 = jnp.bfloat16
HI = lax.Precision.HIGHEST
NT = (((1,), (1,)), ((), ()))
TN = (((0,), (0,)), ((), ()))

EPS = 1e-6
LOG2E = 1.4426950408889634
H_M = 4
DH_M = 128
H_F = 8
DH_F = 64
W_HEADS = 512
N_GATE_ROWS = 16
LANES = 128
ROW_GROUP = 16
MIB = 1024 * 1024
V7X_VMEM_BYTES = 64 * MIB
VMEM_LIMIT = V7X_VMEM_BYTES * 3 // 4
VMEM_LIMIT_WEIGHTS = V7X_VMEM_BYTES - 6 * MIB

PROJ_ROWS = 1024
FFN_ROWS = 512
FFN_CHUNK = 256
DOWN_CHUNK = 512
MLSTM_BATCH = 8
MLSTM_INTERLEAVE = 8
FOX_BLOCK = 512
PAGES_PER_GROUP = 8
PAGE_SLOTS = 4


def _rms(x, g):
    return x * lax.rsqrt(jnp.mean(x * x, axis=-1, keepdims=True) + EPS) * g


def _log_sigmoid(x):
    return jnp.minimum(x, 0.0) - jnp.log1p(jnp.exp(-jnp.abs(x)))


def _sigmoid(x):
    return 1.0 / (1.0 + jnp.exp(-x))


def _const_spec(shape):
    return pl.BlockSpec(shape, lambda *_: (0,) * len(shape), pipeline_mode=pl.Buffered(1))


def _proj_kernel(x_ref, g_ref, wt_ref, wkv_ref, wg_ref, bcol_ref, *outs, kv_transposed):
    if kv_transposed:
        qm_ref, km_ref, vm_ref, om_ref, qf_ref, kf_ref, vf_ref, kfb_ref, vfb_ref, gcol_ref, grow_ref = outs
    else:
        qm_ref, km_ref, vm_ref, om_ref, qf_ref, kf_ref, vf_ref, gcol_ref, grow_ref = outs
    h = _rms(x_ref[...], g_ref[...]).astype(BF16)
    rows = h.shape[0]

    def mm(i):
        w = wt_ref[i * W_HEADS:(i + 1) * W_HEADS, :]
        return lax.dot_general(h, w, NT, preferred_element_type=F32)

    def mm_t(w):
        return lax.dot_general(w, h, NT, preferred_element_type=F32)

    qm_ref[...] = mm(0).astype(qm_ref.dtype)
    km_ref[...] = (mm(1) * (DH_M ** -0.5)).astype(km_ref.dtype)
    if kv_transposed:
        vm_ref[0] = mm_t(wt_ref[2 * W_HEADS:3 * W_HEADS, :]).astype(vm_ref.dtype)
        om_ref[0] = mm_t(wt_ref[3 * W_HEADS:4 * W_HEADS, :]).astype(om_ref.dtype)
    else:
        vm_ref[...] = mm(2).astype(vm_ref.dtype)
        om_ref[...] = mm(3).astype(om_ref.dtype)
    qf_ref[...] = (mm(4) * (DH_F ** -0.5 * LOG2E)).astype(qf_ref.dtype)
    if kv_transposed:
        kt = mm_t(wkv_ref[0:W_HEADS, :])
        kf_ref[0] = kt
        kfb_ref[...] = kt.T.astype(BF16)
        vt = mm_t(wkv_ref[W_HEADS:2 * W_HEADS, :])
        vf_ref[0] = vt
        vfb_ref[0] = vt.astype(BF16)
    else:
        kf_ref[...] = lax.dot_general(h, wkv_ref[0:W_HEADS, :], NT, preferred_element_type=F32)
        vf_ref[...] = lax.dot_general(h, wkv_ref[W_HEADS:2 * W_HEADS, :], NT, preferred_element_type=F32)
    pre_c = lax.dot_general(h, wg_ref[...], NT, preferred_element_type=F32) + bcol_ref[...]
    lane = lax.broadcasted_iota(jnp.int32, (rows, LANES), 1)
    gates = jnp.where(lane < H_M, pre_c, _log_sigmoid(pre_c))
    gcol_ref[...] = gates
    grow_ref[...] = gates.T[0:N_GATE_ROWS, :]


def _project(x2d, g, wt_main, wt_kv, wt_g, bcol, *, batch, seq, kv_transposed, act_dtype):
    rows_total, d_model = x2d.shape
    tm = min(PROJ_ROWS, rows_total)
    steps = rows_total // tm
    per_seq = max(seq // tm, 1)
    row_spec = lambda w: pl.BlockSpec((tm, w), lambda i: (i, 0))
    in_specs = [row_spec(d_model), _const_spec(g.shape), _const_spec(wt_main.shape), _const_spec(wt_kv.shape),
                _const_spec(wt_g.shape), _const_spec(bcol.shape)]
    act = jax.ShapeDtypeStruct((rows_total, W_HEADS), act_dtype)
    act32 = jax.ShapeDtypeStruct((rows_total, W_HEADS), F32)
    out_shape = [act, act, act, act32, act]
    out_specs = [row_spec(W_HEADS)] * 5
    if kv_transposed:
        kv_spec = pl.BlockSpec((1, W_HEADS, tm), lambda i: (i // per_seq, 0, i % per_seq))
        out_shape[2] = jax.ShapeDtypeStruct((batch, W_HEADS, seq), act_dtype)
        out_shape[3] = jax.ShapeDtypeStruct((batch, W_HEADS, seq), F32)
        out_specs[2] = out_specs[3] = kv_spec
        out_shape += [jax.ShapeDtypeStruct((batch, W_HEADS, seq), F32)] * 2
        out_shape += [jax.ShapeDtypeStruct((rows_total, W_HEADS), BF16),
                      jax.ShapeDtypeStruct((batch, W_HEADS, seq), BF16)]
        out_specs += [kv_spec, kv_spec, row_spec(W_HEADS), kv_spec]
    else:
        out_shape += [act32, act32]
        out_specs += [row_spec(W_HEADS)] * 2
    out_shape += [jax.ShapeDtypeStruct((rows_total, LANES), F32),
                  jax.ShapeDtypeStruct((N_GATE_ROWS, rows_total), F32)]
    out_specs += [row_spec(LANES), pl.BlockSpec((N_GATE_ROWS, tm), lambda i: (0, i))]
    return pl.pallas_call(
        functools.partial(_proj_kernel, kv_transposed=kv_transposed),
        grid=(steps,),
        in_specs=in_specs,
        out_specs=out_specs,
        out_shape=out_shape,
        compiler_params=pltpu.CompilerParams(dimension_semantics=("arbitrary",), vmem_limit_bytes=VMEM_LIMIT),
        name="proj",
    )(x2d, g, wt_main, wt_kv, wt_g, bcol)


def _mlstm_kernel(q_ref, k_ref, v_ref, om_ref, gc_ref, gr_ref, hg_ref, c0_ref, n0_ref, m0_ref,
                  hm_ref, c_ref, n_ref, m_ref, *, bb, chunk):
    @pl.when(pl.program_id(1) == 0)
    def _():
        c_ref[...] = c0_ref[...]
        n_ref[...] = n0_ref[...]
        m_ref[...] = m0_ref[...]

    t_idx = lax.broadcasted_iota(jnp.int32, (chunk, chunk), 0)
    s_idx = lax.broadcasted_iota(jnp.int32, (chunk, chunk), 1)
    causal = s_idx <= t_idx
    tril = causal.astype(F32)
    triu = (t_idx <= s_idx).astype(F32)

    def per_group(i, carry):
        bs = [i * MLSTM_INTERLEAVE + j for j in range(MLSTM_INTERLEAVE)]
        gc = [gc_ref[b] for b in bs]
        gr = [gr_ref[b] for b in bs]
        bc = [jnp.dot(tril, g, precision=HI, preferred_element_type=F32) for g in gc]
        br = [jnp.dot(g, triu, precision=HI, preferred_element_type=F32) for g in gr]
        n_all = [n_ref[b] for b in bs]
        m_all = [m_ref[pl.ds(b, 1), :] for b in bs]
        chains = [(j, h) for j in range(MLSTM_INTERLEAVE) for h in range(H_M)]
        ids = range(len(chains))
        sls = [slice(h * DH_M, (h + 1) * DH_M) for _, h in chains]
        c_prev = [c_ref[bs[j], h] for j, h in chains]
        n_prev = [n_all[j][h:h + 1, :] for j, h in chains]
        m_prev = [m_all[j][:, h:h + 1] for j, h in chains]
        qh = [q_ref[bs[j], :, sls[c]].astype(BF16) for c, (j, _) in enumerate(chains)]
        kh = [k_ref[bs[j], :, sls[c]].astype(BF16) for c, (j, _) in enumerate(chains)]
        vh = [v_ref[bs[j], :, sls[c]].astype(BF16) for c, (j, _) in enumerate(chains)]
        qk = [lax.dot_general(qh[c], kh[c], NT, preferred_element_type=F32) for c in ids]
        cq = [lax.dot_general(qh[c], c_prev[c].astype(BF16), NT, preferred_element_type=F32) for c in ids]
        li_c = [gc[j][:, h:h + 1] for j, h in chains]
        li_r = [gr[j][h:h + 1, :] for j, h in chains]
        b_c = [bc[j][:, H_M + h:H_M + h + 1] for j, h in chains]
        b_r = [br[j][H_M + h:H_M + h + 1, :] for j, h in chains]
        dmat = [jnp.where(causal, b_c[c] - b_r[c] + li_r[c], -jnp.inf) for c in ids]
        inter = [m_prev[c] + b_c[c] for c in ids]
        m_t = [jnp.maximum(inter[c], jnp.max(dmat[c], axis=-1, keepdims=True)) for c in ids]
        smat = [qk[c] * jnp.exp(dmat[c] - m_t[c]) for c in ids]
        w_inter = [jnp.exp(inter[c] - m_t[c]) for c in ids]
        sv = [jnp.dot(smat[c].astype(BF16), vh[c], preferred_element_type=F32) for c in ids]
        outs, vw, w_s, decay, m_new = [], [], [], [], []
        for c, (j, h) in enumerate(chains):
            num = w_inter[c] * cq[c] + sv[c]
            nq = jnp.sum(qh[c].astype(F32) * n_prev[c], axis=-1, keepdims=True)
            den = w_inter[c] * nq + jnp.sum(smat[c], axis=-1, keepdims=True)
            hh = num / jnp.maximum(jnp.abs(den), jnp.exp(-m_t[c]))
            hn = hh * lax.rsqrt(jnp.mean(hh * hh, axis=-1, keepdims=True) + EPS) * hg_ref[:, sls[c]]
            outs.append((hn * _sigmoid(om_ref[bs[j], :, sls[c]].astype(F32))).astype(hm_ref.dtype))
            b_last = b_c[c][chunk - 1:chunk, :]
            m_new.append(m_t[c][chunk - 1:chunk, :])
            decay.append(jnp.exp(m_prev[c] + b_last - m_new[c]))
            w_s.append(jnp.exp(li_c[c] + b_last - b_c[c] - m_new[c]))
            vw.append((vh[c].astype(F32) * w_s[c]).astype(BF16))
        for c, (j, h) in enumerate(chains):
            c_new = decay[c] * c_prev[c] + lax.dot_general(vw[c], kh[c], TN, preferred_element_type=F32)
            n_new = decay[c] * n_prev[c] + jnp.sum(kh[c].astype(F32) * w_s[c], axis=0, keepdims=True)
            hm_ref[bs[j], :, sls[c]] = outs[c]
            c_ref[bs[j], h] = c_new
            n_ref[bs[j], h:h + 1, :] = n_new
            m_ref[pl.ds(bs[j], 1), h:h + 1] = m_new[c]
        return carry

    lax.fori_loop(0, bb // MLSTM_INTERLEAVE, per_group, 0)


def _mlstm(qm, km, vm, om, gcol, grow3, head_g, c0, n0, m0, *, batch, seq, chunk, out_dtype):
    bb = MLSTM_BATCH
    n_chunks = seq // chunk
    as3 = lambda a: a.reshape(batch, seq, a.shape[-1])
    tok = lambda w: pl.BlockSpec((bb, chunk, w), lambda g, c: (g, c, 0))
    state = lambda shape: pl.BlockSpec((bb,) + shape, lambda g, c: (g,) + (0,) * len(shape))
    in_specs = [tok(W_HEADS), tok(W_HEADS), tok(W_HEADS), tok(W_HEADS), tok(LANES),
                pl.BlockSpec((bb, N_GATE_ROWS, chunk), lambda g, c: (g, 0, c)),
                _const_spec(head_g.shape),
                state((H_M, DH_M, DH_M)), state((H_M, DH_M)), state((H_M,))]
    out_specs = [tok(W_HEADS), state((H_M, DH_M, DH_M)), state((H_M, DH_M)), state((H_M,))]
    out_shape = [jax.ShapeDtypeStruct((batch, seq, W_HEADS), out_dtype),
                 jax.ShapeDtypeStruct(c0.shape, F32), jax.ShapeDtypeStruct(n0.shape, F32),
                 jax.ShapeDtypeStruct(m0.shape, F32)]
    hm, c_new, n_new, m_new = pl.pallas_call(
        functools.partial(_mlstm_kernel, bb=bb, chunk=chunk),
        grid=(batch // bb, n_chunks),
        in_specs=in_specs,
        out_specs=out_specs,
        out_shape=out_shape,
        compiler_params=pltpu.CompilerParams(dimension_semantics=("arbitrary", "arbitrary"),
                                             vmem_limit_bytes=VMEM_LIMIT),
        name="mlstm",
    )(as3(qm), as3(km), as3(vm), as3(om), as3(gcol), grow3, head_g, c0, n0, m0)
    return hm.reshape(batch * seq, W_HEADS), c_new, n_new, m_new


def _mlstm_t_kernel(q_ref, k_ref, vt_ref, omt_ref, gc_ref, gr_ref, hg_ref, c0_ref, n0_ref, m0_ref,
                    hm_ref, c_ref, n_ref, m_ref, *, bb, chunk):
    @pl.when(pl.program_id(1) == 0)
    def _():
        c_ref[...] = c0_ref[...]
        n_ref[...] = n0_ref[...]
        m_ref[...] = m0_ref[...]

    s_idx = lax.broadcasted_iota(jnp.int32, (chunk, chunk), 0)
    t_idx = lax.broadcasted_iota(jnp.int32, (chunk, chunk), 1)
    causal = s_idx <= t_idx
    tril = (t_idx <= s_idx).astype(F32)
    triu = causal.astype(F32)

    def per_group(i, carry):
        bs = [i * MLSTM_INTERLEAVE + j for j in range(MLSTM_INTERLEAVE)]
        gc = [gc_ref[b] for b in bs]
        gr = [gr_ref[b] for b in bs]
        bc = [jnp.dot(tril, g, precision=HI, preferred_element_type=F32) for g in gc]
        br = [jnp.dot(g, triu, precision=HI, preferred_element_type=F32) for g in gr]
        n_all = [n_ref[b] for b in bs]
        m_all = [m_ref[pl.ds(b, 1), :] for b in bs]
        chains = [(j, h) for j in range(MLSTM_INTERLEAVE) for h in range(H_M)]
        ids = range(len(chains))
        sls = [slice(h * DH_M, (h + 1) * DH_M) for _, h in chains]
        c_prev = [c_ref[bs[j], h] for j, h in chains]
        n_prev = [n_all[j][h:h + 1, :] for j, h in chains]
        m_prev = [m_all[j][:, h:h + 1] for j, h in chains]
        qh = [q_ref[bs[j], :, sls[c]].astype(BF16) for c, (j, _) in enumerate(chains)]
        kh = [k_ref[bs[j], :, sls[c]].astype(BF16) for c, (j, _) in enumerate(chains)]
        vth = [vt_ref[bs[j], sls[c], :].astype(BF16) for c, (j, _) in enumerate(chains)]
        qk = [lax.dot_general(kh[c], qh[c], NT, preferred_element_type=F32) for c in ids]
        cq = [lax.dot_general(c_prev[c].astype(BF16), qh[c], NT, preferred_element_type=F32) for c in ids]
        nq = [lax.dot_general(jnp.broadcast_to(n_prev[c], (8, DH_M)).astype(BF16), qh[c], NT,
                              preferred_element_type=F32)[0:1, :] for c in ids]
        li_r = [gr[j][h:h + 1, :] for j, h in chains]
        b_r = [br[j][H_M + h:H_M + h + 1, :] for j, h in chains]
        col = [gc[j][:, h:h + 1] - bc[j][:, H_M + h:H_M + h + 1] for j, h in chains]
        dmat = [jnp.where(causal, b_r[c] + col[c], -jnp.inf) for c in ids]
        inter = [m_prev[c] + b_r[c] for c in ids]
        m_t = [jnp.maximum(inter[c], jnp.max(dmat[c], axis=0, keepdims=True)) for c in ids]
        smat = [qk[c] * jnp.exp(dmat[c] - m_t[c]) for c in ids]
        w_inter = [jnp.exp(inter[c] - m_t[c]) for c in ids]
        sv = [jnp.dot(vth[c], smat[c].astype(BF16), preferred_element_type=F32) for c in ids]
        outs, vw, w_s_c, decay, m_new = [], [], [], [], []
        for c, (j, h) in enumerate(chains):
            num = w_inter[c] * cq[c] + sv[c]
            den = w_inter[c] * nq[c] + jnp.sum(smat[c], axis=0, keepdims=True)
            hh = num * (1.0 / jnp.maximum(jnp.abs(den), jnp.exp(-m_t[c])))
            hn = hh * lax.rsqrt(jnp.mean(hh * hh, axis=0, keepdims=True) + EPS) * hg_ref[sls[c], :]
            gate = _sigmoid(omt_ref[bs[j], sls[c], :].astype(F32))
            outs.append((hn * gate).T.astype(hm_ref.dtype))
            b_last = b_r[c][:, chunk - 1:chunk]
            m_new.append(m_t[c][:, chunk - 1:chunk])
            decay.append(jnp.exp(m_prev[c] + b_last - m_new[c]))
            w_s_r = jnp.exp(li_r[c] + b_last - b_r[c] - m_new[c])
            w_s_c.append(jnp.exp(col[c] + b_last - m_new[c]))
            vw.append((vth[c].astype(F32) * w_s_r).astype(BF16))
        for c, (j, h) in enumerate(chains):
            c_new = decay[c] * c_prev[c] + jnp.dot(vw[c], kh[c], preferred_element_type=F32)
            n_new = decay[c] * n_prev[c] + jnp.sum(kh[c].astype(F32) * w_s_c[c], axis=0, keepdims=True)
            hm_ref[bs[j], :, sls[c]] = outs[c]
            c_ref[bs[j], h] = c_new
            n_ref[bs[j], h:h + 1, :] = n_new
            m_ref[pl.ds(bs[j], 1), h:h + 1] = m_new[c]
        return carry

    lax.fori_loop(0, bb // MLSTM_INTERLEAVE, per_group, 0)


def _mlstm_prompt(qm, km, vmt, omt, gcol, grow3, head_g, c0, n0, m0, *, batch, seq, chunk, out_dtype):
    bb = MLSTM_BATCH
    n_chunks = seq // chunk
    as3 = lambda a: a.reshape(batch, seq, a.shape[-1])
    tok = lambda w: pl.BlockSpec((bb, chunk, w), lambda g, c: (g, c, 0))
    tok_t = pl.BlockSpec((bb, W_HEADS, chunk), lambda g, c: (g, 0, c))
    state = lambda shape: pl.BlockSpec((bb,) + shape, lambda g, c: (g,) + (0,) * len(shape))
    hg_col = head_g.reshape(W_HEADS, 1)
    in_specs = [tok(W_HEADS), tok(W_HEADS), tok_t, tok_t, tok(LANES),
                pl.BlockSpec((bb, N_GATE_ROWS, chunk), lambda g, c: (g, 0, c)),
                _const_spec(hg_col.shape),
                state((H_M, DH_M, DH_M)), state((H_M, DH_M)), state((H_M,))]
    out_specs = [tok(W_HEADS), state((H_M, DH_M, DH_M)), state((H_M, DH_M)), state((H_M,))]
    out_shape = [jax.ShapeDtypeStruct((batch, seq, W_HEADS), out_dtype),
                 jax.ShapeDtypeStruct(c0.shape, F32), jax.ShapeDtypeStruct(n0.shape, F32),
                 jax.ShapeDtypeStruct(m0.shape, F32)]
    hm, c_new, n_new, m_new = pl.pallas_call(
        functools.partial(_mlstm_t_kernel, bb=bb, chunk=chunk),
        grid=(batch // bb, n_chunks),
        in_specs=in_specs,
        out_specs=out_specs,
        out_shape=out_shape,
        compiler_params=pltpu.CompilerParams(dimension_semantics=("arbitrary", "arbitrary"),
                                             vmem_limit_bytes=VMEM_LIMIT),
        name="mlstm_prompt",
    )(as3(qm), as3(km), vmt, omt, as3(gcol), grow3, hg_col, c0, n0, m0)
    return hm.reshape(batch * seq, W_HEADS), c_new, n_new, m_new


def _fox_cumsum_kernel(gr_ref, cc_ref, cr_ref):
    seq = gr_ref.shape[1]
    t_idx = lax.broadcasted_iota(jnp.int32, (LANES, LANES), 0)
    s_idx = lax.broadcasted_iota(jnp.int32, (LANES, LANES), 1)
    triu = (t_idx <= s_idx).astype(F32)
    blks = [slice(j * LANES, (j + 1) * LANES) for j in range(seq // LANES)]
    rbs = [jnp.dot(gr_ref[:, blk], triu, precision=HI, preferred_element_type=F32) for blk in blks]
    carry = jnp.zeros((N_GATE_ROWS, 1), F32)
    pad = jnp.zeros((LANES - N_GATE_ROWS, LANES), F32)
    for blk, rb in zip(blks, rbs):
        cr = (rb + carry) * LOG2E
        cr_ref[:, blk] = cr
        cc_ref[0, blk, :] = jnp.concatenate([cr, pad], axis=0).T
        carry = carry + rb[:, LANES - 1:LANES]


def _fox_cumsum(grow, *, batch, seq):
    return pl.pallas_call(
        _fox_cumsum_kernel,
        grid=(batch,),
        in_specs=[pl.BlockSpec((N_GATE_ROWS, seq), lambda b: (0, b))],
        out_specs=[pl.BlockSpec((1, seq, LANES), lambda b: (b, 0, 0)),
                   pl.BlockSpec((N_GATE_ROWS, seq), lambda b: (0, b))],
        out_shape=[jax.ShapeDtypeStruct((batch, seq, LANES), F32),
                   jax.ShapeDtypeStruct((N_GATE_ROWS, batch * seq), F32)],
        compiler_params=pltpu.CompilerParams(dimension_semantics=("arbitrary",)),
        name="fox_cumsum",
    )(grow)


def _fox_prompt_kernel(q_ref, k_ref, vt_ref, cc_ref, cr_ref, o_ref, qs_ref, m_ref, l_ref, acc_ref, *, blk):
    i = pl.program_id(1)
    n_pairs = H_F // 2
    lane = lax.broadcasted_iota(jnp.int32, (blk, LANES), 1)
    low_half = lane < DH_F
    s_idx = lax.broadcasted_iota(jnp.int32, (blk, blk), 0)
    t_idx = lax.broadcasted_iota(jnp.int32, (blk, blk), 1)
    gate0 = N_GATE_ROWS - H_F
    q_cols = pl.ds(pl.multiple_of(i * blk, blk), blk)

    for pair in range(n_pairs):
        slab = slice(pair * LANES, (pair + 1) * LANES)
        q_pair = q_ref[:, slab]
        zero = jnp.zeros_like(q_pair)
        qs_ref[pair, 0:blk, :] = jnp.where(low_half, q_pair, zero)
        qs_ref[pair, blk:2 * blk, :] = jnp.where(low_half, zero, q_pair)
    m_ref[...] = jnp.full(m_ref.shape, -jnp.inf, F32)
    l_ref[...] = jnp.zeros(l_ref.shape, F32)
    acc_ref[...] = jnp.zeros(acc_ref.shape, F32)

    pairs = range(n_pairs)

    def key_rows(j):
        return pl.ds(pl.multiple_of(j * blk, blk), blk)

    def scores(j):
        return tuple(lax.dot_general(k_ref[key_rows(j), pair * LANES:(pair + 1) * LANES], qs_ref[pair], NT,
                                     preferred_element_type=F32) for pair in pairs)

    def softmax(j, raw, on_diagonal):
        pb, alpha = [], []
        for pair in pairs:
            ck = jnp.concatenate(
                [jnp.broadcast_to(cc_ref[key_rows(j), gate0 + 2 * pair + e:gate0 + 2 * pair + e + 1], (blk, blk))
                 for e in range(2)], axis=1)
            cq = jnp.concatenate(
                [cr_ref[gate0 + 2 * pair + e:gate0 + 2 * pair + e + 1, q_cols] for e in range(2)], axis=1)
            s = raw[pair] - ck
            if on_diagonal:
                s = jnp.where(jnp.concatenate([s_idx <= t_idx] * 2, axis=1), s, -jnp.inf)
            m_old = m_ref[pair]
            m_new = jnp.maximum(m_old, jnp.max(s, axis=0, keepdims=True) + cq)
            pb.append(jnp.exp2((s - (m_new - cq)).astype(BF16)))
            alpha.append(jnp.exp2(m_old - m_new))
            m_ref[pair] = m_new
        return pb, alpha

    ones_rows = jnp.ones((ROW_GROUP, blk), BF16)

    def values(j, pb, alpha):
        for pair in pairs:
            for e in range(2):
                own = slice(pair * LANES + e * DH_F, pair * LANES + (e + 1) * DH_F)
                lanes = slice(e * blk, (e + 1) * blk)
                lhs = jnp.concatenate([vt_ref[0, own, key_rows(j)], ones_rows], axis=0)
                pv = jnp.dot(lhs, pb[pair][:, lanes], preferred_element_type=F32)
                a = alpha[pair][:, lanes]
                acc_ref[own, :] = a * acc_ref[own, :] + pv[0:DH_F, :]
                l_ref[pair, :, lanes] = a * l_ref[pair, :, lanes] + pv[DH_F:DH_F + 1, :]

    def key_block(j, on_diagonal):
        pb, alpha = softmax(j, scores(j), on_diagonal)
        values(j, pb, alpha)

    def loop_body(j, carry):
        key_block(j, False)
        return carry

    lax.fori_loop(0, i, loop_body, 0)
    key_block(i, True)
    inv_l = jnp.concatenate(
        [jnp.broadcast_to(1.0 / l_ref[h // 2][:, (h % 2) * blk:(h % 2 + 1) * blk], (DH_F, blk)) for h in range(H_F)],
        axis=0)
    o_ref[...] = (acc_ref[...] * inv_l).T.astype(o_ref.dtype)


def _fox_prompt(qf, kfr, vtb, ccol, crow, *, batch, seq):
    blk = FOX_BLOCK
    nq = seq // blk
    stat = pltpu.VMEM((H_F // 2, 1, 2 * blk), F32)
    return pl.pallas_call(
        functools.partial(_fox_prompt_kernel, blk=blk),
        grid=(batch, nq),
        in_specs=[pl.BlockSpec((blk, W_HEADS), lambda b, i: (b * nq + i, 0)),
                  pl.BlockSpec((seq, W_HEADS), lambda b, i: (b, 0)),
                  pl.BlockSpec((1, W_HEADS, seq), lambda b, i: (b, 0, 0)),
                  pl.BlockSpec((seq, LANES), lambda b, i: (b, 0)),
                  pl.BlockSpec((N_GATE_ROWS, seq), lambda b, i: (0, b))],
        out_specs=pl.BlockSpec((blk, W_HEADS), lambda b, i: (b * nq + i, 0)),
        out_shape=jax.ShapeDtypeStruct((batch * seq, W_HEADS), BF16),
        scratch_shapes=[pltpu.VMEM((H_F // 2, 2 * blk, LANES), BF16), stat, stat, pltpu.VMEM((W_HEADS, blk), F32)],
        compiler_params=pltpu.CompilerParams(dimension_semantics=("arbitrary", "arbitrary"),
                                             vmem_limit_bytes=VMEM_LIMIT),
        name="fox_prompt",
    )(qf, kfr, vtb, ccol.reshape(batch * seq, LANES), crow)


def _page_bias_kernel(pt_ref, lf_ref, o_ref):
    b = pl.program_id(0)
    n_pages = o_ref.shape[1]
    x = jnp.concatenate([lf_ref[pt_ref[b, p]] for p in range(n_pages)], axis=0)
    t_idx = lax.broadcasted_iota(jnp.int32, (LANES, LANES), 0)
    s_idx = lax.broadcasted_iota(jnp.int32, (LANES, LANES), 1)
    later = (t_idx > s_idx).astype(F32)
    within = jnp.dot(x, later, precision=HI, preferred_element_type=F32)
    total = jnp.sum(x, axis=-1, keepdims=True)
    run = jnp.zeros((H_F, LANES), F32)
    for p in range(n_pages - 1, -1, -1):
        rows = slice(p * H_F, (p + 1) * H_F)
        o_ref[0, p] = (within[rows] + run) * (-LOG2E)
        run = run + total[rows]


def _page_bias(page_table, lf_pool):
    batch, n_pages = page_table.shape
    return pl.pallas_call(
        _page_bias_kernel,
        grid_spec=pltpu.PrefetchScalarGridSpec(
            num_scalar_prefetch=1,
            grid=(batch,),
            in_specs=[_const_spec(lf_pool.shape)],
            out_specs=pl.BlockSpec((1, n_pages, H_F, LANES), lambda b, pt: (b, 0, 0, 0)),
        ),
        out_shape=jax.ShapeDtypeStruct((batch, n_pages, H_F, LANES), F32),
        compiler_params=pltpu.CompilerParams(dimension_semantics=("arbitrary",), vmem_limit_bytes=VMEM_LIMIT),
        name="page_bias",
    )(page_table, lf_pool)


def _paged_attention(pt_ref, q_ref, gr_ref, kn_ref, vn_ref, cp_ref, k_hbm, v_hbm, o_ref,
                     kbuf, vbuf, sem, acc_ref, *, n_groups, phases):
    b = pl.program_id(0)
    nb = pl.num_programs(0)
    group = PAGES_PER_GROUP
    t_new = q_ref.shape[0]
    n_rows = t_new * H_F

    def page_copies(bi, g, sl):
        cps = []
        for j in range(group):
            page = pt_ref[bi, g * group + j]
            cps.append(pltpu.make_async_copy(k_hbm.at[page], kbuf.at[sl, j], sem.at[sl, 0]))
            cps.append(pltpu.make_async_copy(v_hbm.at[page], vbuf.at[sl, j], sem.at[sl, 1]))
        return cps

    def start_group(bi, g, sl):
        for cp in page_copies(bi, g, sl):
            cp.start()

    @pl.when(b == 0)
    def _():
        for a in range(PAGE_SLOTS - 1):
            start_group(0, a, a)

    sub = lax.broadcasted_iota(jnp.int32, (H_F, W_HEADS), 0)
    lane = lax.broadcasted_iota(jnp.int32, (H_F, W_HEADS), 1)
    own_head = (lane // DH_F) == sub
    q = q_ref[...].astype(F32)
    qbd = jnp.concatenate(
        [jnp.where(own_head, jnp.broadcast_to(q[t:t + 1, :], (H_F, W_HEADS)), 0.0) for t in range(t_new)],
        axis=0).astype(BF16)

    gate0 = N_GATE_ROWS - H_F
    lf_new = gr_ref[0, gate0:N_GATE_ROWS, :]
    a_idx = lax.broadcasted_iota(jnp.int32, (t_new, t_new), 0)
    b_idx = lax.broadcasted_iota(jnp.int32, (t_new, t_new), 1)
    c_new = jnp.dot(lf_new, (a_idx <= b_idx).astype(F32), precision=HI, preferred_element_type=F32) * LOG2E
    cq = jnp.concatenate([c_new[:, t:t + 1] for t in range(t_new)], axis=0)

    acc_ref[...] = jnp.zeros_like(acc_ref)

    def body(g, carry, side_work):
        m_i, l_i = carry
        slot = g % PAGE_SLOTS
        ahead = g + (PAGE_SLOTS - 1)

        @pl.when(ahead < n_groups)
        def _():
            start_group(b, ahead, ahead % PAGE_SLOTS)

        @pl.when(jnp.logical_and(ahead >= n_groups, b + 1 < nb))
        def _():
            start_group(b + 1, ahead - n_groups, ahead % PAGE_SLOTS)

        for cp in page_copies(b, g, slot):
            cp.wait()
        side = side_work(g)

        kcat = jnp.concatenate([kbuf[slot, j].astype(BF16) for j in range(group)], axis=1)
        s = jnp.dot(qbd, kcat, preferred_element_type=F32)
        next(side, None)
        first = pl.multiple_of(g * group, group)
        cpg = cp_ref[0, pl.ds(first, group)]
        bias = jnp.concatenate(
            [jnp.broadcast_to(cpg[j][None], (t_new, H_F, LANES)).reshape(n_rows, LANES) for j in range(group)],
            axis=1)
        s = s + (cq - bias)
        m_new = jnp.maximum(m_i, jnp.max(s, axis=-1, keepdims=True))
        p = jnp.exp2(s - m_new)
        alpha = jnp.exp2(m_i - m_new)
        l_new = alpha * l_i + jnp.sum(p, axis=-1, keepdims=True)
        next(side, None)
        vcat = jnp.concatenate([vbuf[slot, j].T.astype(BF16) for j in range(group)], axis=0)
        pv = jnp.dot(p.astype(BF16), vcat, preferred_element_type=F32)
        next(side, None)
        acc_ref[...] = alpha * acc_ref[...] + pv
        return m_new, l_new

    carry = (jnp.full((n_rows, 1), -jnp.inf, F32), jnp.zeros((n_rows, 1), F32))
    start = 0
    for stop, side_work in phases:
        carry = lax.fori_loop(start, stop, functools.partial(body, side_work=side_work), carry)
        start = stop
    assert start == n_groups
    m_i, l_i = carry

    kn = kn_ref[...].astype(BF16)
    vn = vn_ref[...].astype(BF16)
    s = lax.dot_general(qbd, kn, NT, preferred_element_type=F32)
    ck = jnp.broadcast_to(c_new[None], (t_new, H_F, t_new)).reshape(n_rows, t_new)
    s = s + (cq - ck)
    r_idx = lax.broadcasted_iota(jnp.int32, (n_rows, t_new), 0)
    k_idx = lax.broadcasted_iota(jnp.int32, (n_rows, t_new), 1)
    s = jnp.where(k_idx <= r_idx // H_F, s, -jnp.inf)
    m_new = jnp.maximum(m_i, jnp.max(s, axis=-1, keepdims=True))
    p = jnp.exp2(s - m_new)
    alpha = jnp.exp2(m_i - m_new)
    l_fin = alpha * l_i + jnp.sum(p, axis=-1, keepdims=True)
    acc = alpha * acc_ref[...] + jnp.dot(p.astype(BF16), vn, preferred_element_type=F32)
    out = acc / l_fin
    o_ref[...] = jnp.concatenate(
        [jnp.sum(jnp.where(own_head, out[t * H_F:(t + 1) * H_F, :], 0.0), axis=0, keepdims=True)
         for t in range(t_new)], axis=0).astype(o_ref.dtype)


def _merge_residual(x_ref, hm_ref, hf_ref, wo_ref, g2_ref):
    x1 = (x_ref[...]
          + jnp.dot(hm_ref[...].astype(BF16), wo_ref[0:W_HEADS, :], preferred_element_type=F32)
          + jnp.dot(hf_ref[...].astype(BF16), wo_ref[W_HEADS:2 * W_HEADS, :], preferred_element_type=F32))
    return x1, _rms(x1, g2_ref[...]).astype(BF16)


def _ffn_sample_kernel(pt_ref, x_ref, hm_ref, hf_ref, wo_ref, g2_ref, wg_ref, wu_ref, wd_ref, g3_ref,
                       q_ref, gr_ref, kn_ref, vn_ref, cp_ref, k_hbm, v_hbm, y_ref, o_ref,
                       h_ref, x2_ref, act_ref, kbuf, vbuf, sem, acc_ref, *, n_groups, final_norm):
    x1, h = _merge_residual(x_ref, hm_ref, hf_ref, wo_ref, g2_ref)
    h_ref[...] = h
    x2_ref[...] = x1
    d_ff = wg_ref.shape[1]
    up_groups = d_ff // FFN_CHUNK
    down_groups = d_ff // DOWN_CHUNK
    tail = d_ff - down_groups * DOWN_CHUNK
    assert up_groups + down_groups == n_groups and tail in (0, FFN_CHUNK)

    def up_chunk(c):
        cols = pl.ds(pl.multiple_of(c * FFN_CHUNK, FFN_CHUNK), FFN_CHUNK)
        hh = h_ref[...]
        gate = jnp.dot(hh, wg_ref[:, cols], preferred_element_type=F32)
        yield
        up = jnp.dot(hh, wu_ref[:, cols], preferred_element_type=F32)
        act_ref[:, cols] = (gate * _sigmoid(gate) * up).astype(BF16)
        yield

    def last_up_chunk(c):
        yield from up_chunk(c)
        cols = pl.ds(pl.multiple_of(c * FFN_CHUNK, FFN_CHUNK), FFN_CHUNK)
        x2_ref[...] += jnp.dot(act_ref[:, cols], wd_ref[cols, :], preferred_element_type=F32)
        yield

    def down_slice(g):
        cols = pl.ds(pl.multiple_of((g - up_groups) * DOWN_CHUNK, DOWN_CHUNK), DOWN_CHUNK)
        yield
        x2_ref[...] += jnp.dot(act_ref[:, cols], wd_ref[cols, :], preferred_element_type=F32)
        yield

    if tail:
        phases = [(up_groups - 1, up_chunk), (up_groups, last_up_chunk), (n_groups, down_slice)]
    else:
        phases = [(up_groups, up_chunk), (n_groups, down_slice)]
    _paged_attention(pt_ref, q_ref, gr_ref, kn_ref, vn_ref, cp_ref, k_hbm, v_hbm, o_ref, kbuf, vbuf, sem, acc_ref,
                     n_groups=n_groups, phases=phases)
    x2 = x2_ref[...]
    y_ref[...] = _rms(x2, g3_ref[...]) if final_norm else x2


def _merge_ffn_and_fox_sample(x2d, hm, hf, wo, g2, wg, wu, wd, g3, page_table, qf, grow3, k_new, v_new, page_bias,
                              k_pool, v_pool, *, final_norm):
    rows_total, d_model = x2d.shape
    batch, n_pages = page_table.shape
    t_new = qf.shape[0] // batch
    n_groups = n_pages // PAGES_PER_GROUP
    tm = rows_total // batch
    d_ff = wg.shape[1]
    assert n_pages % PAGES_PER_GROUP == 0 and n_groups % PAGE_SLOTS == 0
    assert rows_total % batch == 0 and tm % 16 == 0 and d_ff % FFN_CHUNK == 0 and W_HEADS % FFN_CHUNK == 0
    page_rows, page_len = k_pool.shape[1], k_pool.shape[2]
    row = lambda w: pl.BlockSpec((tm, w), lambda b, pt: (b, 0))
    tok = lambda w: pl.BlockSpec((t_new, w), lambda b, pt: (b, 0))
    return pl.pallas_call(
        functools.partial(_ffn_sample_kernel, n_groups=n_groups, final_norm=final_norm),
        grid_spec=pltpu.PrefetchScalarGridSpec(
            num_scalar_prefetch=1,
            grid=(batch,),
            in_specs=[row(d_model), row(W_HEADS), row(W_HEADS), _const_spec(wo.shape), _const_spec(g2.shape),
                      _const_spec(wg.shape), _const_spec(wu.shape), _const_spec(wd.shape), _const_spec(g3.shape),
                      tok(W_HEADS),
                      pl.BlockSpec((1, N_GATE_ROWS, t_new), lambda b, pt: (b, 0, 0)),
                      tok(W_HEADS), tok(W_HEADS),
                      pl.BlockSpec((1, n_pages, H_F, LANES), lambda b, pt: (b, 0, 0, 0)),
                      pl.BlockSpec(memory_space=pl.ANY), pl.BlockSpec(memory_space=pl.ANY)],
            out_specs=[row(d_model), tok(W_HEADS)],
            scratch_shapes=[pltpu.VMEM((tm, d_model), BF16),
                            pltpu.VMEM((tm, d_model), F32),
                            pltpu.VMEM((tm, d_ff), BF16),
                            pltpu.VMEM((PAGE_SLOTS, PAGES_PER_GROUP, page_rows, page_len), F32),
                            pltpu.VMEM((PAGE_SLOTS, PAGES_PER_GROUP, page_rows, page_len), F32),
                            pltpu.SemaphoreType.DMA((PAGE_SLOTS, 2)),
                            pltpu.VMEM((t_new * H_F, W_HEADS), F32)],
        ),
        out_shape=[jax.ShapeDtypeStruct((rows_total, d_model), F32),
                   jax.ShapeDtypeStruct((batch * t_new, W_HEADS), F32)],
        compiler_params=pltpu.CompilerParams(dimension_semantics=("arbitrary",), vmem_limit_bytes=VMEM_LIMIT_WEIGHTS),
        name="ffn_and_fox_sample",
    )(page_table, x2d, hm, hf, wo, g2, wg, wu, wd, g3, qf, grow3, k_new, v_new, page_bias, k_pool, v_pool)


def _ffn_kernel(x_ref, hm_ref, hf_ref, wo_ref, g2_ref, wg_ref, wu_ref, wd_ref, g3_ref, y_ref, *, final_norm):
    x1, h = _merge_residual(x_ref, hm_ref, hf_ref, wo_ref, g2_ref)
    gate = jnp.dot(h, wg_ref[...], preferred_element_type=F32)
    up = jnp.dot(h, wu_ref[...], preferred_element_type=F32)
    act = (gate * _sigmoid(gate) * up).astype(BF16)
    x2 = x1 + jnp.dot(act, wd_ref[...], preferred_element_type=F32)
    y_ref[...] = _rms(x2, g3_ref[...]) if final_norm else x2


def _merge_ffn(x2d, hm, hf, wo, g2, wg, wu, wd, g3, *, final_norm):
    rows_total, d_model = x2d.shape
    tm = min(FFN_ROWS, rows_total)
    row_spec = lambda w: pl.BlockSpec((tm, w), lambda i: (i, 0))
    return pl.pallas_call(
        functools.partial(_ffn_kernel, final_norm=final_norm),
        grid=(rows_total // tm,),
        in_specs=[row_spec(d_model), row_spec(W_HEADS), row_spec(W_HEADS), _const_spec(wo.shape),
                  _const_spec(g2.shape), _const_spec(wg.shape), _const_spec(wu.shape), _const_spec(wd.shape),
                  _const_spec(g3.shape)],
        out_specs=row_spec(d_model),
        out_shape=jax.ShapeDtypeStruct((rows_total, d_model), F32),
        compiler_params=pltpu.CompilerParams(dimension_semantics=("arbitrary",), vmem_limit_bytes=VMEM_LIMIT_WEIGHTS),
        name="merge_ffn",
    )(x2d, hm, hf, wo, g2, wg, wu, wd, g3)


def kernel(x_prompt, x_sample, cache_fox_k, cache_fox_v, cache_fox_logf, page_table, state_mlstm_C,
           state_mlstm_n, state_mlstm_m, norm_mix_g, w_in, b_m_igate, b_m_fgate, b_f_fgate, mlstm_head_g,
           w_out, norm_ffn_g, w_gate, w_up, w_down, norm_final_g):
    depth = w_in.shape[0]
    batch, seq, d_model = x_prompt.shape
    dec_batch, dec_seq, _ = x_sample.shape
    n_pool, page_size = cache_fox_k.shape[1], cache_fox_k.shape[2]
    xp = x_prompt.reshape(batch * seq, d_model)
    xs = x_sample.reshape(dec_batch * dec_seq, d_model)
    g_final = norm_final_g.reshape(1, d_model)
    pk, pv, plf, pc, pn, pm = [], [], [], [], [], []
    sk, sv, slf, sc, sn, sm = [], [], [], [], [], []
    gate0 = N_GATE_ROWS - H_F
    o_gm = 4 * W_HEADS
    o_qf = o_gm + 2 * H_M
    o_gf = o_qf + 3 * W_HEADS
    for l in range(depth):
        wt = jnp.swapaxes(w_in[l], 0, 1)
        wt_main = jnp.concatenate([wt[0:o_gm], wt[o_qf:o_qf + W_HEADS]], axis=0).astype(BF16)
        wt_kv = wt[o_qf + W_HEADS:o_gf].astype(BF16)
        wt_g = jnp.concatenate([wt[o_gm:o_qf], wt[o_gf:o_gf + H_F],
                                jnp.zeros((LANES - N_GATE_ROWS, d_model), F32)], axis=0).astype(BF16)
        bias = jnp.concatenate([b_m_igate[l], b_m_fgate[l], b_f_fgate[l],
                                jnp.zeros((LANES - N_GATE_ROWS,), F32)]).astype(F32)
        bcol = bias.reshape(1, LANES)
        g_mix = norm_mix_g[l].reshape(1, d_model)
        g_ffn = norm_ffn_g[l].reshape(1, d_model)
        head_g = mlstm_head_g[l].reshape(1, W_HEADS)
        wo = w_out[l].astype(BF16)
        wg = w_gate[l].astype(BF16)
        wu = w_up[l].astype(BF16)
        wd = w_down[l].astype(BF16)

        qm, km, vmt, omt, qf, kt, vt, kfr, vtb, gcol, grow = _project(
            xp, g_mix, wt_main, wt_kv, wt_g, bcol, batch=batch, seq=seq, kv_transposed=True, act_dtype=BF16)
        chunk = min(LANES, seq)
        grow3 = grow.reshape(N_GATE_ROWS, batch, seq).transpose(1, 0, 2)
        hm, c_p, n_p, m_p = _mlstm_prompt(
            qm, km, vmt, omt, gcol, grow3, head_g,
            jnp.zeros((batch, H_M, DH_M, DH_M), F32), jnp.zeros((batch, H_M, DH_M), F32),
            jnp.zeros((batch, H_M), F32), batch=batch, seq=seq, chunk=chunk, out_dtype=BF16)
        ccol, crow = _fox_cumsum(grow, batch=batch, seq=seq)
        hf = _fox_prompt(qf, kfr, vtb, ccol, crow, batch=batch, seq=seq)
        hm_p, hf_p = hm, hf
        pk.append(kt.reshape(batch, H_F, DH_F, seq).transpose(0, 3, 1, 2))
        pv.append(vt.reshape(batch, H_F, DH_F, seq).transpose(0, 3, 1, 2))
        plf.append(grow3[:, gate0:, :].transpose(0, 2, 1))
        pc.append(c_p); pn.append(n_p); pm.append(m_p)

        qm, km, vm, om, qf, k_new, v_new, gcol, grow = _project(
            xs, g_mix, wt_main, wt_kv, wt_g, bcol, batch=dec_batch, seq=dec_seq, kv_transposed=False,
            act_dtype=F32)
        grow3 = grow.reshape(N_GATE_ROWS, dec_batch, dec_seq).transpose(1, 0, 2)
        hm, c_s, n_s, m_s = _mlstm(
            qm, km, vm, om, gcol, grow3, head_g,
            state_mlstm_C[l].astype(F32), state_mlstm_n[l].astype(F32), state_mlstm_m[l].astype(F32),
            batch=dec_batch, seq=dec_seq, chunk=dec_seq, out_dtype=F32)
        k_pool = cache_fox_k[l].transpose(0, 2, 3, 1).reshape(n_pool, W_HEADS, page_size)
        v_pool = cache_fox_v[l].transpose(0, 2, 3, 1).reshape(n_pool, W_HEADS, page_size)
        lf_pool = cache_fox_logf[l].transpose(0, 2, 1)
        page_bias = _page_bias(page_table, lf_pool)
        xp, hf = _merge_ffn_and_fox_sample(
            xp, hm_p, hf_p, wo, g_ffn, wg, wu, wd, g_final, page_table, qf, grow3, k_new, v_new, page_bias,
            k_pool, v_pool, final_norm=(l == depth - 1))
        xs = _merge_ffn(xs, hm, hf, wo, g_ffn, wg, wu, wd, g_final, final_norm=(l == depth - 1))
        sk.append(k_new.reshape(dec_batch, dec_seq, H_F, DH_F))
        sv.append(v_new.reshape(dec_batch, dec_seq, H_F, DH_F))
        slf.append(grow3[:, gate0:, :].transpose(0, 2, 1))
        sc.append(c_s); sn.append(n_s); sm.append(m_s)

    st = lambda a, ref: jnp.stack(a, axis=0).astype(ref.dtype)
    return (xp.reshape(batch, seq, d_model), xs.reshape(dec_batch, dec_seq, d_model),
            st(pk, cache_fox_k), st(pv, cache_fox_v), st(plf, cache_fox_logf),
            st(pc, state_mlstm_C), st(pn, state_mlstm_n), st(pm, state_mlstm_m),
            st(sk, cache_fox_k), st(sv, cache_fox_v), st(slf, cache_fox_logf),
            st(sc, state_mlstm_C), st(sn, state_mlstm_n), st(sm, state_mlstm_m))
```
